```python
import jax, jax.numpy as jnp
from jax import lax
import numpy as np

D_MODEL = 1024
BATCH = 16
SEQ = 2048
DEPTH = 1
DEC_BATCH = 32
DEC_SEQ = 4
PAST_LEN = 16384
PAGE_SIZE = 128

SB_HEADS = 8
SB_HEAD_DIM = 64
SB_WIDTH = SB_HEADS * SB_HEAD_DIM
Q_BLOCK = 128
SB_BIAS_START = -2.0
RET_HEADS = 4
RET_QK_DIM = 128
RET_V_DIM = 256
RET_QK_WIDTH = RET_HEADS * RET_QK_DIM
RET_V_WIDTH = RET_HEADS * RET_V_DIM
RET_CHUNK = 128
ROPE_BASE = 10000.0
PEER_HEADS = 8
PEER_HALF = 128
PEER_QUERY_DIM = 2 * PEER_HALF
N_KEYS = 128
N_EXPERTS = N_KEYS * N_KEYS
PEER_TOPK = 16
PLE_DIM = 256
RMS_EPS = 1e-6

PROJ_SIZES = (SB_WIDTH, SB_WIDTH, SB_WIDTH, RET_QK_WIDTH, RET_QK_WIDTH, RET_V_WIDTH, RET_V_WIDTH, D_MODEL, D_MODEL)
PROJ_SPLITS = tuple(int(s) for s in np.cumsum(PROJ_SIZES)[:-1])
PROJ_WIDTH = int(sum(PROJ_SIZES))

kernel_name = "stickbreak_retention_peer_hybrid_step"

F32 = jnp.float32


def rms_norm(x, g):
    xf = x.astype(F32)
    y = xf * lax.rsqrt(jnp.mean(xf * xf, axis=-1, keepdims=True) + RMS_EPS)
    return (y * g.astype(F32)).astype(x.dtype)


def rotary(x, pos):
    half = x.shape[-1] // 2
    inv = ROPE_BASE ** (-jnp.arange(half, dtype=F32) / half)
    ang = pos.astype(F32)[:, None] * inv[None, :]
    cos = jnp.cos(ang)[None, :, None, :]
    sin = jnp.sin(ang)[None, :, None, :]
    xf = x.astype(F32)
    x1, x2 = xf[..., :half], xf[..., half:]
    return jnp.concatenate([x1 * cos - x2 * sin, x1 * sin + x2 * cos], axis=-1).astype(x.dtype)


def ret_log_gamma():
    return jnp.log1p(-jnp.exp2(-5.0 - jnp.arange(RET_HEADS, dtype=F32)))


def mixer_inputs(x, pos, norm_g, w_in, q_g, k_g):
    B, T, _ = x.shape
    h = rms_norm(x, norm_g)
    proj = h @ w_in
    qa, ka, va, qb, kb, vb, gb, ga_pre, gb_pre = jnp.split(proj, PROJ_SPLITS, axis=-1)
    qa = rms_norm(qa.reshape(B, T, SB_HEADS, SB_HEAD_DIM), q_g)
    ka = rms_norm(ka.reshape(B, T, SB_HEADS, SB_HEAD_DIM), k_g)
    va = va.reshape(B, T, SB_HEADS, SB_HEAD_DIM)
    qb = rotary(qb.reshape(B, T, RET_HEADS, RET_QK_DIM), pos)
    kb = rotary(kb.reshape(B, T, RET_HEADS, RET_QK_DIM), pos) * (RET_QK_DIM ** -0.5)
    vb = vb.reshape(B, T, RET_HEADS, RET_V_DIM)
    return qa, ka, va, qb, kb, vb, gb, ga_pre, gb_pre


def sb_attend(q, k, v, q_pos, k_pos, bias):
    z = jnp.einsum('bqhd,bkhd->bhqk', q.astype(F32), k.astype(F32)) * (q.shape[-1] ** -0.5)
    z = z + bias.astype(F32)[None, :, None, None]
    mask = k_pos[None, :] < q_pos[:, None]
    log_not = jnp.where(mask, jax.nn.log_sigmoid(-z), 0.0)
    suffix = lax.cumsum(log_not, axis=3, reverse=True) - log_not
    a = jnp.where(mask, jnp.exp(jax.nn.log_sigmoid(z) + suffix), 0.0)
    return jnp.einsum('bhqk,bkhd->bqhd', a, v.astype(F32))


def sb_prompt(q, k, v, bias):
    B, S, H, D = q.shape
    nb = S // Q_BLOCK
    qb = q.reshape(B, nb, Q_BLOCK, H, D).transpose(1, 0, 2, 3, 4)
    posb = jnp.arange(S).reshape(nb, Q_BLOCK)
    k_pos = jnp.arange(S)
    out = lax.map(lambda a: sb_attend(a[0], k, v, a[1], k_pos, bias), (qb, posb))
    return out.transpose(1, 0, 2, 3, 4).reshape(B, S, H, D)


def retention_chunk(state, q, k, v):
    C = q.shape[1]
    lg = ret_log_gamma()
    idx = jnp.arange(C, dtype=F32)
    diff = idx[:, None] - idx[None, :]
    causal = diff >= 0
    decay = jnp.where(causal[None], jnp.exp(jnp.where(causal, diff, 0.0)[None] * lg[:, None, None]), 0.0)
    qf, kf, vf = q.astype(F32), k.astype(F32), v.astype(F32)
    scores = jnp.einsum('bihd,bjhd->bhij', qf, kf) * decay[None]
    inner = jnp.einsum('bhij,bjhe->bihe', scores, vf)
    q_decay = jnp.exp((idx[:, None] + 1.0) * lg[None, :])
    cross = jnp.einsum('bihd,bhde->bihe', qf, state) * q_decay[None, :, :, None]
    k_decay = jnp.exp((C - 1.0 - idx)[:, None] * lg[None, :])
    new_state = jnp.exp(C * lg)[None, :, None, None] * state + jnp.einsum('bjhd,bjhe->bhde', kf * k_decay[None, :, :, None], vf)
    return new_state, inner + cross


def retention_prompt(q, k, v):
    B, S, H, DK = q.shape
    nc = S // RET_CHUNK
    to_chunks = lambda t: t.reshape(B, nc, RET_CHUNK, H, t.shape[-1]).transpose(1, 0, 2, 3, 4)
    init = jnp.zeros((B, H, DK, v.shape[-1]), F32)
    final, outs = lax.scan(lambda s, xs: retention_chunk(s, *xs), init, (to_chunks(q), to_chunks(k), to_chunks(v)))
    return final, outs.transpose(1, 0, 2, 3, 4).reshape(B, S, H, v.shape[-1])


def peer_block(h, w_q, sub_keys, u_tab, v_tab):
    T = h.shape[0]
    q = (h @ w_q).reshape(T, PEER_HEADS, 2, PEER_HALF)
    s = jnp.einsum('thcd,hcnd->thcn', q.astype(F32), sub_keys.astype(F32))
    s1, i1 = lax.top_k(s[:, :, 0], PEER_TOPK)
    s2, i2 = lax.top_k(s[:, :, 1], PEER_TOPK)
    cand = (s1[..., :, None] + s2[..., None, :]).reshape(T, PEER_HEADS, PEER_TOPK * PEER_TOPK)
    top, flat = lax.top_k(cand, PEER_TOPK)
    e = jnp.take_along_axis(i1, flat // PEER_TOPK, axis=-1) * N_KEYS + jnp.take_along_axis(i2, flat % PEER_TOPK, axis=-1)
    g = jax.nn.softmax(top, axis=-1)
    u = u_tab[e]
    act = jax.nn.gelu(jnp.einsum('thkd,td->thk', u, h).astype(F32))
    return jnp.einsum('thk,thkd->td', (g * act).astype(h.dtype), v_tab[e])


def peer_apply(h, block, w_q, sub_keys, u_tab, v_tab):
    B, T, D = h.shape
    hb = h.reshape(-1, block, D)
    out = lax.map(lambda t: peer_block(t, w_q, sub_keys, u_tab, v_tab), hb)
    return out.reshape(B, T, D)


def merge_and_channel(x, o_a, o_b, gb, ga_pre, gb_pre, p_emb, block, ret_norm_g, w_a_up, w_b_up, w_out,
                      norm_ffn_g, peer_w_q, peer_sub_keys, peer_u, peer_v, norm_ple_g, w_ple_gate, w_ple):
    B, T, _ = x.shape
    y_a = o_a.astype(x.dtype).reshape(B, T, SB_WIDTH) @ w_a_up
    ob = rms_norm(o_b, ret_norm_g).astype(x.dtype) * jax.nn.silu(gb.reshape(B, T, RET_HEADS, RET_V_DIM))
    y_b = ob.reshape(B, T, RET_V_WIDTH) @ w_b_up
    merged = jax.nn.sigmoid(ga_pre) * y_a + jax.nn.sigmoid(gb_pre) * y_b
    x = x + merged @ w_out
    h2 = rms_norm(x, norm_ffn_g)
    x = x + peer_apply(h2, block, peer_w_q, peer_sub_keys, peer_u, peer_v)
    gate = jax.nn.sigmoid(rms_norm(x, norm_ple_g) @ w_ple_gate)
    return x + gate * (p_emb @ w_ple)


def setup_inputs(seed: int = 0) -> dict:
    key = jax.random.key(seed)
    ks = jax.random.split(key, 32)
    n_pages = PAST_LEN // PAGE_SIZE
    n_used = DEC_BATCH * n_pages
    n_phys = n_used + max(1, n_used // 4)
    nrm = lambda k, shape, scale: jax.random.normal(k, shape, F32) * scale
    gain = lambda k, shape: 1.0 + 0.02 * jax.random.normal(k, shape, F32)
    page_table = jax.random.permutation(ks[7], n_phys)[:n_used].reshape(DEC_BATCH, n_pages).astype(jnp.int32)
    sb_bias = (SB_BIAS_START - jnp.arange(SB_HEADS, dtype=F32))[None, :] + 0.1 * jax.random.normal(ks[24], (DEPTH, SB_HEADS), F32)
    return {
        "x_prompt": nrm(ks[0], (BATCH, SEQ, D_MODEL), 1.0),
        "x_sample": nrm(ks[1], (DEC_BATCH, DEC_SEQ, D_MODEL), 1.0),
        "cache_k": nrm(ks[2], (DEPTH, n_phys, PAGE_SIZE, SB_HEADS, SB_HEAD_DIM), 1.0),
        "cache_v": nrm(ks[3], (DEPTH, n_phys, PAGE_SIZE, SB_HEADS, SB_HEAD_DIM), 1.0),
        "state_ret": nrm(ks[4], (DEPTH, DEC_BATCH, RET_HEADS, RET_QK_DIM, RET_V_DIM), 0.5),
        "page_table": page_table,
        "p_prompt": nrm(ks[5], (DEPTH, BATCH, SEQ, PLE_DIM), 1.0),
        "p_sample": nrm(ks[6], (DEPTH, DEC_BATCH, DEC_SEQ, PLE_DIM), 1.0),
        "norm_mix_g": gain(ks[8], (DEPTH, D_MODEL)),
        "w_in": nrm(ks[9], (DEPTH, D_MODEL, PROJ_WIDTH), D_MODEL ** -0.5),
        "q_norm_g": gain(ks[10], (DEPTH, SB_HEAD_DIM)),
        "k_norm_g": gain(ks[11], (DEPTH, SB_HEAD_DIM)),
        "sb_bias": sb_bias,
        "ret_norm_g": gain(ks[12], (DEPTH, RET_HEADS, RET_V_DIM)),
        "w_a_up": nrm(ks[13], (DEPTH, SB_WIDTH, D_MODEL), SB_WIDTH ** -0.5),
        "w_b_up": nrm(ks[14], (DEPTH, RET_V_WIDTH, D_MODEL), RET_V_WIDTH ** -0.5),
        "w_out": nrm(ks[15], (DEPTH, D_MODEL, D_MODEL), D_MODEL ** -0.5),
        "norm_ffn_g": gain(ks[16], (DEPTH, D_MODEL)),
        "peer_w_q": nrm(ks[17], (DEPTH, D_MODEL, PEER_HEADS * PEER_QUERY_DIM), D_MODEL ** -0.5),
        "peer_sub_keys": nrm(ks[18], (DEPTH, PEER_HEADS, 2, N_KEYS, PEER_HALF), PEER_HALF ** -0.5),
        "peer_u": nrm(ks[19], (DEPTH, N_EXPERTS, D_MODEL), D_MODEL ** -0.5),
        "peer_v": nrm(ks[20], (DEPTH, N_EXPERTS, D_MODEL), PEER_HEADS ** -0.5),
        "norm_ple_g": gain(ks[21], (DEPTH, D_MODEL)),
        "w_ple_gate": nrm(ks[22], (DEPTH, D_MODEL, D_MODEL), D_MODEL ** -0.5),
        "w_ple": nrm(ks[23], (DEPTH, PLE_DIM, D_MODEL), PLE_DIM ** -0.5),
    }


def reference(x_prompt, x_sample, cache_k, cache_v, state_ret, page_table, p_prompt, p_sample,
              norm_mix_g, w_in, q_norm_g, k_norm_g, sb_bias, ret_norm_g, w_a_up, w_b_up, w_out, norm_ffn_g,
              peer_w_q, peer_sub_keys, peer_u, peer_v, norm_ple_g, w_ple_gate, w_ple):
    S = x_prompt.shape[1]
    DB, T = x_sample.shape[0], x_sample.shape[1]
    past_len = page_table.shape[1] * cache_k.shape[2]
    pos_p = jnp.arange(S)
    pos_s = past_len + jnp.arange(T)
    k_pos_s = jnp.arange(past_len + T)
    xp, xs = x_prompt, x_sample
    kp, vp, sp, kn, vn, sn = [], [], [], [], [], []
    for i in range(DEPTH):
        tail = (ret_norm_g[i], w_a_up[i], w_b_up[i], w_out[i], norm_ffn_g[i], peer_w_q[i], peer_sub_keys[i],
                peer_u[i], peer_v[i], norm_ple_g[i], w_ple_gate[i], w_ple[i])
        qa, ka, va, qb, kb, vb, gb, ga_pre, gb_pre = mixer_inputs(xp, pos_p, norm_mix_g[i], w_in[i], q_norm_g[i], k_norm_g[i])
        o_a = sb_prompt(qa, ka, va, sb_bias[i])
        st_p, o_b = retention_prompt(qb, kb, vb)
        xp = merge_and_channel(xp, o_a, o_b, gb, ga_pre, gb_pre, p_prompt[i], Q_BLOCK, *tail)
        kp.append(ka.astype(cache_k.dtype))
        vp.append(va.astype(cache_v.dtype))
        sp.append(st_p.astype(state_ret.dtype))
        qa, ka, va, qb, kb, vb, gb, ga_pre, gb_pre = mixer_inputs(xs, pos_s, norm_mix_g[i], w_in[i], q_norm_g[i], k_norm_g[i])
        k_past = cache_k[i][page_table].reshape(DB, past_len, SB_HEADS, SB_HEAD_DIM)
        v_past = cache_v[i][page_table].reshape(DB, past_len, SB_HEADS, SB_HEAD_DIM)
        k_all = jnp.concatenate([k_past, ka.astype(k_past.dtype)], axis=1)
        v_all = jnp.concatenate([v_past, va.astype(v_past.dtype)], axis=1)
        o_a = sb_attend(qa, k_all, v_all, pos_s, k_pos_s, sb_bias[i])
        st_s, o_b = retention_chunk(state_ret[i].astype(F32), qb, kb, vb)
        xs = merge_and_channel(xs, o_a, o_b, gb, ga_pre, gb_pre, p_sample[i], T, *tail)
        kn.append(ka.astype(cache_k.dtype))
        vn.append(va.astype(cache_v.dtype))
        sn.append(st_s.astype(state_ret.dtype))
    return (xp, xs, jnp.stack(kp), jnp.stack(vp), jnp.stack(sp), jnp.stack(kn), jnp.stack(vn), jnp.stack(sn))
```

```python
import functools

import numpy as np
import jax
import jax.numpy as jnp
from jax import lax
from jax.experimental import pallas as pl
from jax.experimental.pallas import tpu as pltpu

F32 = jnp.float32
BF16 = jnp.bfloat16
I32 = jnp.int32
U32 = jnp.uint32
HIGHEST = lax.Precision.HIGHEST

RMS_EPS = 1e-6
ROPE_BASE = 10000.0
PEER_TOPK = 16
LANE = 128
SUBLANE = 8
VMEM_LIMIT = 56 * 1024 * 1024
NT_DIMS = (((1,), (1,)), ((), ()))
TN_DIMS = (((0,), (0,)), ((), ()))


def _cparams(sem, vmem=None):
    return pltpu.CompilerParams(dimension_semantics=sem, vmem_limit_bytes=vmem)


def _proj_kernel(x_ref, g_ref, w_ref, cs_ref, sn_ref, qg_ref, kg_ref, gm_ref, o_ref, hn_ref, *, kb_scale):
    j = pl.program_id(1)

    @pl.when(j == 0)
    def _():
        x = x_ref[...]
        ms = jnp.mean(x * x, axis=-1, keepdims=True)
        hn_ref[...] = (x * lax.rsqrt(ms + RMS_EPS) * g_ref[...]).astype(BF16)

    acc = jnp.dot(hn_ref[...], w_ref[...], preferred_element_type=F32)
    is_qk = j <= 1
    is_rot = (j == 3) | (j == 4)

    @pl.when(is_qk)
    def _():
        gain = jnp.where(j == 0, qg_ref[...], kg_ref[...])
        ms = jnp.dot(acc * acc, gm_ref[...], precision=HIGHEST, preferred_element_type=F32)
        o_ref[...] = acc * lax.rsqrt(ms + RMS_EPS) * gain

    @pl.when(is_rot)
    def _():
        scale = jnp.where(j == 4, kb_scale, 1.0).astype(F32)
        cs = cs_ref[...]
        sn = sn_ref[...]
        for h in range(o_ref.shape[1] // LANE):
            xh = acc[:, h * LANE:(h + 1) * LANE]
            r = pltpu.roll(xh, LANE // 2, 1)
            o_ref[:, h * LANE:(h + 1) * LANE] = (xh * cs + r * sn) * scale

    @pl.when(jnp.logical_not(is_qk | is_rot))
    def _():
        o_ref[...] = acc


def _proj_call(x2d, g, w_bf, cs, sn, qg, kg, gm, *, tm, pos_blocks, kb_scale):
    n, d = x2d.shape
    pw = w_bf.shape[1]
    tn = 512
    return pl.pallas_call(
        functools.partial(_proj_kernel, kb_scale=kb_scale),
        grid=(n // tm, pw // tn),
        in_specs=[
            pl.BlockSpec((tm, d), lambda i, j: (i, 0)),
            pl.BlockSpec((1, d), lambda i, j: (0, 0)),
            pl.BlockSpec((d, tn), lambda i, j: (0, j)),
            pl.BlockSpec((tm, LANE), lambda i, j: (i % pos_blocks, 0)),
            pl.BlockSpec((tm, LANE), lambda i, j: (i % pos_blocks, 0)),
            pl.BlockSpec((1, tn), lambda i, j: (0, 0)),
            pl.BlockSpec((1, tn), lambda i, j: (0, 0)),
            pl.BlockSpec((tn, tn), lambda i, j: (0, 0)),
        ],
        out_specs=pl.BlockSpec((tm, tn), lambda i, j: (i, j)),
        out_shape=jax.ShapeDtypeStruct((n, pw), F32),
        scratch_shapes=[pltpu.VMEM((tm, d), BF16)],
        compiler_params=_cparams(("arbitrary", "arbitrary"), VMEM_LIMIT),
    )(x2d, g, w_bf, cs, sn, qg, kg, gm)


def _sb_block(z, mask, tri, car):
    e = jnp.exp(-jnp.abs(z))
    sp = jnp.maximum(z, 0.0) + jnp.log1p(e)
    ln = -sp
    ls = z - sp
    if mask is not None:
        ln = jnp.where(mask, ln, 0.0)
    hi = ln.astype(BF16)
    lo = (ln - hi.astype(F32)).astype(BF16)
    cs = jnp.dot(hi, tri, preferred_element_type=F32) + jnp.dot(lo, tri, preferred_element_type=F32)
    a = jnp.exp(ls + cs[:, :LANE] + car)
    if mask is not None:
        a = jnp.where(mask, a, 0.0)
    return a, car + cs[:, LANE:]


def _sb_prompt_kernel(bias_ref, q_ref, k_ref, v_ref, tri_ref, o_ref, acc_ref, car_ref, *, scale):
    hp = pl.program_id(1)
    i = pl.program_id(2)
    tq = q_ref.shape[0]
    lane = lax.broadcasted_iota(I32, (1, LANE), 1)
    hmask = [lane < LANE // 2, lane >= LANE // 2]
    q = q_ref[...] * scale
    qm = [jnp.where(m, q, 0.0).astype(BF16) for m in hmask]
    tri = tri_ref[...]
    acc_ref[...] = jnp.zeros_like(acc_ref)
    car_ref[...] = jnp.zeros_like(car_ref)
    row = lax.broadcasted_iota(I32, (tq, LANE), 0)
    col = lax.broadcasted_iota(I32, (tq, LANE), 1)
    causal = col < row

    def block(kb, mask):
        start = pl.multiple_of(kb * LANE, LANE)
        kblk = k_ref[pl.ds(start, LANE), :].astype(BF16)
        vblk = v_ref[pl.ds(start, LANE), :]
        for hh in range(2):
            z = lax.dot_general(qm[hh], kblk, NT_DIMS, preferred_element_type=F32) + bias_ref[hp * 2 + hh]
            a, car = _sb_block(z, mask, tri, car_ref[hh])
            car_ref[hh] = car
            vm = jnp.where(hmask[hh], vblk, 0.0).astype(BF16)
            acc_ref[...] += jnp.dot(a.astype(BF16), vm, preferred_element_type=F32)

    block(i, causal)

    def body(jj, c):
        block(i - 1 - jj, None)
        return c

    lax.fori_loop(0, i, body, 0)
    o_ref[...] = acc_ref[...]


def _sb_prompt_call(proj, bias, tri, *, batch, seq, heads, head_dim):
    tq = LANE
    nq = seq // tq
    width = heads * head_dim
    cpb = width // LANE
    return pl.pallas_call(
        functools.partial(_sb_prompt_kernel, scale=head_dim ** -0.5),
        grid=(batch, heads // 2, nq),
        in_specs=[
            pl.BlockSpec(memory_space=pltpu.SMEM),
            pl.BlockSpec((tq, LANE), lambda b, hp, i: (b * nq + i, hp)),
            pl.BlockSpec((seq, LANE), lambda b, hp, i: (b, cpb + hp)),
            pl.BlockSpec((seq, LANE), lambda b, hp, i: (b, 2 * cpb + hp)),
            pl.BlockSpec((LANE, 2 * LANE), lambda b, hp, i: (0, 0)),
        ],
        out_specs=pl.BlockSpec((tq, LANE), lambda b, hp, i: (b * nq + i, hp)),
        out_shape=jax.ShapeDtypeStruct((batch * seq, width), F32),
        scratch_shapes=[pltpu.VMEM((tq, LANE), F32), pltpu.VMEM((2, tq, LANE), F32)],
        compiler_params=_cparams(("arbitrary", "arbitrary", "arbitrary")),
    )(bias, proj, proj, proj, tri)


def _sb_sample_kernel(pt_ref, q_ref, kn_ref, vn_ref, kc_ref, vc_ref, tri_ref, hm_ref, bias_ref, o_ref,
                      qbd_ref, pad_ref, acc_ref, car_ref, *, scale, n_new):
    p = pl.program_id(1)
    rows = qbd_ref.shape[0]
    heads = hm_ref.shape[0]
    tri = tri_ref[...]
    bias = bias_ref[...]

    def block(kblk, vblk, mask):
        z = lax.dot_general(qbd_ref[...], kblk.astype(BF16), NT_DIMS, preferred_element_type=F32) + bias
        a, car = _sb_block(z, mask, tri, car_ref[...])
        car_ref[...] = car
        acc_ref[...] += jnp.dot(a.astype(BF16), vblk.astype(BF16), preferred_element_type=F32)

    @pl.when(p == 0)
    def _():
        hm = hm_ref[...]
        for i in range(n_new):
            qi = q_ref[i:i + 1, :] * scale
            qbd_ref[i * heads:(i + 1) * heads, :] = jnp.where(hm > 0, qi, 0.0).astype(BF16)
        acc_ref[...] = jnp.zeros_like(acc_ref)
        car_ref[...] = jnp.zeros_like(car_ref)
        qrow = lax.broadcasted_iota(I32, (rows, LANE), 0) // heads
        col = lax.broadcasted_iota(I32, (rows, LANE), 1)
        mask = col < qrow
        pad_ref[...] = jnp.zeros_like(pad_ref)
        pad_ref[0, 0:kn_ref.shape[0], :] = kn_ref[...]
        pad_ref[1, 0:vn_ref.shape[0], :] = vn_ref[...]
        block(pad_ref[0], pad_ref[1], mask)

    block(kc_ref[...], vc_ref[...], None)

    @pl.when(p == pl.num_programs(1) - 1)
    def _():
        hm = hm_ref[...]
        o_ref[...] = jnp.zeros_like(o_ref)
        for i in range(n_new):
            blk = acc_ref[i * heads:(i + 1) * heads, :] * hm
            o_ref[i:i + 1, :] = jnp.sum(blk, axis=0, keepdims=True)


def _sb_sample_call(page_table, proj_s, cache_k3, cache_v3, tri, hm, bias_col, *, dec_batch, rows_pad, n_new,
                    heads, head_dim):
    n_pages = page_table.shape[1]
    width = heads * head_dim
    page = cache_k3.shape[1]
    rows = n_new * heads
    grid_spec = pltpu.PrefetchScalarGridSpec(
        num_scalar_prefetch=1,
        grid=(dec_batch, n_pages),
        in_specs=[
            pl.BlockSpec((rows_pad, width), lambda b, p, pt: (b, 0)),
            pl.BlockSpec((rows_pad, width), lambda b, p, pt: (b, 1)),
            pl.BlockSpec((rows_pad, width), lambda b, p, pt: (b, 2)),
            pl.BlockSpec((None, page, width), lambda b, p, pt: (pt[b * n_pages + n_pages - 1 - p], 0, 0)),
            pl.BlockSpec((None, page, width), lambda b, p, pt: (pt[b * n_pages + n_pages - 1 - p], 0, 0)),
            pl.BlockSpec((LANE, 2 * LANE), lambda b, p, pt: (0, 0)),
            pl.BlockSpec((heads, width), lambda b, p, pt: (0, 0)),
            pl.BlockSpec((rows, 1), lambda b, p, pt: (0, 0)),
        ],
        out_specs=pl.BlockSpec((rows_pad, width), lambda b, p, pt: (b, 0)),
        scratch_shapes=[
            pltpu.VMEM((rows, width), BF16),
            pltpu.VMEM((2, page, width), F32),
            pltpu.VMEM((rows, width), F32),
            pltpu.VMEM((rows, LANE), F32),
        ],
    )
    return pl.pallas_call(
        functools.partial(_sb_sample_kernel, scale=head_dim ** -0.5, n_new=n_new),
        grid_spec=grid_spec,
        out_shape=jax.ShapeDtypeStruct((dec_batch * rows_pad, width), F32),
        compiler_params=_cparams(("arbitrary", "arbitrary")),
    )(page_table.reshape(-1), proj_s, proj_s, proj_s, cache_k3, cache_v3, tri, hm, bias_col)


def _ret_kernel(q_ref, k_ref, v_ref, gb_ref, st0_ref, dec_ref, qd_ref, kd_ref, sd_ref, rg_ref,
                ob_ref, st_ref, state_ref, qp_ref, kp_ref, vp_ref, *, use_init):
    c = pl.program_id(2)
    rows = q_ref.shape[0]
    chunk = dec_ref.shape[0]

    @pl.when(c == 0)
    def _():
        if use_init:
            state_ref[...] = st0_ref[...]
        else:
            state_ref[...] = jnp.zeros_like(state_ref)

    if rows == chunk:
        q, k, v = q_ref[...], k_ref[...], v_ref[...]
    else:
        qp_ref[...] = jnp.zeros_like(qp_ref)
        kp_ref[...] = jnp.zeros_like(kp_ref)
        vp_ref[...] = jnp.zeros_like(vp_ref)
        qp_ref[0:rows, :] = q_ref[...]
        kp_ref[0:rows, :] = k_ref[...]
        vp_ref[0:rows, :] = v_ref[...]
        q, k, v = qp_ref[...], kp_ref[...], vp_ref[...]

    qb = q.astype(BF16)
    vb = v.astype(BF16)
    scores = lax.dot_general(qb, k.astype(BF16), NT_DIMS, preferred_element_type=F32) * dec_ref[...]
    inner = jnp.dot(scores.astype(BF16), vb, preferred_element_type=F32)
    st = state_ref[...]
    cross = jnp.dot(qb, st.astype(BF16), preferred_element_type=F32) * qd_ref[...]
    o = (inner + cross)[0:rows, :]
    kd = (k * kd_ref[...]).astype(BF16)
    new = sd_ref[...] * st + lax.dot_general(kd, vb, TN_DIMS, preferred_element_type=F32)
    state_ref[...] = new

    ms = jnp.mean(o * o, axis=-1, keepdims=True)
    gb = gb_ref[...]
    ob_ref[...] = (o * lax.rsqrt(ms + RMS_EPS) * rg_ref[...]) * (gb * jax.nn.sigmoid(gb))

    @pl.when(c == pl.num_programs(2) - 1)
    def _():
        st_ref[...] = new


def _ret_call(proj, state0, dec, qd, kd, sd, rg, *, batch, n_chunks, rows, heads, dk, dv, col0):
    chunk = dec.shape[1]
    qc = col0 // dk
    kc = qc + heads
    vc = (col0 + 2 * heads * dk) // dv
    gc = vc + heads
    use_init = state0 is not None
    if state0 is None:
        state0 = jnp.zeros((1, 1, dk, dv), F32)
        st_map = lambda b, h, c: (0, 0, 0, 0)
    else:
        st_map = lambda b, h, c: (b, h, 0, 0)
    pad_rows = chunk if rows != chunk else SUBLANE
    return pl.pallas_call(
        functools.partial(_ret_kernel, use_init=use_init),
        grid=(batch, heads, n_chunks),
        in_specs=[
            pl.BlockSpec((rows, dk), lambda b, h, c: (b * n_chunks + c, qc + h)),
            pl.BlockSpec((rows, dk), lambda b, h, c: (b * n_chunks + c, kc + h)),
            pl.BlockSpec((rows, dv), lambda b, h, c: (b * n_chunks + c, vc + h)),
            pl.BlockSpec((rows, dv), lambda b, h, c: (b * n_chunks + c, gc + h)),
            pl.BlockSpec((None, None, dk, dv), st_map),
            pl.BlockSpec((None, chunk, chunk), lambda b, h, c: (h, 0, 0)),
            pl.BlockSpec((None, chunk, 1), lambda b, h, c: (h, 0, 0)),
            pl.BlockSpec((None, chunk, 1), lambda b, h, c: (h, 0, 0)),
            pl.BlockSpec((None, 1, 1), lambda b, h, c: (h, 0, 0)),
            pl.BlockSpec((None, 1, dv), lambda b, h, c: (h, 0, 0)),
        ],
        out_specs=[
            pl.BlockSpec((rows, dv), lambda b, h, c: (b * n_chunks + c, h)),
            pl.BlockSpec((None, None, dk, dv), lambda b, h, c: (b, h, 0, 0)),
        ],
        out_shape=[
            jax.ShapeDtypeStruct((batch * n_chunks * rows, heads * dv), F32),
            jax.ShapeDtypeStruct((batch, heads, dk, dv), F32),
        ],
        scratch_shapes=[
            pltpu.VMEM((dk, dv), F32),
            pltpu.VMEM((pad_rows, dk), F32),
            pltpu.VMEM((pad_rows, dk), F32),
            pltpu.VMEM((pad_rows, dv), F32),
        ],
        compiler_params=_cparams(("arbitrary", "arbitrary", "arbitrary")),
    )(proj, proj, proj, proj, state0, dec, qd, kd, sd, rg)


def _merge_kernel(x_ref, oa_ref, ob_ref, ga0_ref, ga1_ref, gb0_ref, gb1_ref, wa_ref, wb_ref, wo_ref, gf_ref,
                  wq_ref, x1_ref, h2_ref, qp_ref):
    ya = jnp.dot(oa_ref[...].astype(BF16), wa_ref[...], preferred_element_type=F32)
    yb = jnp.dot(ob_ref[...].astype(BF16), wb_ref[...], preferred_element_type=F32)
    ga = jnp.concatenate([ga0_ref[...], ga1_ref[...]], axis=1)
    gb = jnp.concatenate([gb0_ref[...], gb1_ref[...]], axis=1)
    merged = jax.nn.sigmoid(ga) * ya + jax.nn.sigmoid(gb) * yb
    x1 = x_ref[...] + jnp.dot(merged.astype(BF16), wo_ref[...], preferred_element_type=F32)
    x1_ref[...] = x1
    ms = jnp.mean(x1 * x1, axis=-1, keepdims=True)
    h2 = x1 * lax.rsqrt(ms + RMS_EPS) * gf_ref[...]
    h2_ref[...] = h2
    qp_ref[...] = jnp.dot(h2.astype(BF16), wq_ref[...], preferred_element_type=F32)


def _merge_call(x2d, oa, ob, proj, wa, wb, wo, gf, wq, *, tm, gate_col0):
    n, d = x2d.shape
    half = d // 2
    g0 = gate_col0 // half
    row = lambda i: (i, 0)
    const = lambda i: (0, 0)
    return pl.pallas_call(
        _merge_kernel,
        grid=(n // tm,),
        in_specs=[
            pl.BlockSpec((tm, d), row),
            pl.BlockSpec((tm, oa.shape[1]), row),
            pl.BlockSpec((tm, ob.shape[1]), row),
            pl.BlockSpec((tm, half), lambda i: (i, g0)),
            pl.BlockSpec((tm, half), lambda i: (i, g0 + 1)),
            pl.BlockSpec((tm, half), lambda i: (i, g0 + 2)),
            pl.BlockSpec((tm, half), lambda i: (i, g0 + 3)),
            pl.BlockSpec(wa.shape, const),
            pl.BlockSpec(wb.shape, const),
            pl.BlockSpec(wo.shape, const),
            pl.BlockSpec((1, d), const),
            pl.BlockSpec(wq.shape, const),
        ],
        out_specs=[
            pl.BlockSpec((tm, d), row),
            pl.BlockSpec((tm, d), row),
            pl.BlockSpec((tm, wq.shape[1]), row),
        ],
        out_shape=[
            jax.ShapeDtypeStruct((n, d), F32),
            jax.ShapeDtypeStruct((n, d), F32),
            jax.ShapeDtypeStruct((n, wq.shape[1]), F32),
        ],
        compiler_params=_cparams(("arbitrary",), VMEM_LIMIT),
    )(x2d, oa, ob, proj, proj, proj, proj, wa, wb, wo, gf, wq)


def _topk_rows(s, k):
    n, t = s.shape
    row = lax.broadcasted_iota(I32, (n, t), 0)
    out_row = lax.broadcasted_iota(I32, (k, t), 0)
    vals = jnp.zeros((k, t), F32)
    idxs = jnp.zeros((k, t), I32)
    for r in range(k):
        m = jnp.max(s, axis=0, keepdims=True)
        am = jnp.min(jnp.where(s == m, row, n), axis=0, keepdims=True)
        vals = jnp.where(out_row == r, m, vals)
        idxs = jnp.where(out_row == r, am, idxs)
        s = jnp.where(row == am, -jnp.inf, s)
    return vals, idxs


def _select_rows(table, sel):
    out = jnp.zeros(sel.shape, table.dtype)
    for a in range(table.shape[0]):
        out = jnp.where(sel == a, table[a:a + 1, :], out)
    return out


def _peer_topk_kernel(qp_ref, keys_ref, e_ref, g_ref, *, n_keys):
    heads = keys_ref.shape[0]
    half = keys_ref.shape[3]
    k = PEER_TOPK
    e_rows, g_rows = [], []
    for h in range(heads):
        tops = []
        for c in range(2):
            col = (h * 2 + c) * half
            q = qp_ref[:, col:col + half].astype(BF16)
            st = lax.dot_general(keys_ref[h, c].astype(BF16), q, NT_DIMS, preferred_element_type=F32)
            tops.append(_topk_rows(st, k))
        (s1, i1), (s2, i2) = tops
        cand = jnp.concatenate([s1[a:a + 1, :] + s2 for a in range(k)], axis=0)
        top, flat = _topk_rows(cand, k)
        ia = _select_rows(i1, flat // k)
        ib = _select_rows(i2, flat % k)
        e_rows.append(ia * n_keys + ib)
        ex = jnp.exp(top - jnp.max(top, axis=0, keepdims=True))
        g_rows.append(ex / jnp.sum(ex, axis=0, keepdims=True))
    e_ref[...] = jnp.transpose(jnp.concatenate(e_rows, axis=0))
    g_ref[...] = jnp.transpose(jnp.concatenate(g_rows, axis=0))


def _peer_topk_call(qp, keys, *, tm):
    n = qp.shape[0]
    heads, _, n_keys, half = keys.shape
    slots = heads * PEER_TOPK
    return pl.pallas_call(
        functools.partial(_peer_topk_kernel, n_keys=n_keys),
        grid=(n // tm,),
        in_specs=[
            pl.BlockSpec((tm, qp.shape[1]), lambda i: (i, 0)),
            pl.BlockSpec(keys.shape, lambda i: (0, 0, 0, 0)),
        ],
        out_specs=[
            pl.BlockSpec((tm, slots), lambda i: (i, 0)),
            pl.BlockSpec((tm, slots), lambda i: (i, 0)),
        ],
        out_shape=[
            jax.ShapeDtypeStruct((n, slots), I32),
            jax.ShapeDtypeStruct((n, slots), F32),
        ],
        compiler_params=_cparams(("arbitrary",), VMEM_LIMIT),
    )(qp, keys)


def _pack_table(tab):
    n, d = tab.shape
    bits = lax.bitcast_convert_type(tab.astype(BF16), jnp.uint16).astype(U32)
    bits = bits.reshape(n // 2, 2, d)
    packed = bits[:, 0, :] | (bits[:, 1, :] << 16)
    return packed.reshape(n // 2 * (d // LANE), LANE)


def _expert_row(tab_ref, e):
    base = pl.multiple_of((e >> 1) * SUBLANE, SUBLANE)
    words = tab_ref[pl.ds(base, SUBLANE), :]
    shift = ((e & 1) ^ 1) * 16
    return lax.bitcast_convert_type((words << shift.astype(U32)) & jnp.uint32(0xFFFF0000), F32)


def _sublane_sums(ps):
    sub = lax.broadcasted_iota(I32, (SUBLANE, LANE), 0)
    dist = SUBLANE // 2
    while len(ps) > 1:
        keep = (sub & dist) == 0
        nxt = []
        for i in range(len(ps) // 2):
            a, b = ps[i], ps[i + len(ps) // 2]
            ta = a + pltpu.roll(a, SUBLANE - dist, 0)
            tb = b + pltpu.roll(b, dist, 0)
            nxt.append(jnp.where(keep, ta, tb))
        ps = nxt
        dist //= 2
    return ps[0]


def _peer_u_kernel(idx_ref, h_ref, g_ref, tab_ref, w_ref, q_ref):
    tb, slots = idx_ref.shape
    ones = jnp.ones((SUBLANE, LANE), BF16)
    sub = lax.broadcasted_iota(I32, (SUBLANE, LANE), 0)

    def group(t8, carry):
        def token(s, c):
            t = t8 * SUBLANE + s
            h = h_ref[pl.ds(pl.multiple_of(t * SUBLANE, SUBLANE), SUBLANE), :]
            for grp in range(slots // SUBLANE):
                ps = [_expert_row(tab_ref, idx_ref[t, grp * SUBLANE + r]) * h for r in range(SUBLANE)]
                q_ref[s, grp * SUBLANE:(grp + 1) * SUBLANE, :] = _sublane_sums(ps)
            return c

        lax.fori_loop(0, SUBLANE, token, 0)
        act = jnp.zeros((SUBLANE, slots), F32)
        for s in range(SUBLANE):
            qs = q_ref[s]
            hi = qs.astype(BF16)
            lo = (qs - hi.astype(F32)).astype(BF16)
            r = (lax.dot_general(ones, hi, NT_DIMS, preferred_element_type=F32)
                 + lax.dot_general(ones, lo, NT_DIMS, preferred_element_type=F32))
            act = jnp.where(sub == s, r, act)
        rows = pl.ds(pl.multiple_of(t8 * SUBLANE, SUBLANE), SUBLANE)
        w_ref[rows, :] = g_ref[rows, :] * jax.nn.gelu(act)
        return carry

    lax.fori_loop(0, tb // SUBLANE, group, 0)


def _peer_u_call(idx, h2, g, tab, *, tb):
    n, slots = idx.shape
    d = h2.shape[1]
    rows_per_tok = d // LANE
    h3 = h2.reshape(n * rows_per_tok, LANE)
    return pl.pallas_call(
        _peer_u_kernel,
        grid=(n // tb,),
        in_specs=[
            pl.BlockSpec((tb, slots), lambda i: (i, 0), memory_space=pltpu.SMEM),
            pl.BlockSpec((tb * rows_per_tok, LANE), lambda i: (i, 0)),
            pl.BlockSpec((tb, slots), lambda i: (i, 0)),
            pl.BlockSpec(tab.shape, lambda i: (0, 0), pipeline_mode=pl.Buffered(1)),
        ],
        out_specs=pl.BlockSpec((tb, slots), lambda i: (i, 0)),
        out_shape=jax.ShapeDtypeStruct((n, slots), F32),
        scratch_shapes=[pltpu.VMEM((SUBLANE, slots, LANE), F32)],
        compiler_params=_cparams(("arbitrary",), VMEM_LIMIT),
    )(idx, h3, g, tab)


def _peer_v_kernel(idx_ref, w_ref, x_ref, tab_ref, o_ref):
    tb, slots = idx_ref.shape
    n_acc = 4

    def token(t, c):
        accs = [jnp.zeros((SUBLANE, LANE), F32) for _ in range(n_acc)]
        for r in range(slots):
            accs[r % n_acc] = accs[r % n_acc] + _expert_row(tab_ref, idx_ref[t, r]) * w_ref[t, r]
        rows = pl.ds(pl.multiple_of(t * SUBLANE, SUBLANE), SUBLANE)
        o_ref[rows, :] = x_ref[rows, :] + ((accs[0] + accs[1]) + (accs[2] + accs[3]))
        return c

    lax.fori_loop(0, tb, token, 0)


def _peer_v_call(idx, w, x1, tab, *, tb):
    n, slots = idx.shape
    d = x1.shape[1]
    rows_per_tok = d // LANE
    x3 = x1.reshape(n * rows_per_tok, LANE)
    out = pl.pallas_call(
        _peer_v_kernel,
        grid=(n // tb,),
        in_specs=[
            pl.BlockSpec((tb, slots), lambda i: (i, 0), memory_space=pltpu.SMEM),
            pl.BlockSpec((tb, slots), lambda i: (i, 0), memory_space=pltpu.SMEM),
            pl.BlockSpec((tb * rows_per_tok, LANE), lambda i: (i, 0)),
            pl.BlockSpec(tab.shape, lambda i: (0, 0), pipeline_mode=pl.Buffered(1)),
        ],
        out_specs=pl.BlockSpec((tb * rows_per_tok, LANE), lambda i: (i, 0)),
        out_shape=jax.ShapeDtypeStruct((n * rows_per_tok, LANE), F32),
        compiler_params=_cparams(("arbitrary",), VMEM_LIMIT),
    )(idx, w, x3, tab)
    return out.reshape(n, d)


def _ple_kernel(x_ref, p_ref, g_ref, wg_ref, wp_ref, o_ref):
    x = x_ref[...]
    ms = jnp.mean(x * x, axis=-1, keepdims=True)
    hn = (x * lax.rsqrt(ms + RMS_EPS) * g_ref[...]).astype(BF16)
    gate = jax.nn.sigmoid(jnp.dot(hn, wg_ref[...], preferred_element_type=F32))
    pe = jnp.dot(p_ref[...].astype(BF16), wp_ref[...], preferred_element_type=F32)
    o_ref[...] = x + gate * pe


def _ple_call(x2, p, g, wg, wp, *, tm):
    n, d = x2.shape
    return pl.pallas_call(
        _ple_kernel,
        grid=(n // tm,),
        in_specs=[
            pl.BlockSpec((tm, d), lambda i: (i, 0)),
            pl.BlockSpec((tm, p.shape[1]), lambda i: (i, 0)),
            pl.BlockSpec((1, d), lambda i: (0, 0)),
            pl.BlockSpec(wg.shape, lambda i: (0, 0)),
            pl.BlockSpec(wp.shape, lambda i: (0, 0)),
        ],
        out_specs=pl.BlockSpec((tm, d), lambda i: (i, 0)),
        out_shape=jax.ShapeDtypeStruct((n, d), F32),
        compiler_params=_cparams(("arbitrary",), VMEM_LIMIT),
    )(x2, p, g, wg, wp)


def _rope_tables(pos, dim):
    half = dim // 2
    inv = ROPE_BASE ** (-jnp.arange(half, dtype=F32) / half)
    ang = pos.astype(F32)[:, None] * inv[None, :]
    cos, sin = jnp.cos(ang), jnp.sin(ang)
    return jnp.concatenate([cos, cos], axis=1), jnp.concatenate([-sin, sin], axis=1)


def _ret_tables(heads, chunk, n_valid):
    lg = jnp.log1p(-jnp.exp2(-5.0 - jnp.arange(heads, dtype=F32)))
    idx = jnp.arange(chunk, dtype=F32)
    diff = idx[:, None] - idx[None, :]
    causal = diff >= 0
    dec = jnp.where(causal[None], jnp.exp(jnp.where(causal, diff, 0.0)[None] * lg[:, None, None]), 0.0)
    qd = jnp.exp((idx[None, :] + 1.0) * lg[:, None])
    kd = jnp.exp((n_valid - 1.0 - idx)[None, :] * lg[:, None])
    kd = jnp.where(idx[None, :] < n_valid, kd, 0.0)
    sd = jnp.exp(n_valid * lg)
    return dec, qd[:, :, None], kd[:, :, None], sd[:, None, None]


def _token_tail(x2d, oa, ob, proj, p2d, lw, *, tm, tb, gate_col0):
    x1, h2, qp = _merge_call(x2d, oa, ob, proj, lw["wa"], lw["wb"], lw["wo"], lw["gf"], lw["wq"],
                             tm=tm, gate_col0=gate_col0)
    e, g = _peer_topk_call(qp, lw["keys"], tm=tm)
    w = _peer_u_call(e, h2, g, lw["u_tab"], tb=tb)
    x2 = _peer_v_call(e, w, x1, lw["v_tab"], tb=tb)
    return _ple_call(x2, p2d, lw["gp"], lw["wg"], lw["wp"], tm=tm)


def kernel(x_prompt, x_sample, cache_k, cache_v, state_ret, page_table, p_prompt, p_sample, norm_mix_g, w_in,
           q_norm_g, k_norm_g, sb_bias, ret_norm_g, w_a_up, w_b_up, w_out, norm_ffn_g, peer_w_q, peer_sub_keys,
           peer_u, peer_v, norm_ple_g, w_ple_gate, w_ple):
    batch, seq, d = x_prompt.shape
    dec_batch, dec_seq, _ = x_sample.shape
    depth, n_phys, page, sb_heads, sb_dim = cache_k.shape
    _, _, ret_heads, dk, dv = state_ret.shape
    sb_width = sb_heads * sb_dim
    past_len = page_table.shape[1] * page
    chunk = LANE
    rows_pad = SUBLANE
    assert dec_seq <= rows_pad and dk == LANE and sb_width % LANE == 0
    qb_col0 = 3 * sb_width
    gate_col0 = qb_col0 + 2 * ret_heads * dk + 2 * ret_heads * dv

    xp = x_prompt.reshape(batch * seq, d)
    xs = jnp.pad(x_sample, ((0, 0), (0, rows_pad - dec_seq), (0, 0))).reshape(dec_batch * rows_pad, d)
    ns = dec_batch * rows_pad

    cs_p, sn_p = _rope_tables(jnp.arange(seq), dk)
    cs_s, sn_s = _rope_tables(past_len + jnp.arange(rows_pad), dk)
    cs_s, sn_s = jnp.tile(cs_s, (dec_batch, 1)), jnp.tile(sn_s, (dec_batch, 1))
    gm = jnp.kron(jnp.eye(sb_heads, dtype=F32), jnp.full((sb_dim, sb_dim), 1.0 / sb_dim, F32))
    tri = jnp.concatenate([jnp.tril(jnp.ones((LANE, LANE), F32), -1), jnp.ones((LANE, LANE), F32)], axis=1).astype(BF16)
    head_mask = jnp.kron(jnp.eye(sb_heads, dtype=F32), jnp.ones((1, sb_dim), F32))
    dec_p = _ret_tables(ret_heads, chunk, float(chunk))
    dec_s = _ret_tables(ret_heads, chunk, float(dec_seq))

    tm = 256
    tb = 256
    kp, vp, sp, kn, vn, sn_out = [], [], [], [], [], []
    for i in range(depth):
        lw = dict(
            wa=w_a_up[i].astype(BF16), wb=w_b_up[i].astype(BF16), wo=w_out[i].astype(BF16),
            gf=norm_ffn_g[i][None, :], wq=peer_w_q[i].astype(BF16), keys=peer_sub_keys[i],
            u_tab=_pack_table(peer_u[i]), v_tab=_pack_table(peer_v[i]),
            gp=norm_ple_g[i][None, :], wg=w_ple_gate[i].astype(BF16), wp=w_ple[i].astype(BF16),
        )
        w_bf = w_in[i].astype(BF16)
        g_mix = norm_mix_g[i][None, :]
        qg = jnp.tile(q_norm_g[i], sb_heads)[None, :]
        kg = jnp.tile(k_norm_g[i], sb_heads)[None, :]
        rg = ret_norm_g[i][:, None, :]
        kb_scale = dk ** -0.5

        proj = _proj_call(xp, g_mix, w_bf, cs_p, sn_p, qg, kg, gm, tm=tm, pos_blocks=seq // tm, kb_scale=kb_scale)
        oa = _sb_prompt_call(proj, sb_bias[i], tri, batch=batch, seq=seq, heads=sb_heads, head_dim=sb_dim)
        ob, st_p = _ret_call(proj, None, *dec_p, rg, batch=batch, n_chunks=seq // chunk, rows=chunk,
                             heads=ret_heads, dk=dk, dv=dv, col0=qb_col0)
        xp = _token_tail(xp, oa, ob, proj, p_prompt[i].reshape(batch * seq, -1), lw, tm=tm, tb=tb,
                         gate_col0=gate_col0)
        kp.append(proj[:, sb_width:2 * sb_width].reshape(batch, seq, sb_heads, sb_dim))
        vp.append(proj[:, 2 * sb_width:3 * sb_width].reshape(batch, seq, sb_heads, sb_dim))
        sp.append(st_p)

        proj_s = _proj_call(xs, g_mix, w_bf, cs_s, sn_s, qg, kg, gm, tm=ns, pos_blocks=1, kb_scale=kb_scale)
        bias_col = jnp.tile(sb_bias[i], dec_seq)[:, None]
        oa_s = _sb_sample_call(page_table, proj_s, cache_k[i].reshape(n_phys, page, sb_width),
                               cache_v[i].reshape(n_phys, page, sb_width), tri, head_mask, bias_col,
                               dec_batch=dec_batch, rows_pad=rows_pad, n_new=dec_seq, heads=sb_heads,
                               head_dim=sb_dim)
        ob_s, st_s = _ret_call(proj_s, state_ret[i], *dec_s, rg, batch=dec_batch, n_chunks=1, rows=rows_pad,
                               heads=ret_heads, dk=dk, dv=dv, col0=qb_col0)
        p_s = jnp.pad(p_sample[i], ((0, 0), (0, rows_pad - dec_seq), (0, 0))).reshape(ns, -1)
        xs = _token_tail(xs, oa_s, ob_s, proj_s, p_s, lw, tm=ns, tb=ns, gate_col0=gate_col0)
        ka_s = proj_s[:, sb_width:2 * sb_width].reshape(dec_batch, rows_pad, sb_heads, sb_dim)[:, :dec_seq]
        va_s = proj_s[:, 2 * sb_width:3 * sb_width].reshape(dec_batch, rows_pad, sb_heads, sb_dim)[:, :dec_seq]
        kn.append(ka_s)
        vn.append(va_s)
        sn_out.append(st_s)

    y_prompt = xp.reshape(batch, seq, d)
    y_sample = xs.reshape(dec_batch, rows_pad, d)[:, :dec_seq]
    return (y_prompt, y_sample, jnp.stack(kp), jnp.stack(vp), jnp.stack(sp), jnp.stack(kn), jnp.stack(vn),
            jnp.stack(sn_out))
```

```python
import functools

import numpy as np
import jax
import jax.numpy as jnp
from jax import lax
from jax.experimental import pallas as pl
from jax.experimental.pallas import tpu as pltpu

F32 = jnp.float32
BF16 = jnp.bfloat16
I32 = jnp.int32
U32 = jnp.uint32
HIGHEST = lax.Precision.HIGHEST

RMS_EPS = 1e-6
ROPE_BASE = 10000.0
PEER_TOPK = 16
LANE = 128
SUBLANE = 8
VMEM_LIMIT = 56 * 1024 * 1024
NT_DIMS = (((1,), (1,)), ((), ()))
TN_DIMS = (((0,), (0,)), ((), ()))


def _cparams(sem, vmem=None):
    return pltpu.CompilerParams(dimension_semantics=sem, vmem_limit_bytes=vmem)


def _proj_kernel(x_ref, g_ref, w_ref, cs_ref, sn_ref, qg_ref, kg_ref, gm_ref, o_ref, hn_ref, *, kb_scale):
    j = pl.program_id(1)

    @pl.when(j == 0)
    def _():
        x = x_ref[...]
        ms = jnp.mean(x * x, axis=-1, keepdims=True)
        hn_ref[...] = (x * lax.rsqrt(ms + RMS_EPS) * g_ref[...]).astype(BF16)

    acc = jnp.dot(hn_ref[...], w_ref[...], preferred_element_type=F32)
    is_qk = j <= 1
    is_rot = (j == 3) | (j == 4)

    @pl.when(is_qk)
    def _():
        gain = jnp.where(j == 0, qg_ref[...], kg_ref[...])
        ms = jnp.dot(acc * acc, gm_ref[...], precision=HIGHEST, preferred_element_type=F32)
        o_ref[...] = acc * lax.rsqrt(ms + RMS_EPS) * gain

    @pl.when(is_rot)
    def _():
        scale = jnp.where(j == 4, kb_scale, 1.0).astype(F32)
        cs = cs_ref[...]
        sn = sn_ref[...]
        for h in range(o_ref.shape[1] // LANE):
            xh = acc[:, h * LANE:(h + 1) * LANE]
            r = pltpu.roll(xh, LANE // 2, 1)
            o_ref[:, h * LANE:(h + 1) * LANE] = (xh * cs + r * sn) * scale

    @pl.when(jnp.logical_not(is_qk | is_rot))
    def _():
        o_ref[...] = acc


def _proj_call(x2d, g, w_bf, cs, sn, qg, kg, gm, *, tm, pos_blocks, kb_scale):
    n, d = x2d.shape
    pw = w_bf.shape[1]
    tn = 512
    return pl.pallas_call(
        functools.partial(_proj_kernel, kb_scale=kb_scale),
        grid=(n // tm, pw // tn),
        in_specs=[
            pl.BlockSpec((tm, d), lambda i, j: (i, 0)),
            pl.BlockSpec((1, d), lambda i, j: (0, 0)),
            pl.BlockSpec((d, tn), lambda i, j: (0, j)),
            pl.BlockSpec((tm, LANE), lambda i, j: (i % pos_blocks, 0)),
            pl.BlockSpec((tm, LANE), lambda i, j: (i % pos_blocks, 0)),
            pl.BlockSpec((1, tn), lambda i, j: (0, 0)),
            pl.BlockSpec((1, tn), lambda i, j: (0, 0)),
            pl.BlockSpec((tn, tn), lambda i, j: (0, 0)),
        ],
        out_specs=pl.BlockSpec((tm, tn), lambda i, j: (i, j)),
        out_shape=jax.ShapeDtypeStruct((n, pw), F32),
        scratch_shapes=[pltpu.VMEM((tm, d), BF16)],
        compiler_params=_cparams(("arbitrary", "arbitrary"), VMEM_LIMIT),
    )(x2d, g, w_bf, cs, sn, qg, kg, gm)


def _sb_block(z, mask, tri, car):
    e = jnp.exp(-jnp.abs(z))
    sp = jnp.maximum(z, 0.0) + jnp.log(1.0 + e)
    spm = sp if mask is None else jnp.where(mask, sp, 0.0)
    hi = spm.astype(BF16)
    lo = (spm - hi.astype(F32)).astype(BF16)
    cs = jnp.dot(hi, tri, preferred_element_type=F32) + jnp.dot(lo, tri, preferred_element_type=F32)
    a = jnp.exp(z - sp - cs[:, :LANE] - car)
    if mask is not None:
        a = jnp.where(mask, a, 0.0)
    return a, car + cs[:, LANE:]


def _sb_prompt_kernel(bias_ref, q_ref, k_ref, v_ref, tri_ref, o_ref, qm_ref, acc_ref, car_ref, *, scale):
    i = pl.program_id(1)
    tq = q_ref.shape[0]
    heads = qm_ref.shape[0]
    lane = lax.broadcasted_iota(I32, (1, LANE), 1)
    hmask = [lane < LANE // 2, lane >= LANE // 2]
    for h in range(heads):
        qt = q_ref[:, (h // 2) * LANE:(h // 2 + 1) * LANE] * scale
        qm_ref[h] = jnp.where(hmask[h % 2], qt, 0.0).astype(BF16)
    tri = tri_ref[...]
    acc_ref[...] = jnp.zeros_like(acc_ref)
    car_ref[...] = jnp.zeros_like(car_ref)
    row = lax.broadcasted_iota(I32, (tq, LANE), 0)
    col = lax.broadcasted_iota(I32, (tq, LANE), 1)
    causal = col < row

    def block(kb, mask):
        start = pl.multiple_of(kb * LANE, LANE)
        for t in range(heads // 2):
            cols = slice(t * LANE, (t + 1) * LANE)
            kblk = k_ref[pl.ds(start, LANE), cols].astype(BF16)
            vblk = v_ref[pl.ds(start, LANE), cols]
            contrib = None
            for hh in range(2):
                h = 2 * t + hh
                z = lax.dot_general(qm_ref[h], kblk, NT_DIMS, preferred_element_type=F32) + bias_ref[h]
                a, car = _sb_block(z, mask, tri, car_ref[h])
                car_ref[h] = car
                vm = jnp.where(hmask[hh], vblk, 0.0).astype(BF16)
                d = jnp.dot(a.astype(BF16), vm, preferred_element_type=F32)
                contrib = d if contrib is None else contrib + d
            acc_ref[:, cols] += contrib

    block(i, causal)

    def body(jj, c):
        block(i - 1 - jj, None)
        return c

    lax.fori_loop(0, i, body, 0)
    o_ref[...] = acc_ref[...]


def _sb_prompt_call(proj, bias, tri, *, batch, seq, heads, head_dim):
    tq = LANE
    nq = seq // tq
    width = heads * head_dim
    return pl.pallas_call(
        functools.partial(_sb_prompt_kernel, scale=head_dim ** -0.5),
        grid=(batch, nq),
        in_specs=[
            pl.BlockSpec(memory_space=pltpu.SMEM),
            pl.BlockSpec((tq, width), lambda b, i: (b * nq + i, 0)),
            pl.BlockSpec((seq, width), lambda b, i: (b, 1)),
            pl.BlockSpec((seq, width), lambda b, i: (b, 2)),
            pl.BlockSpec((LANE, 2 * LANE), lambda b, i: (0, 0)),
        ],
        out_specs=pl.BlockSpec((tq, width), lambda b, i: (b * nq + i, 0)),
        out_shape=jax.ShapeDtypeStruct((batch * seq, width), F32),
        scratch_shapes=[
            pltpu.VMEM((heads, tq, LANE), BF16),
            pltpu.VMEM((tq, width), F32),
            pltpu.VMEM((heads, tq, LANE), F32),
        ],
        compiler_params=_cparams(("arbitrary", "arbitrary"), VMEM_LIMIT),
    )(bias, proj, proj, proj, tri)


def _sb_sample_kernel(pt_ref, q_ref, kn_ref, vn_ref, *rest, scale, n_new, pages_per_step):
    kc_refs = rest[:pages_per_step]
    vc_refs = rest[pages_per_step:2 * pages_per_step]
    tri_ref, hm_ref, bias_ref, o_ref, qbd_ref, pad_ref, acc_ref, car_ref = rest[2 * pages_per_step:]
    p = pl.program_id(1)
    rows = qbd_ref.shape[0]
    heads = hm_ref.shape[0]
    tri = tri_ref[...]
    bias = bias_ref[...]

    def blocks(kv, mask):
        car = car_ref[...]
        total = None
        for kblk, vblk in kv:
            z = lax.dot_general(qbd_ref[...], kblk.astype(BF16), NT_DIMS, preferred_element_type=F32) + bias
            a, car = _sb_block(z, mask, tri, car)
            d = jnp.dot(a.astype(BF16), vblk.astype(BF16), preferred_element_type=F32)
            total = d if total is None else total + d
        car_ref[...] = car
        acc_ref[...] += total

    @pl.when(p == 0)
    def _():
        hm = hm_ref[...]
        for i in range(n_new):
            qi = q_ref[i:i + 1, :] * scale
            qbd_ref[i * heads:(i + 1) * heads, :] = jnp.where(hm > 0, qi, 0.0).astype(BF16)
        acc_ref[...] = jnp.zeros_like(acc_ref)
        car_ref[...] = jnp.zeros_like(car_ref)
        qrow = lax.broadcasted_iota(I32, (rows, LANE), 0) // heads
        col = lax.broadcasted_iota(I32, (rows, LANE), 1)
        mask = col < qrow
        pad_ref[...] = jnp.zeros_like(pad_ref)
        pad_ref[0, 0:kn_ref.shape[0], :] = kn_ref[...]
        pad_ref[1, 0:vn_ref.shape[0], :] = vn_ref[...]
        blocks([(pad_ref[0], pad_ref[1])], mask)

    blocks([(kc[...], vc[...]) for kc, vc in zip(kc_refs, vc_refs)], None)

    @pl.when(p == pl.num_programs(1) - 1)
    def _():
        hm = hm_ref[...]
        o_ref[...] = jnp.zeros_like(o_ref)
        for i in range(n_new):
            blk = acc_ref[i * heads:(i + 1) * heads, :] * hm
            o_ref[i:i + 1, :] = jnp.sum(blk, axis=0, keepdims=True)


def _sb_sample_call(page_table, proj_s, cache_k3, cache_v3, tri, hm, bias_col, *, dec_batch, rows_pad, n_new,
                    heads, head_dim):
    n_pages = page_table.shape[1]
    width = heads * head_dim
    page = cache_k3.shape[1]
    rows = n_new * heads
    pps = max(c for c in (8, 4, 2, 1) if n_pages % c == 0)

    def page_spec(j):
        return pl.BlockSpec((None, page, width),
                            lambda b, p, pt: (pt[b * n_pages + n_pages - 1 - (p * pps + j)], 0, 0))

    grid_spec = pltpu.PrefetchScalarGridSpec(
        num_scalar_prefetch=1,
        grid=(dec_batch, n_pages // pps),
        in_specs=[
            pl.BlockSpec((rows_pad, width), lambda b, p, pt: (b, 0)),
            pl.BlockSpec((rows_pad, width), lambda b, p, pt: (b, 1)),
            pl.BlockSpec((rows_pad, width), lambda b, p, pt: (b, 2)),
            *[page_spec(j) for j in range(pps)],
            *[page_spec(j) for j in range(pps)],
            pl.BlockSpec((LANE, 2 * LANE), lambda b, p, pt: (0, 0)),
            pl.BlockSpec((heads, width), lambda b, p, pt: (0, 0)),
            pl.BlockSpec((rows, 1), lambda b, p, pt: (0, 0)),
        ],
        out_specs=pl.BlockSpec((rows_pad, width), lambda b, p, pt: (b, 0)),
        scratch_shapes=[
            pltpu.VMEM((rows, width), BF16),
            pltpu.VMEM((2, page, width), F32),
            pltpu.VMEM((rows, width), F32),
            pltpu.VMEM((rows, LANE), F32),
        ],
    )
    return pl.pallas_call(
        functools.partial(_sb_sample_kernel, scale=head_dim ** -0.5, n_new=n_new, pages_per_step=pps),
        grid_spec=grid_spec,
        out_shape=jax.ShapeDtypeStruct((dec_batch * rows_pad, width), F32),
        compiler_params=_cparams(("arbitrary", "arbitrary"), VMEM_LIMIT),
    )(page_table.reshape(-1), proj_s, proj_s, proj_s, *([cache_k3] * pps), *([cache_v3] * pps), tri, hm, bias_col)


def _ret_kernel(q_ref, k_ref, v_ref, gb_ref, st0_ref, dec_ref, qd_ref, kd_ref, sd_ref, rg_ref,
                ob_ref, st_ref, state_ref, qp_ref, kp_ref, vp_ref, *, use_init):
    c = pl.program_id(2)
    rows = q_ref.shape[0]
    chunk = dec_ref.shape[0]

    @pl.when(c == 0)
    def _():
        if use_init:
            state_ref[...] = st0_ref[...]
        else:
            state_ref[...] = jnp.zeros_like(state_ref)

    if rows == chunk:
        q, k, v = q_ref[...], k_ref[...], v_ref[...]
    else:
        qp_ref[...] = jnp.zeros_like(qp_ref)
        kp_ref[...] = jnp.zeros_like(kp_ref)
        vp_ref[...] = jnp.zeros_like(vp_ref)
        qp_ref[0:rows, :] = q_ref[...]
        kp_ref[0:rows, :] = k_ref[...]
        vp_ref[0:rows, :] = v_ref[...]
        q, k, v = qp_ref[...], kp_ref[...], vp_ref[...]

    qb = q.astype(BF16)
    vb = v.astype(BF16)
    scores = lax.dot_general(qb, k.astype(BF16), NT_DIMS, preferred_element_type=F32) * dec_ref[...]
    inner = jnp.dot(scores.astype(BF16), vb, preferred_element_type=F32)
    st = state_ref[...]
    cross = jnp.dot(qb, st.astype(BF16), preferred_element_type=F32) * qd_ref[...]
    o = (inner + cross)[0:rows, :]
    kd = (k * kd_ref[...]).astype(BF16)
    new = sd_ref[...] * st + lax.dot_general(kd, vb, TN_DIMS, preferred_element_type=F32)
    state_ref[...] = new

    ms = jnp.mean(o * o, axis=-1, keepdims=True)
    gb = gb_ref[...]
    ob_ref[...] = (o * lax.rsqrt(ms + RMS_EPS) * rg_ref[...]) * (gb * jax.nn.sigmoid(gb))

    @pl.when(c == pl.num_programs(2) - 1)
    def _():
        st_ref[...] = new


def _ret_call(proj, state0, dec, qd, kd, sd, rg, *, batch, n_chunks, rows, heads, dk, dv, col0):
    chunk = dec.shape[1]
    qc = col0 // dk
    kc = qc + heads
    vc = (col0 + 2 * heads * dk) // dv
    gc = vc + heads
    use_init = state0 is not None
    if state0 is None:
        state0 = jnp.zeros((1, 1, dk, dv), F32)
        st_map = lambda b, h, c: (0, 0, 0, 0)
    else:
        st_map = lambda b, h, c: (b, h, 0, 0)
    pad_rows = chunk if rows != chunk else SUBLANE
    return pl.pallas_call(
        functools.partial(_ret_kernel, use_init=use_init),
        grid=(batch, heads, n_chunks),
        in_specs=[
            pl.BlockSpec((rows, dk), lambda b, h, c: (b * n_chunks + c, qc + h)),
            pl.BlockSpec((rows, dk), lambda b, h, c: (b * n_chunks + c, kc + h)),
            pl.BlockSpec((rows, dv), lambda b, h, c: (b * n_chunks + c, vc + h)),
            pl.BlockSpec((rows, dv), lambda b, h, c: (b * n_chunks + c, gc + h)),
            pl.BlockSpec((None, None, dk, dv), st_map),
            pl.BlockSpec((None, chunk, chunk), lambda b, h, c: (h, 0, 0)),
            pl.BlockSpec((None, chunk, 1), lambda b, h, c: (h, 0, 0)),
            pl.BlockSpec((None, chunk, 1), lambda b, h, c: (h, 0, 0)),
            pl.BlockSpec((None, 1, 1), lambda b, h, c: (h, 0, 0)),
            pl.BlockSpec((None, 1, dv), lambda b, h, c: (h, 0, 0)),
        ],
        out_specs=[
            pl.BlockSpec((rows, dv), lambda b, h, c: (b * n_chunks + c, h)),
            pl.BlockSpec((None, None, dk, dv), lambda b, h, c: (b, h, 0, 0)),
        ],
        out_shape=[
            jax.ShapeDtypeStruct((batch * n_chunks * rows, heads * dv), F32),
            jax.ShapeDtypeStruct((batch, heads, dk, dv), F32),
        ],
        scratch_shapes=[
            pltpu.VMEM((dk, dv), F32),
            pltpu.VMEM((pad_rows, dk), F32),
            pltpu.VMEM((pad_rows, dk), F32),
            pltpu.VMEM((pad_rows, dv), F32),
        ],
        compiler_params=_cparams(("arbitrary", "arbitrary", "arbitrary")),
    )(proj, proj, proj, proj, state0, dec, qd, kd, sd, rg)


def _merge_kernel(x_ref, oa_ref, ob_ref, ga0_ref, ga1_ref, gb0_ref, gb1_ref, wa_ref, wb_ref, wo_ref, gf_ref,
                  wq_ref, x1_ref, h2_ref, qp_ref):
    ya = jnp.dot(oa_ref[...].astype(BF16), wa_ref[...], preferred_element_type=F32)
    yb = jnp.dot(ob_ref[...].astype(BF16), wb_ref[...], preferred_element_type=F32)
    ga = jnp.concatenate([ga0_ref[...], ga1_ref[...]], axis=1)
    gb = jnp.concatenate([gb0_ref[...], gb1_ref[...]], axis=1)
    merged = jax.nn.sigmoid(ga) * ya + jax.nn.sigmoid(gb) * yb
    x1 = x_ref[...] + jnp.dot(merged.astype(BF16), wo_ref[...], preferred_element_type=F32)
    x1_ref[...] = x1
    ms = jnp.mean(x1 * x1, axis=-1, keepdims=True)
    h2 = x1 * lax.rsqrt(ms + RMS_EPS) * gf_ref[...]
    h2_ref[...] = h2
    qp_ref[...] = jnp.dot(h2.astype(BF16), wq_ref[...], preferred_element_type=F32)


def _merge_call(x2d, oa, ob, proj, wa, wb, wo, gf, wq, *, tm, gate_col0):
    n, d = x2d.shape
    half = d // 2
    g0 = gate_col0 // half
    row = lambda i: (i, 0)
    const = lambda i: (0, 0)
    return pl.pallas_call(
        _merge_kernel,
        grid=(n // tm,),
        in_specs=[
            pl.BlockSpec((tm, d), row),
            pl.BlockSpec((tm, oa.shape[1]), row),
            pl.BlockSpec((tm, ob.shape[1]), row),
            pl.BlockSpec((tm, half), lambda i: (i, g0)),
            pl.BlockSpec((tm, half), lambda i: (i, g0 + 1)),
            pl.BlockSpec((tm, half), lambda i: (i, g0 + 2)),
            pl.BlockSpec((tm, half), lambda i: (i, g0 + 3)),
            pl.BlockSpec(wa.shape, const),
            pl.BlockSpec(wb.shape, const),
            pl.BlockSpec(wo.shape, const),
            pl.BlockSpec((1, d), const),
            pl.BlockSpec(wq.shape, const),
        ],
        out_specs=[
            pl.BlockSpec((tm, d), row),
            pl.BlockSpec((tm, d), row),
            pl.BlockSpec((tm, wq.shape[1]), row),
        ],
        out_shape=[
            jax.ShapeDtypeStruct((n, d), F32),
            jax.ShapeDtypeStruct((n, d), F32),
            jax.ShapeDtypeStruct((n, wq.shape[1]), F32),
        ],
        compiler_params=_cparams(("arbitrary",), VMEM_LIMIT),
    )(x2d, oa, ob, proj, proj, proj, proj, wa, wb, wo, gf, wq)


def _topk_rows(s, k):
    n, t = s.shape
    row = lax.broadcasted_iota(I32, (n, t), 0)
    out_row = lax.broadcasted_iota(I32, (k, t), 0)
    vals = jnp.zeros((k, t), F32)
    idxs = jnp.zeros((k, t), I32)
    for r in range(k):
        m = jnp.max(s, axis=0, keepdims=True)
        am = jnp.min(jnp.where(s == m, row, n), axis=0, keepdims=True)
        vals = jnp.where(out_row == r, m, vals)
        idxs = jnp.where(out_row == r, am, idxs)
        s = jnp.where(row == am, -jnp.inf, s)
    return vals, idxs


def _select_rows(table, sel):
    out = jnp.zeros(sel.shape, table.dtype)
    for a in range(table.shape[0]):
        out = jnp.where(sel == a, table[a:a + 1, :], out)
    return out


def _peer_topk_kernel(qp_ref, keys_ref, e_ref, g_ref, *, n_keys):
    heads = keys_ref.shape[0]
    half = keys_ref.shape[3]
    k = PEER_TOPK
    e_rows, g_rows = [], []
    for h in range(heads):
        tops = []
        for c in range(2):
            col = (h * 2 + c) * half
            q = qp_ref[:, col:col + half].astype(BF16)
            st = lax.dot_general(keys_ref[h, c].astype(BF16), q, NT_DIMS, preferred_element_type=F32)
            tops.append(_topk_rows(st, k))
        (s1, i1), (s2, i2) = tops
        cand = jnp.concatenate([s1[a:a + 1, :] + s2 for a in range(k)], axis=0)
        top, flat = _topk_rows(cand, k)
        ia = _select_rows(i1, flat // k)
        ib = _select_rows(i2, flat % k)
        e_rows.append((ia * n_keys + ib) * EXPERT_ROWS)
        ex = jnp.exp(top - jnp.max(top, axis=0, keepdims=True))
        g_rows.append(ex / jnp.sum(ex, axis=0, keepdims=True))
    e_ref[...] = jnp.transpose(jnp.concatenate(e_rows, axis=0))
    g_ref[...] = jnp.transpose(jnp.concatenate(g_rows, axis=0))


def _peer_topk_call(qp, keys, *, tm):
    n = qp.shape[0]
    heads, _, n_keys, half = keys.shape
    slots = heads * PEER_TOPK
    return pl.pallas_call(
        functools.partial(_peer_topk_kernel, n_keys=n_keys),
        grid=(n // tm,),
        in_specs=[
            pl.BlockSpec((tm, qp.shape[1]), lambda i: (i, 0)),
            pl.BlockSpec(keys.shape, lambda i: (0, 0, 0, 0)),
        ],
        out_specs=[
            pl.BlockSpec((tm, slots), lambda i: (i, 0)),
            pl.BlockSpec((tm, slots), lambda i: (i, 0)),
        ],
        out_shape=[
            jax.ShapeDtypeStruct((n, slots), I32),
            jax.ShapeDtypeStruct((n, slots), F32),
        ],
        compiler_params=_cparams(("arbitrary",), VMEM_LIMIT),
    )(qp, keys)


EXPERT_ROWS = 4


def _pack_table(tab):
    n, d = tab.shape
    assert d == 2 * EXPERT_ROWS * LANE
    bits = lax.bitcast_convert_type(tab.astype(BF16), jnp.uint16).astype(U32)
    packed = bits[:, :d // 2] | (bits[:, d // 2:] << 16)
    return packed.reshape(n * EXPERT_ROWS, LANE)


def _expert_halves(tab_ref, off):
    words = tab_ref[pl.ds(pl.multiple_of(off, EXPERT_ROWS), EXPERT_ROWS), :]
    lo = lax.bitcast_convert_type(words << 16, F32)
    hi = lax.bitcast_convert_type(words & jnp.uint32(0xFFFF0000), F32)
    return lo, hi


def _fold_sublanes(a, b, dist, sub):
    ta = a + pltpu.roll(a, SUBLANE - dist, 0)
    tb = b + pltpu.roll(b, dist, 0)
    return jnp.where((sub & dist) == 0, ta, tb)


def _product_slot(r):
    j = r % SUBLANE
    return (r - j) + (2 * j if j < SUBLANE // 2 else 2 * (j - SUBLANE // 2) + 1)


def _peer_u_kernel(idx_ref, h_ref, g_ref, tab_ref, w_ref, q_ref, p_ref):
    tb, slots = idx_ref.shape
    ones = jnp.ones((SUBLANE, LANE), BF16)
    sub = lax.broadcasted_iota(I32, (SUBLANE, LANE), 0)

    def group(t8, carry):
        def token(s, c):
            t = t8 * SUBLANE + s
            base = pl.multiple_of(t * SUBLANE, SUBLANE)
            h_lo = h_ref[pl.ds(base, EXPERT_ROWS), :]
            h_hi = h_ref[pl.ds(base + EXPERT_ROWS, EXPERT_ROWS), :]
            for r in range(slots):
                lo, hi = _expert_halves(tab_ref, idx_ref[t, r])
                slot = _product_slot(r)
                p_ref[slot * EXPERT_ROWS:(slot + 1) * EXPERT_ROWS, :] = lo * h_lo + hi * h_hi
            for m in range(slots // SUBLANE):
                v = [p_ref[(4 * m + k) * SUBLANE:(4 * m + k + 1) * SUBLANE, :] for k in range(4)]
                c0 = _fold_sublanes(v[0], v[2], 2, sub)
                c1 = _fold_sublanes(v[1], v[3], 2, sub)
                q_ref[s, m * SUBLANE:(m + 1) * SUBLANE, :] = _fold_sublanes(c0, c1, 1, sub)
            return c

        lax.fori_loop(0, SUBLANE, token, 0)
        act = jnp.zeros((SUBLANE, slots), F32)
        for s in range(SUBLANE):
            qs = q_ref[s]
            hi = qs.astype(BF16)
            lo = (qs - hi.astype(F32)).astype(BF16)
            r = (lax.dot_general(ones, hi, NT_DIMS, preferred_element_type=F32)
                 + lax.dot_general(ones, lo, NT_DIMS, preferred_element_type=F32))
            act = jnp.where(sub == s, r, act)
        rows = pl.ds(pl.multiple_of(t8 * SUBLANE, SUBLANE), SUBLANE)
        w_ref[rows, :] = g_ref[rows, :] * jax.nn.gelu(act)
        return carry

    lax.fori_loop(0, tb // SUBLANE, group, 0)


def _peer_u_call(idx, h2, g, tab, *, tb):
    n, slots = idx.shape
    d = h2.shape[1]
    rows_per_tok = d // LANE
    h3 = h2.reshape(n * rows_per_tok, LANE)
    return pl.pallas_call(
        _peer_u_kernel,
        grid=(n // tb,),
        in_specs=[
            pl.BlockSpec((tb, slots), lambda i: (i, 0), memory_space=pltpu.SMEM),
            pl.BlockSpec((tb * rows_per_tok, LANE), lambda i: (i, 0)),
            pl.BlockSpec((tb, slots), lambda i: (i, 0)),
            pl.BlockSpec(tab.shape, lambda i: (0, 0), pipeline_mode=pl.Buffered(1)),
        ],
        out_specs=pl.BlockSpec((tb, slots), lambda i: (i, 0)),
        out_shape=jax.ShapeDtypeStruct((n, slots), F32),
        scratch_shapes=[pltpu.VMEM((SUBLANE, slots, LANE), F32), pltpu.VMEM((slots * EXPERT_ROWS, LANE), F32)],
        compiler_params=_cparams(("arbitrary",), VMEM_LIMIT),
    )(idx, h3, g, tab)


def _peer_v_kernel(idx_ref, w_ref, x_ref, tab_ref, o_ref):
    tb, slots = idx_ref.shape
    n_acc = 2

    def token(t, c):
        acc_lo = [jnp.zeros((EXPERT_ROWS, LANE), F32) for _ in range(n_acc)]
        acc_hi = [jnp.zeros((EXPERT_ROWS, LANE), F32) for _ in range(n_acc)]
        for r in range(slots):
            lo, hi = _expert_halves(tab_ref, idx_ref[t, r])
            w = w_ref[t, r]
            acc_lo[r % n_acc] = acc_lo[r % n_acc] + lo * w
            acc_hi[r % n_acc] = acc_hi[r % n_acc] + hi * w
        base = pl.multiple_of(t * SUBLANE, SUBLANE)
        rows_lo = pl.ds(base, EXPERT_ROWS)
        rows_hi = pl.ds(base + EXPERT_ROWS, EXPERT_ROWS)
        o_ref[rows_lo, :] = x_ref[rows_lo, :] + (acc_lo[0] + acc_lo[1])
        o_ref[rows_hi, :] = x_ref[rows_hi, :] + (acc_hi[0] + acc_hi[1])
        return c

    lax.fori_loop(0, tb, token, 0)


def _peer_v_call(idx, w, x1, tab, *, tb):
    n, slots = idx.shape
    d = x1.shape[1]
    rows_per_tok = d // LANE
    x3 = x1.reshape(n * rows_per_tok, LANE)
    out = pl.pallas_call(
        _peer_v_kernel,
        grid=(n // tb,),
        in_specs=[
            pl.BlockSpec((tb, slots), lambda i: (i, 0), memory_space=pltpu.SMEM),
            pl.BlockSpec((tb, slots), lambda i: (i, 0), memory_space=pltpu.SMEM),
            pl.BlockSpec((tb * rows_per_tok, LANE), lambda i: (i, 0)),
            pl.BlockSpec(tab.shape, lambda i: (0, 0), pipeline_mode=pl.Buffered(1)),
        ],
        out_specs=pl.BlockSpec((tb * rows_per_tok, LANE), lambda i: (i, 0)),
        out_shape=jax.ShapeDtypeStruct((n * rows_per_tok, LANE), F32),
        compiler_params=_cparams(("arbitrary",), VMEM_LIMIT),
    )(idx, w, x3, tab)
    return out.reshape(n, d)


def _ple_kernel(x_ref, p_ref, g_ref, wg_ref, wp_ref, o_ref):
    x = x_ref[...]
    ms = jnp.mean(x * x, axis=-1, keepdims=True)
    hn = (x * lax.rsqrt(ms + RMS_EPS) * g_ref[...]).astype(BF16)
    gate = jax.nn.sigmoid(jnp.dot(hn, wg_ref[...], preferred_element_type=F32))
    pe = jnp.dot(p_ref[...].astype(BF16), wp_ref[...], preferred_element_type=F32)
    o_ref[...] = x + gate * pe


def _ple_call(x2, p, g, wg, wp, *, tm):
    n, d = x2.shape
    return pl.pallas_call(
        _ple_kernel,
        grid=(n // tm,),
        in_specs=[
            pl.BlockSpec((tm, d), lambda i: (i, 0)),
            pl.BlockSpec((tm, p.shape[1]), lambda i: (i, 0)),
            pl.BlockSpec((1, d), lambda i: (0, 0)),
            pl.BlockSpec(wg.shape, lambda i: (0, 0)),
            pl.BlockSpec(wp.shape, lambda i: (0, 0)),
        ],
        out_specs=pl.BlockSpec((tm, d), lambda i: (i, 0)),
        out_shape=jax.ShapeDtypeStruct((n, d), F32),
        compiler_params=_cparams(("arbitrary",), VMEM_LIMIT),
    )(x2, p, g, wg, wp)


def _rope_tables(pos, dim):
    half = dim // 2
    inv = ROPE_BASE ** (-jnp.arange(half, dtype=F32) / half)
    ang = pos.astype(F32)[:, None] * inv[None, :]
    cos, sin = jnp.cos(ang), jnp.sin(ang)
    return jnp.concatenate([cos, cos], axis=1), jnp.concatenate([-sin, sin], axis=1)


def _ret_tables(heads, chunk, n_valid):
    lg = jnp.log1p(-jnp.exp2(-5.0 - jnp.arange(heads, dtype=F32)))
    idx = jnp.arange(chunk, dtype=F32)
    diff = idx[:, None] - idx[None, :]
    causal = diff >= 0
    dec = jnp.where(causal[None], jnp.exp(jnp.where(causal, diff, 0.0)[None] * lg[:, None, None]), 0.0)
    qd = jnp.exp((idx[None, :] + 1.0) * lg[:, None])
    kd = jnp.exp((n_valid - 1.0 - idx)[None, :] * lg[:, None])
    kd = jnp.where(idx[None, :] < n_valid, kd, 0.0)
    sd = jnp.exp(n_valid * lg)
    return dec, qd[:, :, None], kd[:, :, None], sd[:, None, None]


def _token_tail(x2d, oa, ob, proj, p2d, lw, *, tm, tb, gate_col0):
    x1, h2, qp = _merge_call(x2d, oa, ob, proj, lw["wa"], lw["wb"], lw["wo"], lw["gf"], lw["wq"],
                             tm=tm, gate_col0=gate_col0)
    e, g = _peer_topk_call(qp, lw["keys"], tm=tm)
    w = _peer_u_call(e, h2, g, lw["u_tab"], tb=tb)
    x2 = _peer_v_call(e, w, x1, lw["v_tab"], tb=tb)
    return _ple_call(x2, p2d, lw["gp"], lw["wg"], lw["wp"], tm=tm)


def kernel(x_prompt, x_sample, cache_k, cache_v, state_ret, page_table, p_prompt, p_sample, norm_mix_g, w_in,
           q_norm_g, k_norm_g, sb_bias, ret_norm_g, w_a_up, w_b_up, w_out, norm_ffn_g, peer_w_q, peer_sub_keys,
           peer_u, peer_v, norm_ple_g, w_ple_gate, w_ple):
    batch, seq, d = x_prompt.shape
    dec_batch, dec_seq, _ = x_sample.shape
    depth, n_phys, page, sb_heads, sb_dim = cache_k.shape
    _, _, ret_heads, dk, dv = state_ret.shape
    sb_width = sb_heads * sb_dim
    past_len = page_table.shape[1] * page
    chunk = LANE
    rows_pad = SUBLANE
    assert dec_seq <= rows_pad and dk == LANE and sb_width % LANE == 0
    qb_col0 = 3 * sb_width
    gate_col0 = qb_col0 + 2 * ret_heads * dk + 2 * ret_heads * dv

    xp = x_prompt.reshape(batch * seq, d)
    xs = jnp.pad(x_sample, ((0, 0), (0, rows_pad - dec_seq), (0, 0))).reshape(dec_batch * rows_pad, d)
    ns = dec_batch * rows_pad

    cs_p, sn_p = _rope_tables(jnp.arange(seq), dk)
    cs_s, sn_s = _rope_tables(past_len + jnp.arange(rows_pad), dk)
    cs_s, sn_s = jnp.tile(cs_s, (dec_batch, 1)), jnp.tile(sn_s, (dec_batch, 1))
    gm = jnp.kron(jnp.eye(sb_heads, dtype=F32), jnp.full((sb_dim, sb_dim), 1.0 / sb_dim, F32))
    tri = jnp.concatenate([jnp.tril(jnp.ones((LANE, LANE), F32), -1), jnp.ones((LANE, LANE), F32)], axis=1).astype(BF16)
    head_mask = jnp.kron(jnp.eye(sb_heads, dtype=F32), jnp.ones((1, sb_dim), F32))
    dec_p = _ret_tables(ret_heads, chunk, float(chunk))
    dec_s = _ret_tables(ret_heads, chunk, float(dec_seq))

    tm = 256
    tb = 256
    kp, vp, sp, kn, vn, sn_out = [], [], [], [], [], []
    for i in range(depth):
        lw = dict(
            wa=w_a_up[i].astype(BF16), wb=w_b_up[i].astype(BF16), wo=w_out[i].astype(BF16),
            gf=norm_ffn_g[i][None, :], wq=peer_w_q[i].astype(BF16), keys=peer_sub_keys[i],
            u_tab=_pack_table(peer_u[i]), v_tab=_pack_table(peer_v[i]),
            gp=norm_ple_g[i][None, :], wg=w_ple_gate[i].astype(BF16), wp=w_ple[i].astype(BF16),
        )
        w_bf = w_in[i].astype(BF16)
        g_mix = norm_mix_g[i][None, :]
        qg = jnp.tile(q_norm_g[i], sb_heads)[None, :]
        kg = jnp.tile(k_norm_g[i], sb_heads)[None, :]
        rg = ret_norm_g[i][:, None, :]
        kb_scale = dk ** -0.5

        tm_proj = 512 if seq % 512 == 0 else tm
        proj = _proj_call(xp, g_mix, w_bf, cs_p, sn_p, qg, kg, gm, tm=tm_proj, pos_blocks=seq // tm_proj,
                          kb_scale=kb_scale)
        oa = _sb_prompt_call(proj, sb_bias[i], tri, batch=batch, seq=seq, heads=sb_heads, head_dim=sb_dim)
        ob, st_p = _ret_call(proj, None, *dec_p, rg, batch=batch, n_chunks=seq // chunk, rows=chunk,
                             heads=ret_heads, dk=dk, dv=dv, col0=qb_col0)
        xp = _token_tail(xp, oa, ob, proj, p_prompt[i].reshape(batch * seq, -1), lw, tm=tm, tb=tb,
                         gate_col0=gate_col0)
        kp.append(proj[:, sb_width:2 * sb_width].reshape(batch, seq, sb_heads, sb_dim))
        vp.append(proj[:, 2 * sb_width:3 * sb_width].reshape(batch, seq, sb_heads, sb_dim))
        sp.append(st_p)

        proj_s = _proj_call(xs, g_mix, w_bf, cs_s, sn_s, qg, kg, gm, tm=ns, pos_blocks=1, kb_scale=kb_scale)
        bias_col = jnp.tile(sb_bias[i], dec_seq)[:, None]
        oa_s = _sb_sample_call(page_table, proj_s, cache_k[i].reshape(n_phys, page, sb_width),
                               cache_v[i].reshape(n_phys, page, sb_width), tri, head_mask, bias_col,
                               dec_batch=dec_batch, rows_pad=rows_pad, n_new=dec_seq, heads=sb_heads,
                               head_dim=sb_dim)
        ob_s, st_s = _ret_call(proj_s, state_ret[i], *dec_s, rg, batch=dec_batch, n_chunks=1, rows=rows_pad,
                               heads=ret_heads, dk=dk, dv=dv, col0=qb_col0)
        p_s = jnp.pad(p_sample[i], ((0, 0), (0, rows_pad - dec_seq), (0, 0))).reshape(ns, -1)
        xs = _token_tail(xs, oa_s, ob_s, proj_s, p_s, lw, tm=ns, tb=ns, gate_col0=gate_col0)
        ka_s = proj_s[:, sb_width:2 * sb_width].reshape(dec_batch, rows_pad, sb_heads, sb_dim)[:, :dec_seq]
        va_s = proj_s[:, 2 * sb_width:3 * sb_width].reshape(dec_batch, rows_pad, sb_heads, sb_dim)[:, :dec_seq]
        kn.append(ka_s)
        vn.append(va_s)
        sn_out.append(st_s)

    y_prompt = xp.reshape(batch, seq, d)
    y_sample = xs.reshape(dec_batch, rows_pad, d)[:, :dec_seq]
    return (y_prompt, y_sample, jnp.stack(kp), jnp.stack(vp), jnp.stack(sp), jnp.stack(kn), jnp.stack(vn),
            jnp.stack(sn_out))
```

```python
import functools

import numpy as np
import jax
import jax.numpy as jnp
from jax import lax
from jax.experimental import pallas as pl
from jax.experimental.pallas import tpu as pltpu

F32 = jnp.float32
BF16 = jnp.bfloat16
I32 = jnp.int32
U32 = jnp.uint32
HIGHEST = lax.Precision.HIGHEST

RMS_EPS = 1e-6
ROPE_BASE = 10000.0
PEER_TOPK = 16
LANE = 128
SUBLANE = 8
VMEM_LIMIT = 56 * 1024 * 1024
NT_DIMS = (((1,), (1,)), ((), ()))
TN_DIMS = (((0,), (0,)), ((), ()))


def _cparams(sem, vmem=None):
    return pltpu.CompilerParams(dimension_semantics=sem, vmem_limit_bytes=vmem)


def _proj_kernel(x_ref, g_ref, w_ref, cs_ref, sn_ref, qg_ref, kg_ref, gm_ref, o_ref, hn_ref, *, kb_scale):
    j = pl.program_id(1)

    @pl.when(j == 0)
    def _():
        x = x_ref[...]
        ms = jnp.mean(x * x, axis=-1, keepdims=True)
        hn_ref[...] = (x * lax.rsqrt(ms + RMS_EPS) * g_ref[...]).astype(BF16)

    acc = jnp.dot(hn_ref[...], w_ref[...], preferred_element_type=F32)
    is_qk = j <= 1
    is_rot = (j == 3) | (j == 4)

    @pl.when(is_qk)
    def _():
        gain = jnp.where(j == 0, qg_ref[...], kg_ref[...])
        ms = jnp.dot(acc * acc, gm_ref[...], precision=HIGHEST, preferred_element_type=F32)
        o_ref[...] = acc * lax.rsqrt(ms + RMS_EPS) * gain

    @pl.when(is_rot)
    def _():
        scale = jnp.where(j == 4, kb_scale, 1.0).astype(F32)
        cs = cs_ref[...]
        sn = sn_ref[...]
        for h in range(o_ref.shape[1] // LANE):
            xh = acc[:, h * LANE:(h + 1) * LANE]
            r = pltpu.roll(xh, LANE // 2, 1)
            o_ref[:, h * LANE:(h + 1) * LANE] = (xh * cs + r * sn) * scale

    @pl.when(jnp.logical_not(is_qk | is_rot))
    def _():
        o_ref[...] = acc


def _proj_call(x2d, g, w_bf, cs, sn, qg, kg, gm, *, tm, pos_blocks, kb_scale):
    n, d = x2d.shape
    pw = w_bf.shape[1]
    tn = 512
    return pl.pallas_call(
        functools.partial(_proj_kernel, kb_scale=kb_scale),
        grid=(n // tm, pw // tn),
        in_specs=[
            pl.BlockSpec((tm, d), lambda i, j: (i, 0)),
            pl.BlockSpec((1, d), lambda i, j: (0, 0)),
            pl.BlockSpec((d, tn), lambda i, j: (0, j)),
            pl.BlockSpec((tm, LANE), lambda i, j: (i % pos_blocks, 0)),
            pl.BlockSpec((tm, LANE), lambda i, j: (i % pos_blocks, 0)),
            pl.BlockSpec((1, tn), lambda i, j: (0, 0)),
            pl.BlockSpec((1, tn), lambda i, j: (0, 0)),
            pl.BlockSpec((tn, tn), lambda i, j: (0, 0)),
        ],
        out_specs=pl.BlockSpec((tm, tn), lambda i, j: (i, j)),
        out_shape=jax.ShapeDtypeStruct((n, pw), F32),
        scratch_shapes=[pltpu.VMEM((tm, d), BF16)],
        compiler_params=_cparams(("arbitrary", "arbitrary"), VMEM_LIMIT),
    )(x2d, g, w_bf, cs, sn, qg, kg, gm)


def _sb_block(z, mask, tri, car):
    e = jnp.exp(-jnp.abs(z))
    sp = jnp.maximum(z, 0.0) + jnp.log(1.0 + e)
    spm = sp if mask is None else jnp.where(mask, sp, 0.0)
    hi = spm.astype(BF16)
    lo = (spm - hi.astype(F32)).astype(BF16)
    cs = jnp.dot(hi, tri, preferred_element_type=F32) + jnp.dot(lo, tri, preferred_element_type=F32)
    a = jnp.exp(z - sp - cs[:, :LANE] - car)
    if mask is not None:
        a = jnp.where(mask, a, 0.0)
    return a, car + cs[:, LANE:]


def _sb_prompt_kernel(bias_ref, q_ref, k_ref, v_ref, tri_ref, o_ref, qm_ref, acc_ref, car_ref, *, scale):
    i = pl.program_id(1)
    tq = q_ref.shape[0]
    heads = qm_ref.shape[0]
    lane = lax.broadcasted_iota(I32, (1, LANE), 1)
    hmask = [lane < LANE // 2, lane >= LANE // 2]
    for h in range(heads):
        qt = q_ref[:, (h // 2) * LANE:(h // 2 + 1) * LANE] * scale
        qm_ref[h] = jnp.where(hmask[h % 2], qt, 0.0).astype(BF16)
    tri = tri_ref[...]
    acc_ref[...] = jnp.zeros_like(acc_ref)
    car_ref[...] = jnp.zeros_like(car_ref)
    row = lax.broadcasted_iota(I32, (tq, LANE), 0)
    col = lax.broadcasted_iota(I32, (tq, LANE), 1)
    causal = col < row

    def block(kb, mask):
        start = pl.multiple_of(kb * LANE, LANE)
        for t in range(heads // 2):
            cols = slice(t * LANE, (t + 1) * LANE)
            kblk = k_ref[pl.ds(start, LANE), cols].astype(BF16)
            vblk = v_ref[pl.ds(start, LANE), cols]
            contrib = None
            for hh in range(2):
                h = 2 * t + hh
                z = lax.dot_general(qm_ref[h], kblk, NT_DIMS, preferred_element_type=F32) + bias_ref[h]
                a, car = _sb_block(z, mask, tri, car_ref[h])
                car_ref[h] = car
                vm = jnp.where(hmask[hh], vblk, 0.0).astype(BF16)
                d = jnp.dot(a.astype(BF16), vm, preferred_element_type=F32)
                contrib = d if contrib is None else contrib + d
            acc_ref[:, cols] += contrib

    block(i, causal)

    def body(jj, c):
        block(i - 1 - jj, None)
        return c

    lax.fori_loop(0, i, body, 0)
    o_ref[...] = acc_ref[...]


def _sb_prompt_call(proj, bias, tri, *, batch, seq, heads, head_dim):
    tq = LANE
    nq = seq // tq
    width = heads * head_dim
    return pl.pallas_call(
        functools.partial(_sb_prompt_kernel, scale=head_dim ** -0.5),
        grid=(batch, nq),
        in_specs=[
            pl.BlockSpec(memory_space=pltpu.SMEM),
            pl.BlockSpec((tq, width), lambda b, i: (b * nq + i, 0)),
            pl.BlockSpec((seq, width), lambda b, i: (b, 1)),
            pl.BlockSpec((seq, width), lambda b, i: (b, 2)),
            pl.BlockSpec((LANE, 2 * LANE), lambda b, i: (0, 0)),
        ],
        out_specs=pl.BlockSpec((tq, width), lambda b, i: (b * nq + i, 0)),
        out_shape=jax.ShapeDtypeStruct((batch * seq, width), F32),
        scratch_shapes=[
            pltpu.VMEM((heads, tq, LANE), BF16),
            pltpu.VMEM((tq, width), F32),
            pltpu.VMEM((heads, tq, LANE), F32),
        ],
        compiler_params=_cparams(("arbitrary", "arbitrary"), VMEM_LIMIT),
    )(bias, proj, proj, proj, tri)


def _sb_sample_kernel(pt_ref, q_ref, kn_ref, vn_ref, *rest, scale, n_new, pages_per_step):
    kc_refs = rest[:pages_per_step]
    vc_refs = rest[pages_per_step:2 * pages_per_step]
    tri_ref, hm_ref, bias_ref, o_ref, qbd_ref, pad_ref, acc_ref, car_ref = rest[2 * pages_per_step:]
    p = pl.program_id(1)
    rows = qbd_ref.shape[0]
    heads = hm_ref.shape[0]
    tri = tri_ref[...]
    bias = bias_ref[...]

    def blocks(kv, mask):
        car = car_ref[...]
        total = None
        for kblk, vblk in kv:
            z = lax.dot_general(qbd_ref[...], kblk.astype(BF16), NT_DIMS, preferred_element_type=F32) + bias
            a, car = _sb_block(z, mask, tri, car)
            d = jnp.dot(a.astype(BF16), vblk.astype(BF16), preferred_element_type=F32)
            total = d if total is None else total + d
        car_ref[...] = car
        acc_ref[...] += total

    @pl.when(p == 0)
    def _():
        hm = hm_ref[...]
        for i in range(n_new):
            qi = q_ref[i:i + 1, :] * scale
            qbd_ref[i * heads:(i + 1) * heads, :] = jnp.where(hm > 0, qi, 0.0).astype(BF16)
        acc_ref[...] = jnp.zeros_like(acc_ref)
        car_ref[...] = jnp.zeros_like(car_ref)
        qrow = lax.broadcasted_iota(I32, (rows, LANE), 0) // heads
        col = lax.broadcasted_iota(I32, (rows, LANE), 1)
        mask = col < qrow
        pad_ref[...] = jnp.zeros_like(pad_ref)
        pad_ref[0, 0:kn_ref.shape[0], :] = kn_ref[...]
        pad_ref[1, 0:vn_ref.shape[0], :] = vn_ref[...]
        blocks([(pad_ref[0], pad_ref[1])], mask)

    blocks([(kc[...], vc[...]) for kc, vc in zip(kc_refs, vc_refs)], None)

    @pl.when(p == pl.num_programs(1) - 1)
    def _():
        hm = hm_ref[...]
        o_ref[...] = jnp.zeros_like(o_ref)
        for i in range(n_new):
            blk = acc_ref[i * heads:(i + 1) * heads, :] * hm
            o_ref[i:i + 1, :] = jnp.sum(blk, axis=0, keepdims=True)


def _sb_sample_call(page_table, proj_s, cache_k3, cache_v3, tri, hm, bias_col, *, dec_batch, rows_pad, n_new,
                    heads, head_dim):
    n_pages = page_table.shape[1]
    width = heads * head_dim
    page = cache_k3.shape[1]
    rows = n_new * heads
    pps = max(c for c in (8, 4, 2, 1) if n_pages % c == 0)

    def page_spec(j):
        return pl.BlockSpec((None, page, width),
                            lambda b, p, pt: (pt[b * n_pages + n_pages - 1 - (p * pps + j)], 0, 0))

    grid_spec = pltpu.PrefetchScalarGridSpec(
        num_scalar_prefetch=1,
        grid=(dec_batch, n_pages // pps),
        in_specs=[
            pl.BlockSpec((rows_pad, width), lambda b, p, pt: (b, 0)),
            pl.BlockSpec((rows_pad, width), lambda b, p, pt: (b, 1)),
            pl.BlockSpec((rows_pad, width), lambda b, p, pt: (b, 2)),
            *[page_spec(j) for j in range(pps)],
            *[page_spec(j) for j in range(pps)],
            pl.BlockSpec((LANE, 2 * LANE), lambda b, p, pt: (0, 0)),
            pl.BlockSpec((heads, width), lambda b, p, pt: (0, 0)),
            pl.BlockSpec((rows, 1), lambda b, p, pt: (0, 0)),
        ],
        out_specs=pl.BlockSpec((rows_pad, width), lambda b, p, pt: (b, 0)),
        scratch_shapes=[
            pltpu.VMEM((rows, width), BF16),
            pltpu.VMEM((2, page, width), F32),
            pltpu.VMEM((rows, width), F32),
            pltpu.VMEM((rows, LANE), F32),
        ],
    )
    return pl.pallas_call(
        functools.partial(_sb_sample_kernel, scale=head_dim ** -0.5, n_new=n_new, pages_per_step=pps),
        grid_spec=grid_spec,
        out_shape=jax.ShapeDtypeStruct((dec_batch * rows_pad, width), F32),
        compiler_params=_cparams(("arbitrary", "arbitrary"), VMEM_LIMIT),
    )(page_table.reshape(-1), proj_s, proj_s, proj_s, *([cache_k3] * pps), *([cache_v3] * pps), tri, hm, bias_col)


def _ret_kernel(q_ref, k_ref, v_ref, gb_ref, st0_ref, dec_ref, qd_ref, kd_ref, sd_ref, rg_ref,
                ob_ref, st_ref, state_ref, qp_ref, kp_ref, vp_ref, *, use_init):
    c = pl.program_id(2)
    rows = q_ref.shape[0]
    chunk = dec_ref.shape[0]

    @pl.when(c == 0)
    def _():
        if use_init:
            state_ref[...] = st0_ref[...]
        else:
            state_ref[...] = jnp.zeros_like(state_ref)

    if rows == chunk:
        q, k, v = q_ref[...], k_ref[...], v_ref[...]
    else:
        qp_ref[...] = jnp.zeros_like(qp_ref)
        kp_ref[...] = jnp.zeros_like(kp_ref)
        vp_ref[...] = jnp.zeros_like(vp_ref)
        qp_ref[0:rows, :] = q_ref[...]
        kp_ref[0:rows, :] = k_ref[...]
        vp_ref[0:rows, :] = v_ref[...]
        q, k, v = qp_ref[...], kp_ref[...], vp_ref[...]

    qb = q.astype(BF16)
    vb = v.astype(BF16)
    scores = lax.dot_general(qb, k.astype(BF16), NT_DIMS, preferred_element_type=F32) * dec_ref[...]
    inner = jnp.dot(scores.astype(BF16), vb, preferred_element_type=F32)
    st = state_ref[...]
    cross = jnp.dot(qb, st.astype(BF16), preferred_element_type=F32) * qd_ref[...]
    o = (inner + cross)[0:rows, :]
    kd = (k * kd_ref[...]).astype(BF16)
    new = sd_ref[...] * st + lax.dot_general(kd, vb, TN_DIMS, preferred_element_type=F32)
    state_ref[...] = new

    ms = jnp.mean(o * o, axis=-1, keepdims=True)
    gb = gb_ref[...]
    ob_ref[...] = (o * lax.rsqrt(ms + RMS_EPS) * rg_ref[...]) * (gb * jax.nn.sigmoid(gb))

    @pl.when(c == pl.num_programs(2) - 1)
    def _():
        st_ref[...] = new


def _ret_call(proj, state0, dec, qd, kd, sd, rg, *, batch, n_chunks, rows, heads, dk, dv, col0):
    chunk = dec.shape[1]
    qc = col0 // dk
    kc = qc + heads
    vc = (col0 + 2 * heads * dk) // dv
    gc = vc + heads
    use_init = state0 is not None
    if state0 is None:
        state0 = jnp.zeros((1, 1, dk, dv), F32)
        st_map = lambda b, h, c: (0, 0, 0, 0)
    else:
        st_map = lambda b, h, c: (b, h, 0, 0)
    pad_rows = chunk if rows != chunk else SUBLANE
    return pl.pallas_call(
        functools.partial(_ret_kernel, use_init=use_init),
        grid=(batch, heads, n_chunks),
        in_specs=[
            pl.BlockSpec((rows, dk), lambda b, h, c: (b * n_chunks + c, qc + h)),
            pl.BlockSpec((rows, dk), lambda b, h, c: (b * n_chunks + c, kc + h)),
            pl.BlockSpec((rows, dv), lambda b, h, c: (b * n_chunks + c, vc + h)),
            pl.BlockSpec((rows, dv), lambda b, h, c: (b * n_chunks + c, gc + h)),
            pl.BlockSpec((None, None, dk, dv), st_map),
            pl.BlockSpec((None, chunk, chunk), lambda b, h, c: (h, 0, 0)),
            pl.BlockSpec((None, chunk, 1), lambda b, h, c: (h, 0, 0)),
            pl.BlockSpec((None, chunk, 1), lambda b, h, c: (h, 0, 0)),
            pl.BlockSpec((None, 1, 1), lambda b, h, c: (h, 0, 0)),
            pl.BlockSpec((None, 1, dv), lambda b, h, c: (h, 0, 0)),
        ],
        out_specs=[
            pl.BlockSpec((rows, dv), lambda b, h, c: (b * n_chunks + c, h)),
            pl.BlockSpec((None, None, dk, dv), lambda b, h, c: (b, h, 0, 0)),
        ],
        out_shape=[
            jax.ShapeDtypeStruct((batch * n_chunks * rows, heads * dv), F32),
            jax.ShapeDtypeStruct((batch, heads, dk, dv), F32),
        ],
        scratch_shapes=[
            pltpu.VMEM((dk, dv), F32),
            pltpu.VMEM((pad_rows, dk), F32),
            pltpu.VMEM((pad_rows, dk), F32),
            pltpu.VMEM((pad_rows, dv), F32),
        ],
        compiler_params=_cparams(("arbitrary", "arbitrary", "arbitrary")),
    )(proj, proj, proj, proj, state0, dec, qd, kd, sd, rg)


def _merge_kernel(x_ref, oa_ref, ob_ref, ga0_ref, ga1_ref, gb0_ref, gb1_ref, wa_ref, wb_ref, wo_ref, gf_ref,
                  wq_ref, x1_ref, h2_ref, qp_ref):
    ya = jnp.dot(oa_ref[...].astype(BF16), wa_ref[...], preferred_element_type=F32)
    yb = jnp.dot(ob_ref[...].astype(BF16), wb_ref[...], preferred_element_type=F32)
    ga = jnp.concatenate([ga0_ref[...], ga1_ref[...]], axis=1)
    gb = jnp.concatenate([gb0_ref[...], gb1_ref[...]], axis=1)
    merged = jax.nn.sigmoid(ga) * ya + jax.nn.sigmoid(gb) * yb
    x1 = x_ref[...] + jnp.dot(merged.astype(BF16), wo_ref[...], preferred_element_type=F32)
    x1_ref[...] = x1
    ms = jnp.mean(x1 * x1, axis=-1, keepdims=True)
    h2 = x1 * lax.rsqrt(ms + RMS_EPS) * gf_ref[...]
    h2_ref[...] = h2
    qp_ref[...] = jnp.dot(h2.astype(BF16), wq_ref[...], preferred_element_type=F32)


def _merge_call(x2d, oa, ob, proj, wa, wb, wo, gf, wq, *, tm, gate_col0):
    n, d = x2d.shape
    half = d // 2
    g0 = gate_col0 // half
    row = lambda i: (i, 0)
    const = lambda i: (0, 0)
    return pl.pallas_call(
        _merge_kernel,
        grid=(n // tm,),
        in_specs=[
            pl.BlockSpec((tm, d), row),
            pl.BlockSpec((tm, oa.shape[1]), row),
            pl.BlockSpec((tm, ob.shape[1]), row),
            pl.BlockSpec((tm, half), lambda i: (i, g0)),
            pl.BlockSpec((tm, half), lambda i: (i, g0 + 1)),
            pl.BlockSpec((tm, half), lambda i: (i, g0 + 2)),
            pl.BlockSpec((tm, half), lambda i: (i, g0 + 3)),
            pl.BlockSpec(wa.shape, const),
            pl.BlockSpec(wb.shape, const),
            pl.BlockSpec(wo.shape, const),
            pl.BlockSpec((1, d), const),
            pl.BlockSpec(wq.shape, const),
        ],
        out_specs=[
            pl.BlockSpec((tm, d), row),
            pl.BlockSpec((tm, d), row),
            pl.BlockSpec((tm, wq.shape[1]), row),
        ],
        out_shape=[
            jax.ShapeDtypeStruct((n, d), F32),
            jax.ShapeDtypeStruct((n, d), F32),
            jax.ShapeDtypeStruct((n, wq.shape[1]), F32),
        ],
        compiler_params=_cparams(("arbitrary",), VMEM_LIMIT),
    )(x2d, oa, ob, proj, proj, proj, proj, wa, wb, wo, gf, wq)


def _topk_rows(s, codes, k):
    t = s.shape[1]
    out_row = lax.broadcasted_iota(I32, (k, t), 0)
    vals = jnp.zeros((k, t), F32)
    picked = jnp.zeros((k, t), F32)
    for r in range(k):
        m = jnp.max(s, axis=0, keepdims=True)
        am = jnp.min(jnp.where(s == m, codes, jnp.inf), axis=0, keepdims=True)
        vals = jnp.where(out_row == r, m, vals)
        picked = jnp.where(out_row == r, am, picked)
        s = jnp.where(codes == am, -jnp.inf, s)
    return vals, picked


def _select_rows(table, sel):
    out = jnp.zeros(sel.shape, table.dtype)
    for a in range(table.shape[0]):
        out = jnp.where(sel == a, table[a:a + 1, :], out)
    return out


def _pair_candidates(s1, s2):
    k, t = s1.shape
    blk = SUBLANE
    a_lo = lax.broadcasted_iota(I32, (blk, t), 0)
    vals, codes = [], []
    for a0 in range(0, k, blk):
        vals.append(s1[a0:a0 + blk, :] + s2[0:1, :])
        codes.append((a_lo + a0) * k)
    for b in range(1, blk):
        ok = a_lo < k // (b + 1)
        vals.append(jnp.where(ok, s1[0:blk, :] + s2[b:b + 1, :], -jnp.inf))
        codes.append(a_lo * k + b)
    for b0 in range(blk, k, blk):
        vals.append(s1[0:1, :] + s2[b0:b0 + blk, :])
        codes.append(a_lo + b0)
    return jnp.concatenate(vals, axis=0), jnp.concatenate(codes, axis=0).astype(F32)


def _peer_topk_kernel(qp_ref, keys_ref, e_ref, g_ref, *, n_keys):
    heads = keys_ref.shape[0]
    half = keys_ref.shape[3]
    k = PEER_TOPK
    t = qp_ref.shape[0]
    key_codes = lax.broadcasted_iota(I32, (n_keys, t), 0).astype(F32)
    e_rows, g_rows = [], []
    for h in range(heads):
        tops = []
        for c in range(2):
            col = (h * 2 + c) * half
            q = qp_ref[:, col:col + half].astype(BF16)
            st = lax.dot_general(keys_ref[h, c].astype(BF16), q, NT_DIMS, preferred_element_type=F32)
            tops.append(_topk_rows(st, key_codes, k))
        (s1, i1), (s2, i2) = tops
        cand, codes = _pair_candidates(s1, s2)
        top, flat = _topk_rows(cand, codes, k)
        flat = flat.astype(I32)
        ia = _select_rows(i1, flat // k)
        ib = _select_rows(i2, flat % k)
        e_rows.append(((ia * n_keys + ib) * EXPERT_ROWS).astype(I32))
        ex = jnp.exp(top - jnp.max(top, axis=0, keepdims=True))
        g_rows.append(ex / jnp.sum(ex, axis=0, keepdims=True))
    e_ref[...] = jnp.transpose(jnp.concatenate(e_rows, axis=0))
    g_ref[...] = jnp.transpose(jnp.concatenate(g_rows, axis=0))


def _peer_topk_call(qp, keys, *, tm):
    n = qp.shape[0]
    heads, _, n_keys, half = keys.shape
    slots = heads * PEER_TOPK
    return pl.pallas_call(
        functools.partial(_peer_topk_kernel, n_keys=n_keys),
        grid=(n // tm,),
        in_specs=[
            pl.BlockSpec((tm, qp.shape[1]), lambda i: (i, 0)),
            pl.BlockSpec(keys.shape, lambda i: (0, 0, 0, 0)),
        ],
        out_specs=[
            pl.BlockSpec((tm, slots), lambda i: (i, 0)),
            pl.BlockSpec((tm, slots), lambda i: (i, 0)),
        ],
        out_shape=[
            jax.ShapeDtypeStruct((n, slots), I32),
            jax.ShapeDtypeStruct((n, slots), F32),
        ],
        compiler_params=_cparams(("arbitrary",), VMEM_LIMIT),
    )(qp, keys)


EXPERT_ROWS = 4


def _pack_table(tab):
    n, d = tab.shape
    assert d == 2 * EXPERT_ROWS * LANE
    bits = lax.bitcast_convert_type(tab.astype(BF16), jnp.uint16).astype(U32)
    packed = bits[:, :d // 2] | (bits[:, d // 2:] << 16)
    return packed.reshape(n * EXPERT_ROWS, LANE)


PACKED_ROWS = 2 * EXPERT_ROWS


def _gather_rows(tab_ref, idx_ref, t, dst_ref):
    for r in range(idx_ref.shape[1]):
        off = pl.multiple_of(idx_ref[t, r], EXPERT_ROWS)
        dst_ref[r * EXPERT_ROWS:(r + 1) * EXPERT_ROWS, :] = tab_ref[pl.ds(off, EXPERT_ROWS), :]


def _pack_bf16_pair(lo, hi):
    lo_bits = lax.bitcast_convert_type(lo.astype(BF16).astype(F32), U32) >> 16
    hi_bits = lax.bitcast_convert_type(hi.astype(BF16).astype(F32), U32) & jnp.uint32(0xFFFF0000)
    return lo_bits | hi_bits


def _peer_u_kernel(idx_ref, h_ref, g_ref, tab_ref, fold_ref, w_ref, p_ref):
    tb, slots = idx_ref.shape
    cols = slots * PACKED_ROWS
    sub = lax.broadcasted_iota(I32, (SUBLANE, cols), 0)
    lane = lax.broadcasted_iota(I32, (SUBLANE, cols), 1)
    diag = (lane & (PACKED_ROWS - 1)) == sub
    fold = fold_ref[...]

    def group(t8, carry):
        v8 = jnp.zeros((SUBLANE, cols), F32)
        for s in range(SUBLANE):
            t = t8 * SUBLANE + s
            _gather_rows(tab_ref, idx_ref, t, p_ref.at[s])
            base = pl.multiple_of(t * SUBLANE, SUBLANE)
            hw = _pack_bf16_pair(h_ref[pl.ds(base, EXPERT_ROWS), :], h_ref[pl.ds(base + EXPERT_ROWS, EXPERT_ROWS), :])
            m = lax.dot_general(pltpu.bitcast(hw, BF16), pltpu.bitcast(p_ref[s], BF16), NT_DIMS,
                                preferred_element_type=F32)
            v = jnp.sum(jnp.where(diag, m, 0.0), axis=0, keepdims=True)
            v8 = jnp.where(sub == s, v, v8)
        hi = v8.astype(BF16)
        lo = (v8 - hi.astype(F32)).astype(BF16)
        act = jnp.dot(hi, fold, preferred_element_type=F32) + jnp.dot(lo, fold, preferred_element_type=F32)
        rows = pl.ds(pl.multiple_of(t8 * SUBLANE, SUBLANE), SUBLANE)
        w_ref[rows, :] = g_ref[rows, :] * jax.nn.gelu(act)
        return carry

    lax.fori_loop(0, tb // SUBLANE, group, 0)


def _peer_u_call(idx, h2, g, tab, *, tb):
    n, slots = idx.shape
    d = h2.shape[1]
    rows_per_tok = d // LANE
    h3 = h2.reshape(n * rows_per_tok, LANE)
    fold = jnp.repeat(jnp.eye(slots, dtype=BF16), PACKED_ROWS, axis=0)
    return pl.pallas_call(
        _peer_u_kernel,
        grid=(n // tb,),
        in_specs=[
            pl.BlockSpec((tb, slots), lambda i: (i, 0), memory_space=pltpu.SMEM),
            pl.BlockSpec((tb * rows_per_tok, LANE), lambda i: (i, 0)),
            pl.BlockSpec((tb, slots), lambda i: (i, 0)),
            pl.BlockSpec(tab.shape, lambda i: (0, 0), pipeline_mode=pl.Buffered(1)),
            pl.BlockSpec(fold.shape, lambda i: (0, 0)),
        ],
        out_specs=pl.BlockSpec((tb, slots), lambda i: (i, 0)),
        out_shape=jax.ShapeDtypeStruct((n, slots), F32),
        scratch_shapes=[pltpu.VMEM((SUBLANE, slots * EXPERT_ROWS, LANE), U32)],
        compiler_params=_cparams(("arbitrary",), VMEM_LIMIT),
    )(idx, h3, g, tab, fold)


def _peer_v_kernel(idx_ref, w_ref, x_ref, tab_ref, expand_ref, o_ref, p_ref):
    tb, slots = idx_ref.shape
    cols = slots * PACKED_ROWS
    sub = lax.broadcasted_iota(I32, (SUBLANE, cols), 0)
    j = lax.broadcasted_iota(I32, (SUBLANE, cols), 1) & (PACKED_ROWS - 1)
    probe = pltpu.bitcast(jnp.full((EXPERT_ROWS, LANE), 0x40003F80, U32), BF16).astype(F32)
    low_first = probe[0:1, 0:1] == 1.0
    half = jnp.where(low_first, j & 1, 1 - (j & 1))
    place = (half * EXPERT_ROWS + (j >> 1)) == sub
    expand = expand_ref[...]

    def group(t8, carry):
        rows8 = pl.ds(pl.multiple_of(t8 * SUBLANE, SUBLANE), SUBLANE)
        wrep = jnp.dot(w_ref[rows8, :].astype(BF16), expand, preferred_element_type=F32)
        for s in range(SUBLANE):
            t = t8 * SUBLANE + s
            _gather_rows(tab_ref, idx_ref, t, p_ref.at[s])
            lhs = jnp.where(place, wrep[s:s + 1, :], 0.0).astype(BF16)
            out = jnp.dot(lhs, pltpu.bitcast(p_ref[s], BF16), preferred_element_type=F32)
            rows = pl.ds(pl.multiple_of(t * SUBLANE, SUBLANE), SUBLANE)
            o_ref[rows, :] = x_ref[rows, :] + out
        return carry

    lax.fori_loop(0, tb // SUBLANE, group, 0)


def _peer_v_call(idx, w, x1, tab, *, tb):
    n, slots = idx.shape
    d = x1.shape[1]
    rows_per_tok = d // LANE
    x3 = x1.reshape(n * rows_per_tok, LANE)
    expand = jnp.repeat(jnp.eye(slots, dtype=BF16), PACKED_ROWS, axis=1)
    out = pl.pallas_call(
        _peer_v_kernel,
        grid=(n // tb,),
        in_specs=[
            pl.BlockSpec((tb, slots), lambda i: (i, 0), memory_space=pltpu.SMEM),
            pl.BlockSpec((tb, slots), lambda i: (i, 0)),
            pl.BlockSpec((tb * rows_per_tok, LANE), lambda i: (i, 0)),
            pl.BlockSpec(tab.shape, lambda i: (0, 0), pipeline_mode=pl.Buffered(1)),
            pl.BlockSpec(expand.shape, lambda i: (0, 0)),
        ],
        out_specs=pl.BlockSpec((tb * rows_per_tok, LANE), lambda i: (i, 0)),
        out_shape=jax.ShapeDtypeStruct((n * rows_per_tok, LANE), F32),
        scratch_shapes=[pltpu.VMEM((SUBLANE, slots * EXPERT_ROWS, LANE), U32)],
        compiler_params=_cparams(("arbitrary",), VMEM_LIMIT),
    )(idx, w, x3, tab, expand)
    return out.reshape(n, d)


def _ple_kernel(x_ref, p_ref, g_ref, wg_ref, wp_ref, o_ref):
    x = x_ref[...]
    ms = jnp.mean(x * x, axis=-1, keepdims=True)
    hn = (x * lax.rsqrt(ms + RMS_EPS) * g_ref[...]).astype(BF16)
    gate = jax.nn.sigmoid(jnp.dot(hn, wg_ref[...], preferred_element_type=F32))
    pe = jnp.dot(p_ref[...].astype(BF16), wp_ref[...], preferred_element_type=F32)
    o_ref[...] = x + gate * pe


def _ple_call(x2, p, g, wg, wp, *, tm):
    n, d = x2.shape
    return pl.pallas_call(
        _ple_kernel,
        grid=(n // tm,),
        in_specs=[
            pl.BlockSpec((tm, d), lambda i: (i, 0)),
            pl.BlockSpec((tm, p.shape[1]), lambda i: (i, 0)),
            pl.BlockSpec((1, d), lambda i: (0, 0)),
            pl.BlockSpec(wg.shape, lambda i: (0, 0)),
            pl.BlockSpec(wp.shape, lambda i: (0, 0)),
        ],
        out_specs=pl.BlockSpec((tm, d), lambda i: (i, 0)),
        out_shape=jax.ShapeDtypeStruct((n, d), F32),
        compiler_params=_cparams(("arbitrary",), VMEM_LIMIT),
    )(x2, p, g, wg, wp)


def _rope_tables(pos, dim):
    half = dim // 2
    inv = ROPE_BASE ** (-jnp.arange(half, dtype=F32) / half)
    ang = pos.astype(F32)[:, None] * inv[None, :]
    cos, sin = jnp.cos(ang), jnp.sin(ang)
    return jnp.concatenate([cos, cos], axis=1), jnp.concatenate([-sin, sin], axis=1)


def _ret_tables(heads, chunk, n_valid):
    lg = jnp.log1p(-jnp.exp2(-5.0 - jnp.arange(heads, dtype=F32)))
    idx = jnp.arange(chunk, dtype=F32)
    diff = idx[:, None] - idx[None, :]
    causal = diff >= 0
    dec = jnp.where(causal[None], jnp.exp(jnp.where(causal, diff, 0.0)[None] * lg[:, None, None]), 0.0)
    qd = jnp.exp((idx[None, :] + 1.0) * lg[:, None])
    kd = jnp.exp((n_valid - 1.0 - idx)[None, :] * lg[:, None])
    kd = jnp.where(idx[None, :] < n_valid, kd, 0.0)
    sd = jnp.exp(n_valid * lg)
    return dec, qd[:, :, None], kd[:, :, None], sd[:, None, None]


def _token_tail(x2d, oa, ob, proj, p2d, lw, *, tm, tb, gate_col0):
    x1, h2, qp = _merge_call(x2d, oa, ob, proj, lw["wa"], lw["wb"], lw["wo"], lw["gf"], lw["wq"],
                             tm=tm, gate_col0=gate_col0)
    e, g = _peer_topk_call(qp, lw["keys"], tm=tm)
    w = _peer_u_call(e, h2, g, lw["u_tab"], tb=tb)
    x2 = _peer_v_call(e, w, x1, lw["v_tab"], tb=tb)
    return _ple_call(x2, p2d, lw["gp"], lw["wg"], lw["wp"], tm=tm)


def kernel(x_prompt, x_sample, cache_k, cache_v, state_ret, page_table, p_prompt, p_sample, norm_mix_g, w_in,
           q_norm_g, k_norm_g, sb_bias, ret_norm_g, w_a_up, w_b_up, w_out, norm_ffn_g, peer_w_q, peer_sub_keys,
           peer_u, peer_v, norm_ple_g, w_ple_gate, w_ple):
    batch, seq, d = x_prompt.shape
    dec_batch, dec_seq, _ = x_sample.shape
    depth, n_phys, page, sb_heads, sb_dim = cache_k.shape
    _, _, ret_heads, dk, dv = state_ret.shape
    sb_width = sb_heads * sb_dim
    past_len = page_table.shape[1] * page
    chunk = LANE
    rows_pad = SUBLANE
    assert dec_seq <= rows_pad and dk == LANE and sb_width % LANE == 0
    qb_col0 = 3 * sb_width
    gate_col0 = qb_col0 + 2 * ret_heads * dk + 2 * ret_heads * dv

    xp = x_prompt.reshape(batch * seq, d)
    xs = jnp.pad(x_sample, ((0, 0), (0, rows_pad - dec_seq), (0, 0))).reshape(dec_batch * rows_pad, d)
    ns = dec_batch * rows_pad

    cs_p, sn_p = _rope_tables(jnp.arange(seq), dk)
    cs_s, sn_s = _rope_tables(past_len + jnp.arange(rows_pad), dk)
    cs_s, sn_s = jnp.tile(cs_s, (dec_batch, 1)), jnp.tile(sn_s, (dec_batch, 1))
    gm = jnp.kron(jnp.eye(sb_heads, dtype=F32), jnp.full((sb_dim, sb_dim), 1.0 / sb_dim, F32))
    tri = jnp.concatenate([jnp.tril(jnp.ones((LANE, LANE), F32), -1), jnp.ones((LANE, LANE), F32)], axis=1).astype(BF16)
    head_mask = jnp.kron(jnp.eye(sb_heads, dtype=F32), jnp.ones((1, sb_dim), F32))
    dec_p = _ret_tables(ret_heads, chunk, float(chunk))
    dec_s = _ret_tables(ret_heads, chunk, float(dec_seq))

    tm = 256
    tb = 256
    kp, vp, sp, kn, vn, sn_out = [], [], [], [], [], []
    for i in range(depth):
        lw = dict(
            wa=w_a_up[i].astype(BF16), wb=w_b_up[i].astype(BF16), wo=w_out[i].astype(BF16),
            gf=norm_ffn_g[i][None, :], wq=peer_w_q[i].astype(BF16), keys=peer_sub_keys[i],
            u_tab=_pack_table(peer_u[i]), v_tab=_pack_table(peer_v[i]),
            gp=norm_ple_g[i][None, :], wg=w_ple_gate[i].astype(BF16), wp=w_ple[i].astype(BF16),
        )
        w_bf = w_in[i].astype(BF16)
        g_mix = norm_mix_g[i][None, :]
        qg = jnp.tile(q_norm_g[i], sb_heads)[None, :]
        kg = jnp.tile(k_norm_g[i], sb_heads)[None, :]
        rg = ret_norm_g[i][:, None, :]
        kb_scale = dk ** -0.5

        tm_proj = 512 if seq % 512 == 0 else tm
        proj = _proj_call(xp, g_mix, w_bf, cs_p, sn_p, qg, kg, gm, tm=tm_proj, pos_blocks=seq // tm_proj,
                          kb_scale=kb_scale)
        oa = _sb_prompt_call(proj, sb_bias[i], tri, batch=batch, seq=seq, heads=sb_heads, head_dim=sb_dim)
        ob, st_p = _ret_call(proj, None, *dec_p, rg, batch=batch, n_chunks=seq // chunk, rows=chunk,
                             heads=ret_heads, dk=dk, dv=dv, col0=qb_col0)
        xp = _token_tail(xp, oa, ob, proj, p_prompt[i].reshape(batch * seq, -1), lw, tm=tm, tb=tb,
                         gate_col0=gate_col0)
        kp.append(proj[:, sb_width:2 * sb_width].reshape(batch, seq, sb_heads, sb_dim))
        vp.append(proj[:, 2 * sb_width:3 * sb_width].reshape(batch, seq, sb_heads, sb_dim))
        sp.append(st_p)

        proj_s = _proj_call(xs, g_mix, w_bf, cs_s, sn_s, qg, kg, gm, tm=ns, pos_blocks=1, kb_scale=kb_scale)
        bias_col = jnp.tile(sb_bias[i], dec_seq)[:, None]
        oa_s = _sb_sample_call(page_table, proj_s, cache_k[i].reshape(n_phys, page, sb_width),
                               cache_v[i].reshape(n_phys, page, sb_width), tri, head_mask, bias_col,
                               dec_batch=dec_batch, rows_pad=rows_pad, n_new=dec_seq, heads=sb_heads,
                               head_dim=sb_dim)
        ob_s, st_s = _ret_call(proj_s, state_ret[i], *dec_s, rg, batch=dec_batch, n_chunks=1, rows=rows_pad,
                               heads=ret_heads, dk=dk, dv=dv, col0=qb_col0)
        p_s = jnp.pad(p_sample[i], ((0, 0), (0, rows_pad - dec_seq), (0, 0))).reshape(ns, -1)
        xs = _token_tail(xs, oa_s, ob_s, proj_s, p_s, lw, tm=ns, tb=ns, gate_col0=gate_col0)
        ka_s = proj_s[:, sb_width:2 * sb_width].reshape(dec_batch, rows_pad, sb_heads, sb_dim)[:, :dec_seq]
        va_s = proj_s[:, 2 * sb_width:3 * sb_width].reshape(dec_batch, rows_pad, sb_heads, sb_dim)[:, :dec_seq]
        kn.append(ka_s)
        vn.append(va_s)
        sn_out.append(st_s)

    y_prompt = xp.reshape(batch, seq, d)
    y_sample = xs.reshape(dec_batch, rows_pad, d)[:, :dec_seq]
    return (y_prompt, y_sample, jnp.stack(kp), jnp.stack(vp), jnp.stack(sp), jnp.stack(kn), jnp.stack(vn),
            jnp.stack(sn_out))
```

```python
import functools

import numpy as np
import jax
import jax.numpy as jnp
from jax import lax
from jax.experimental import pallas as pl
from jax.experimental.pallas import tpu as pltpu

F32 = jnp.float32
BF16 = jnp.bfloat16
I32 = jnp.int32
U32 = jnp.uint32
HIGHEST = lax.Precision.HIGHEST

RMS_EPS = 1e-6
ROPE_BASE = 10000.0
PEER_TOPK = 16
LANE = 128
SUBLANE = 8
VMEM_LIMIT = 56 * 1024 * 1024
NT_DIMS = (((1,), (1,)), ((), ()))
TN_DIMS = (((0,), (0,)), ((), ()))


def _cparams(sem, vmem=None):
    return pltpu.CompilerParams(dimension_semantics=sem, vmem_limit_bytes=vmem)


def _proj_kernel(x_ref, g_ref, w_ref, cs_ref, sn_ref, qg_ref, kg_ref, gm_ref, o_ref, hn_ref, *, kb_scale):
    j = pl.program_id(1)

    @pl.when(j == 0)
    def _():
        x = x_ref[...]
        ms = jnp.mean(x * x, axis=-1, keepdims=True)
        hn_ref[...] = (x * lax.rsqrt(ms + RMS_EPS) * g_ref[...]).astype(BF16)

    acc = jnp.dot(hn_ref[...], w_ref[...], preferred_element_type=F32)
    is_qk = j <= 1
    is_rot = (j == 3) | (j == 4)

    @pl.when(is_qk)
    def _():
        gain = jnp.where(j == 0, qg_ref[...], kg_ref[...])
        ms = jnp.dot(acc * acc, gm_ref[...], precision=HIGHEST, preferred_element_type=F32)
        o_ref[...] = acc * lax.rsqrt(ms + RMS_EPS) * gain

    @pl.when(is_rot)
    def _():
        scale = jnp.where(j == 4, kb_scale, 1.0).astype(F32)
        cs = cs_ref[...]
        sn = sn_ref[...]
        for h in range(o_ref.shape[1] // LANE):
            xh = acc[:, h * LANE:(h + 1) * LANE]
            r = pltpu.roll(xh, LANE // 2, 1)
            o_ref[:, h * LANE:(h + 1) * LANE] = (xh * cs + r * sn) * scale

    @pl.when(jnp.logical_not(is_qk | is_rot))
    def _():
        o_ref[...] = acc


def _proj_call(x2d, g, w_bf, cs, sn, qg, kg, gm, *, tm, pos_blocks, kb_scale):
    n, d = x2d.shape
    pw = w_bf.shape[1]
    tn = 512
    return pl.pallas_call(
        functools.partial(_proj_kernel, kb_scale=kb_scale),
        grid=(n // tm, pw // tn),
        in_specs=[
            pl.BlockSpec((tm, d), lambda i, j: (i, 0)),
            pl.BlockSpec((1, d), lambda i, j: (0, 0)),
            pl.BlockSpec((d, tn), lambda i, j: (0, j)),
            pl.BlockSpec((tm, LANE), lambda i, j: (i % pos_blocks, 0)),
            pl.BlockSpec((tm, LANE), lambda i, j: (i % pos_blocks, 0)),
            pl.BlockSpec((1, tn), lambda i, j: (0, 0)),
            pl.BlockSpec((1, tn), lambda i, j: (0, 0)),
            pl.BlockSpec((tn, tn), lambda i, j: (0, 0)),
        ],
        out_specs=pl.BlockSpec((tm, tn), lambda i, j: (i, j)),
        out_shape=jax.ShapeDtypeStruct((n, pw), F32),
        scratch_shapes=[pltpu.VMEM((tm, d), BF16)],
        compiler_params=_cparams(("arbitrary", "arbitrary"), VMEM_LIMIT),
    )(x2d, g, w_bf, cs, sn, qg, kg, gm)


def _sb_prompt_kernel(bias_ref, q_ref, k_ref, v_ref, tri_ref, o_ref, qs_ref, acc_ref, car_ref, *, scale):
    i = pl.program_id(1)
    tq = q_ref.shape[0]
    pairs = qs_ref.shape[0]
    lane = lax.broadcasted_iota(I32, (1, LANE), 1)
    hmask = [lane < LANE // 2, lane >= LANE // 2]
    first = lax.broadcasted_iota(I32, (1, 2 * LANE), 1) < LANE
    for t in range(pairs):
        qs_ref[t] = (q_ref[:, t * LANE:(t + 1) * LANE] * scale).astype(BF16)
    tri2 = tri_ref[...]
    acc_ref[...] = jnp.zeros_like(acc_ref)
    car_ref[...] = jnp.zeros_like(car_ref)
    row = lax.broadcasted_iota(I32, (tq, 2 * LANE), 0)
    col = lax.broadcasted_iota(I32, (tq, 2 * LANE), 1) & (LANE - 1)
    causal2 = col < row

    def split_heads(x):
        return jnp.concatenate([jnp.where(hmask[0], x, 0.0), jnp.where(hmask[1], x, 0.0)], axis=0).astype(BF16)

    def block(kb, mask2):
        start = pl.multiple_of(kb * LANE, LANE)
        tiles = [slice(t * LANE, (t + 1) * LANE) for t in range(pairs)]
        z = []
        for t in range(pairs):
            rk = split_heads(k_ref[pl.ds(start, LANE), tiles[t]])
            bias2 = jnp.where(first, bias_ref[2 * t], bias_ref[2 * t + 1])
            z.append(lax.dot_general(qs_ref[t], rk, NT_DIMS, preferred_element_type=F32) + bias2)
        sp = [jnp.maximum(zt, 0.0) + jnp.log(1.0 + jnp.exp(-jnp.abs(zt))) for zt in z]
        spm = sp if mask2 is None else [jnp.where(mask2, s, 0.0) for s in sp]
        cs = []
        for t in range(pairs):
            hi = spm[t].astype(BF16)
            lo = (spm[t] - hi.astype(F32)).astype(BF16)
            cs.append(jnp.dot(hi, tri2, preferred_element_type=F32) + jnp.dot(lo, tri2, preferred_element_type=F32))
        a = []
        for t in range(pairs):
            at = jnp.exp(z[t] - sp[t] - cs[t] - car_ref[t])
            a.append(at if mask2 is None else jnp.where(mask2, at, 0.0))
            tot = [jnp.broadcast_to(jnp.sum(spm[t][:, h * LANE:(h + 1) * LANE], axis=1, keepdims=True), (tq, LANE))
                   for h in range(2)]
            car_ref[t] += jnp.concatenate(tot, axis=1)
        for t in range(pairs):
            rv = split_heads(v_ref[pl.ds(start, LANE), tiles[t]])
            acc_ref[:, tiles[t]] += jnp.dot(a[t].astype(BF16), rv, preferred_element_type=F32)

    block(i, causal2)

    def body(jj, c):
        block(i - 1 - jj, None)
        return c

    lax.fori_loop(0, i, body, 0)
    o_ref[...] = acc_ref[...]


def _sb_prompt_call(proj, bias, *, batch, seq, heads, head_dim):
    assert 2 * head_dim == LANE
    tq = LANE
    nq = seq // tq
    width = heads * head_dim
    pairs = heads // 2
    tri2 = jnp.kron(jnp.eye(2, dtype=F32), jnp.tril(jnp.ones((LANE, LANE), F32), -1)).astype(BF16)
    return pl.pallas_call(
        functools.partial(_sb_prompt_kernel, scale=head_dim ** -0.5),
        grid=(batch, nq),
        in_specs=[
            pl.BlockSpec(memory_space=pltpu.SMEM),
            pl.BlockSpec((tq, width), lambda b, i: (b * nq + i, 0)),
            pl.BlockSpec((seq, width), lambda b, i: (b, 1)),
            pl.BlockSpec((seq, width), lambda b, i: (b, 2)),
            pl.BlockSpec((2 * LANE, 2 * LANE), lambda b, i: (0, 0)),
        ],
        out_specs=pl.BlockSpec((tq, width), lambda b, i: (b * nq + i, 0)),
        out_shape=jax.ShapeDtypeStruct((batch * seq, width), F32),
        scratch_shapes=[
            pltpu.VMEM((pairs, tq, LANE), BF16),
            pltpu.VMEM((tq, width), F32),
            pltpu.VMEM((pairs, tq, 2 * LANE), F32),
        ],
        compiler_params=_cparams(("arbitrary", "arbitrary"), VMEM_LIMIT),
    )(bias, proj, proj, proj, tri2)


def _sb_sample_kernel(pt_ref, q_ref, kn_ref, vn_ref, *rest, scale, n_new, pages_per_step):
    kc_refs = rest[:pages_per_step]
    vc_refs = rest[pages_per_step:2 * pages_per_step]
    tri_ref, hm_ref, bias_ref, o_ref, qbd_ref, pad_ref, acc_ref, car_ref = rest[2 * pages_per_step:]
    p = pl.program_id(1)
    rows = qbd_ref.shape[0]
    heads = hm_ref.shape[0]
    tri = tri_ref[...]
    bias = bias_ref[...]

    def blocks(kv, mask):
        q = qbd_ref[...]
        z = [lax.dot_general(q, kblk.astype(BF16), NT_DIMS, preferred_element_type=F32) + bias for kblk, _ in kv]
        sp = [jnp.maximum(zj, 0.0) + jnp.log(1.0 + jnp.exp(-jnp.abs(zj))) for zj in z]
        spm = sp if mask is None else [jnp.where(mask, s, 0.0) for s in sp]
        cs = []
        for s in spm:
            hi = s.astype(BF16)
            lo = (s - hi.astype(F32)).astype(BF16)
            cs.append(jnp.dot(hi, tri, preferred_element_type=F32) + jnp.dot(lo, tri, preferred_element_type=F32))
        car = car_ref[...]
        total = None
        for j, (_, vblk) in enumerate(kv):
            a = jnp.exp(z[j] - sp[j] - cs[j][:, :LANE] - car)
            if mask is not None:
                a = jnp.where(mask, a, 0.0)
            car = car + cs[j][:, LANE:]
            d = jnp.dot(a.astype(BF16), vblk.astype(BF16), preferred_element_type=F32)
            total = d if total is None else total + d
        car_ref[...] = car
        acc_ref[...] += total

    @pl.when(p == 0)
    def _():
        hm = hm_ref[...]
        for i in range(n_new):
            qi = q_ref[i:i + 1, :] * scale
            qbd_ref[i * heads:(i + 1) * heads, :] = jnp.where(hm > 0, qi, 0.0).astype(BF16)
        acc_ref[...] = jnp.zeros_like(acc_ref)
        car_ref[...] = jnp.zeros_like(car_ref)
        qrow = lax.broadcasted_iota(I32, (rows, LANE), 0) // heads
        col = lax.broadcasted_iota(I32, (rows, LANE), 1)
        mask = col < qrow
        pad_ref[...] = jnp.zeros_like(pad_ref)
        pad_ref[0, 0:kn_ref.shape[0], :] = kn_ref[...]
        pad_ref[1, 0:vn_ref.shape[0], :] = vn_ref[...]
        blocks([(pad_ref[0], pad_ref[1])], mask)

    blocks([(kc[...], vc[...]) for kc, vc in zip(kc_refs, vc_refs)], None)

    @pl.when(p == pl.num_programs(1) - 1)
    def _():
        hm = hm_ref[...]
        o_ref[...] = jnp.zeros_like(o_ref)
        for i in range(n_new):
            blk = acc_ref[i * heads:(i + 1) * heads, :] * hm
            o_ref[i:i + 1, :] = jnp.sum(blk, axis=0, keepdims=True)


def _sb_sample_call(page_table, proj_s, cache_k3, cache_v3, tri, hm, bias_col, *, dec_batch, rows_pad, n_new,
                    heads, head_dim):
    n_pages = page_table.shape[1]
    width = heads * head_dim
    page = cache_k3.shape[1]
    rows = n_new * heads
    pps = max(c for c in (8, 4, 2, 1) if n_pages % c == 0)

    def page_spec(j):
        return pl.BlockSpec((None, page, width),
                            lambda b, p, pt: (pt[b * n_pages + n_pages - 1 - (p * pps + j)], 0, 0))

    grid_spec = pltpu.PrefetchScalarGridSpec(
        num_scalar_prefetch=1,
        grid=(dec_batch, n_pages // pps),
        in_specs=[
            pl.BlockSpec((rows_pad, width), lambda b, p, pt: (b, 0)),
            pl.BlockSpec((rows_pad, width), lambda b, p, pt: (b, 1)),
            pl.BlockSpec((rows_pad, width), lambda b, p, pt: (b, 2)),
            *[page_spec(j) for j in range(pps)],
            *[page_spec(j) for j in range(pps)],
            pl.BlockSpec((LANE, 2 * LANE), lambda b, p, pt: (0, 0)),
            pl.BlockSpec((heads, width), lambda b, p, pt: (0, 0)),
            pl.BlockSpec((rows, 1), lambda b, p, pt: (0, 0)),
        ],
        out_specs=pl.BlockSpec((rows_pad, width), lambda b, p, pt: (b, 0)),
        scratch_shapes=[
            pltpu.VMEM((rows, width), BF16),
            pltpu.VMEM((2, page, width), F32),
            pltpu.VMEM((rows, width), F32),
            pltpu.VMEM((rows, LANE), F32),
        ],
    )
    return pl.pallas_call(
        functools.partial(_sb_sample_kernel, scale=head_dim ** -0.5, n_new=n_new, pages_per_step=pps),
        grid_spec=grid_spec,
        out_shape=jax.ShapeDtypeStruct((dec_batch * rows_pad, width), F32),
        compiler_params=_cparams(("arbitrary", "arbitrary"), VMEM_LIMIT),
    )(page_table.reshape(-1), proj_s, proj_s, proj_s, *([cache_k3] * pps), *([cache_v3] * pps), tri, hm, bias_col)


def _ret_kernel(q_ref, k_ref, v_ref, gb_ref, st0_ref, dec_ref, qd_ref, kd_ref, sd_ref, rg_ref,
                ob_ref, st_ref, state_ref, qp_ref, kp_ref, vp_ref, *, use_init):
    c = pl.program_id(2)
    rows = q_ref.shape[0]
    chunk = dec_ref.shape[0]

    @pl.when(c == 0)
    def _():
        if use_init:
            state_ref[...] = st0_ref[...]
        else:
            state_ref[...] = jnp.zeros_like(state_ref)

    if rows == chunk:
        q, k, v = q_ref[...], k_ref[...], v_ref[...]
    else:
        qp_ref[...] = jnp.zeros_like(qp_ref)
        kp_ref[...] = jnp.zeros_like(kp_ref)
        vp_ref[...] = jnp.zeros_like(vp_ref)
        qp_ref[0:rows, :] = q_ref[...]
        kp_ref[0:rows, :] = k_ref[...]
        vp_ref[0:rows, :] = v_ref[...]
        q, k, v = qp_ref[...], kp_ref[...], vp_ref[...]

    qb = q.astype(BF16)
    vb = v.astype(BF16)
    scores = lax.dot_general(qb, k.astype(BF16), NT_DIMS, preferred_element_type=F32) * dec_ref[...]
    inner = jnp.dot(scores.astype(BF16), vb, preferred_element_type=F32)
    st = state_ref[...]
    cross = jnp.dot(qb, st.astype(BF16), preferred_element_type=F32) * qd_ref[...]
    o = (inner + cross)[0:rows, :]
    kd = (k * kd_ref[...]).astype(BF16)
    new = sd_ref[...] * st + lax.dot_general(kd, vb, TN_DIMS, preferred_element_type=F32)
    state_ref[...] = new

    ms = jnp.mean(o * o, axis=-1, keepdims=True)
    gb = gb_ref[...]
    ob_ref[...] = (o * lax.rsqrt(ms + RMS_EPS) * rg_ref[...]) * (gb * jax.nn.sigmoid(gb))

    @pl.when(c == pl.num_programs(2) - 1)
    def _():
        st_ref[...] = new


def _ret_call(proj, state0, dec, qd, kd, sd, rg, *, batch, n_chunks, rows, heads, dk, dv, col0):
    chunk = dec.shape[1]
    qc = col0 // dk
    kc = qc + heads
    vc = (col0 + 2 * heads * dk) // dv
    gc = vc + heads
    use_init = state0 is not None
    if state0 is None:
        state0 = jnp.zeros((1, 1, dk, dv), F32)
        st_map = lambda b, h, c: (0, 0, 0, 0)
    else:
        st_map = lambda b, h, c: (b, h, 0, 0)
    pad_rows = chunk if rows != chunk else SUBLANE
    return pl.pallas_call(
        functools.partial(_ret_kernel, use_init=use_init),
        grid=(batch, heads, n_chunks),
        in_specs=[
            pl.BlockSpec((rows, dk), lambda b, h, c: (b * n_chunks + c, qc + h)),
            pl.BlockSpec((rows, dk), lambda b, h, c: (b * n_chunks + c, kc + h)),
            pl.BlockSpec((rows, dv), lambda b, h, c: (b * n_chunks + c, vc + h)),
            pl.BlockSpec((rows, dv), lambda b, h, c: (b * n_chunks + c, gc + h)),
            pl.BlockSpec((None, None, dk, dv), st_map),
            pl.BlockSpec((None, chunk, chunk), lambda b, h, c: (h, 0, 0)),
            pl.BlockSpec((None, chunk, 1), lambda b, h, c: (h, 0, 0)),
            pl.BlockSpec((None, chunk, 1), lambda b, h, c: (h, 0, 0)),
            pl.BlockSpec((None, 1, 1), lambda b, h, c: (h, 0, 0)),
            pl.BlockSpec((None, 1, dv), lambda b, h, c: (h, 0, 0)),
        ],
        out_specs=[
            pl.BlockSpec((rows, dv), lambda b, h, c: (b * n_chunks + c, h)),
            pl.BlockSpec((None, None, dk, dv), lambda b, h, c: (b, h, 0, 0)),
        ],
        out_shape=[
            jax.ShapeDtypeStruct((batch * n_chunks * rows, heads * dv), F32),
            jax.ShapeDtypeStruct((batch, heads, dk, dv), F32),
        ],
        scratch_shapes=[
            pltpu.VMEM((dk, dv), F32),
            pltpu.VMEM((pad_rows, dk), F32),
            pltpu.VMEM((pad_rows, dk), F32),
            pltpu.VMEM((pad_rows, dv), F32),
        ],
        compiler_params=_cparams(("arbitrary", "arbitrary", "arbitrary")),
    )(proj, proj, proj, proj, state0, dec, qd, kd, sd, rg)


def _merge_kernel(x_ref, oa_ref, ob_ref, ga0_ref, ga1_ref, gb0_ref, gb1_ref, wa_ref, wb_ref, wo_ref, gf_ref,
                  wq_ref, x1_ref, h2_ref, qp_ref):
    ya = jnp.dot(oa_ref[...].astype(BF16), wa_ref[...], preferred_element_type=F32)
    yb = jnp.dot(ob_ref[...].astype(BF16), wb_ref[...], preferred_element_type=F32)
    ga = jnp.concatenate([ga0_ref[...], ga1_ref[...]], axis=1)
    gb = jnp.concatenate([gb0_ref[...], gb1_ref[...]], axis=1)
    merged = jax.nn.sigmoid(ga) * ya + jax.nn.sigmoid(gb) * yb
    x1 = x_ref[...] + jnp.dot(merged.astype(BF16), wo_ref[...], preferred_element_type=F32)
    x1_ref[...] = x1
    ms = jnp.mean(x1 * x1, axis=-1, keepdims=True)
    h2 = x1 * lax.rsqrt(ms + RMS_EPS) * gf_ref[...]
    h2_ref[...] = h2
    qp_ref[...] = jnp.dot(h2.astype(BF16), wq_ref[...], preferred_element_type=F32)


def _merge_call(x2d, oa, ob, proj, wa, wb, wo, gf, wq, *, tm, gate_col0):
    n, d = x2d.shape
    half = d // 2
    g0 = gate_col0 // half
    row = lambda i: (i, 0)
    const = lambda i: (0, 0)
    return pl.pallas_call(
        _merge_kernel,
        grid=(n // tm,),
        in_specs=[
            pl.BlockSpec((tm, d), row),
            pl.BlockSpec((tm, oa.shape[1]), row),
            pl.BlockSpec((tm, ob.shape[1]), row),
            pl.BlockSpec((tm, half), lambda i: (i, g0)),
            pl.BlockSpec((tm, half), lambda i: (i, g0 + 1)),
            pl.BlockSpec((tm, half), lambda i: (i, g0 + 2)),
            pl.BlockSpec((tm, half), lambda i: (i, g0 + 3)),
            pl.BlockSpec(wa.shape, const),
            pl.BlockSpec(wb.shape, const),
            pl.BlockSpec(wo.shape, const),
            pl.BlockSpec((1, d), const),
            pl.BlockSpec(wq.shape, const),
        ],
        out_specs=[
            pl.BlockSpec((tm, d), row),
            pl.BlockSpec((tm, d), row),
            pl.BlockSpec((tm, wq.shape[1]), row),
        ],
        out_shape=[
            jax.ShapeDtypeStruct((n, d), F32),
            jax.ShapeDtypeStruct((n, d), F32),
            jax.ShapeDtypeStruct((n, wq.shape[1]), F32),
        ],
        compiler_params=_cparams(("arbitrary",), VMEM_LIMIT),
    )(x2d, oa, ob, proj, proj, proj, proj, wa, wb, wo, gf, wq)


def _topk_rows(s, codes, k):
    t = s.shape[1]
    out_row = lax.broadcasted_iota(I32, (k, t), 0)
    vals = jnp.zeros((k, t), F32)
    picked = jnp.zeros((k, t), F32)
    for r in range(k):
        m = jnp.max(s, axis=0, keepdims=True)
        am = jnp.min(jnp.where(s == m, codes, jnp.inf), axis=0, keepdims=True)
        vals = jnp.where(out_row == r, m, vals)
        picked = jnp.where(out_row == r, am, picked)
        s = jnp.where(codes == am, -jnp.inf, s)
    return vals, picked


def _select_rows(table, sel):
    out = jnp.zeros(sel.shape, table.dtype)
    for a in range(table.shape[0]):
        out = jnp.where(sel == a, table[a:a + 1, :], out)
    return out


def _pair_candidates(s1, s2):
    k, t = s1.shape
    blk = SUBLANE
    a_lo = lax.broadcasted_iota(I32, (blk, t), 0)
    vals, codes = [], []
    for a0 in range(0, k, blk):
        vals.append(s1[a0:a0 + blk, :] + s2[0:1, :])
        codes.append((a_lo + a0) * k)
    for b in range(1, blk):
        ok = a_lo < k // (b + 1)
        vals.append(jnp.where(ok, s1[0:blk, :] + s2[b:b + 1, :], -jnp.inf))
        codes.append(a_lo * k + b)
    for b0 in range(blk, k, blk):
        vals.append(s1[0:1, :] + s2[b0:b0 + blk, :])
        codes.append(a_lo + b0)
    return jnp.concatenate(vals, axis=0), jnp.concatenate(codes, axis=0).astype(F32)


def _peer_topk_kernel(qp_ref, keys_ref, e_ref, g_ref, *, n_keys):
    heads = keys_ref.shape[0]
    half = keys_ref.shape[3]
    k = PEER_TOPK
    t = qp_ref.shape[0]
    key_codes = lax.broadcasted_iota(I32, (n_keys, t), 0).astype(F32)
    e_rows, g_rows = [], []
    for h in range(heads):
        tops = []
        for c in range(2):
            col = (h * 2 + c) * half
            q = qp_ref[:, col:col + half].astype(BF16)
            st = lax.dot_general(keys_ref[h, c].astype(BF16), q, NT_DIMS, preferred_element_type=F32)
            tops.append(_topk_rows(st, key_codes, k))
        (s1, i1), (s2, i2) = tops
        cand, codes = _pair_candidates(s1, s2)
        top, flat = _topk_rows(cand, codes, k)
        flat = flat.astype(I32)
        ia = _select_rows(i1, flat // k)
        ib = _select_rows(i2, flat % k)
        e_rows.append(((ia * n_keys + ib) * EXPERT_ROWS).astype(I32))
        ex = jnp.exp(top - jnp.max(top, axis=0, keepdims=True))
        g_rows.append(ex / jnp.sum(ex, axis=0, keepdims=True))
    e_ref[...] = jnp.transpose(jnp.concatenate(e_rows, axis=0))
    g_ref[...] = jnp.transpose(jnp.concatenate(g_rows, axis=0))


def _peer_topk_call(qp, keys, *, tm):
    n = qp.shape[0]
    heads, _, n_keys, half = keys.shape
    slots = heads * PEER_TOPK
    return pl.pallas_call(
        functools.partial(_peer_topk_kernel, n_keys=n_keys),
        grid=(n // tm,),
        in_specs=[
            pl.BlockSpec((tm, qp.shape[1]), lambda i: (i, 0)),
            pl.BlockSpec(keys.shape, lambda i: (0, 0, 0, 0)),
        ],
        out_specs=[
            pl.BlockSpec((tm, slots), lambda i: (i, 0)),
            pl.BlockSpec((tm, slots), lambda i: (i, 0)),
        ],
        out_shape=[
            jax.ShapeDtypeStruct((n, slots), I32),
            jax.ShapeDtypeStruct((n, slots), F32),
        ],
        compiler_params=_cparams(("arbitrary",), VMEM_LIMIT),
    )(qp, keys)


EXPERT_ROWS = 4


def _pack_table(tab):
    n, d = tab.shape
    assert d == 2 * EXPERT_ROWS * LANE
    bits = lax.bitcast_convert_type(tab.astype(BF16), jnp.uint16).astype(U32)
    packed = bits[:, :d // 2] | (bits[:, d // 2:] << 16)
    return packed.reshape(n * EXPERT_ROWS, LANE)


PACKED_ROWS = 2 * EXPERT_ROWS


def _gather_rows(tab_ref, idx_ref, t):
    rows = []
    for r in range(idx_ref.shape[1]):
        off = pl.multiple_of(idx_ref[t, r], EXPERT_ROWS)
        rows.append(tab_ref[pl.ds(off, EXPERT_ROWS), :])
    return jnp.concatenate(rows, axis=0)


def _pack_bf16_pair(lo, hi):
    lo_bits = lax.bitcast_convert_type(lo.astype(BF16).astype(F32), U32) >> 16
    hi_bits = lax.bitcast_convert_type(hi.astype(BF16).astype(F32), U32) & jnp.uint32(0xFFFF0000)
    return lo_bits | hi_bits


def _peer_u_kernel(idx_ref, h_ref, g_ref, tab_ref, fold_ref, w_ref):
    tb, slots = idx_ref.shape
    cols = slots * PACKED_ROWS
    sub = lax.broadcasted_iota(I32, (SUBLANE, cols), 0)
    lane = lax.broadcasted_iota(I32, (SUBLANE, cols), 1)
    diag = (lane & (PACKED_ROWS - 1)) == sub
    fold = fold_ref[...]

    def group(t8, carry):
        v8 = jnp.zeros((SUBLANE, cols), F32)
        for s in range(SUBLANE):
            t = t8 * SUBLANE + s
            words = _gather_rows(tab_ref, idx_ref, t)
            base = pl.multiple_of(t * SUBLANE, SUBLANE)
            hw = _pack_bf16_pair(h_ref[pl.ds(base, EXPERT_ROWS), :], h_ref[pl.ds(base + EXPERT_ROWS, EXPERT_ROWS), :])
            m = lax.dot_general(pltpu.bitcast(hw, BF16), pltpu.bitcast(words, BF16), NT_DIMS,
                                preferred_element_type=F32)
            v = jnp.sum(jnp.where(diag, m, 0.0), axis=0, keepdims=True)
            v8 = jnp.where(sub == s, v, v8)
        hi = v8.astype(BF16)
        lo = (v8 - hi.astype(F32)).astype(BF16)
        act = jnp.dot(hi, fold, preferred_element_type=F32) + jnp.dot(lo, fold, preferred_element_type=F32)
        rows = pl.ds(pl.multiple_of(t8 * SUBLANE, SUBLANE), SUBLANE)
        w_ref[rows, :] = g_ref[rows, :] * jax.nn.gelu(act)
        return carry

    lax.fori_loop(0, tb // SUBLANE, group, 0)


def _peer_u_call(idx, h2, g, tab, *, tb):
    n, slots = idx.shape
    d = h2.shape[1]
    rows_per_tok = d // LANE
    h3 = h2.reshape(n * rows_per_tok, LANE)
    fold = jnp.repeat(jnp.eye(slots, dtype=BF16), PACKED_ROWS, axis=0)
    return pl.pallas_call(
        _peer_u_kernel,
        grid=(n // tb,),
        in_specs=[
            pl.BlockSpec((tb, slots), lambda i: (i, 0), memory_space=pltpu.SMEM),
            pl.BlockSpec((tb * rows_per_tok, LANE), lambda i: (i, 0)),
            pl.BlockSpec((tb, slots), lambda i: (i, 0)),
            pl.BlockSpec(tab.shape, lambda i: (0, 0), pipeline_mode=pl.Buffered(1)),
            pl.BlockSpec(fold.shape, lambda i: (0, 0)),
        ],
        out_specs=pl.BlockSpec((tb, slots), lambda i: (i, 0)),
        out_shape=jax.ShapeDtypeStruct((n, slots), F32),
        compiler_params=_cparams(("arbitrary",), VMEM_LIMIT),
    )(idx, h3, g, tab, fold)


def _peer_v_kernel(idx_ref, w_ref, x_ref, tab_ref, expand_ref, o_ref):
    tb, slots = idx_ref.shape
    cols = slots * PACKED_ROWS
    sub = lax.broadcasted_iota(I32, (SUBLANE, cols), 0)
    j = lax.broadcasted_iota(I32, (SUBLANE, cols), 1) & (PACKED_ROWS - 1)
    probe = pltpu.bitcast(jnp.full((EXPERT_ROWS, LANE), 0x40003F80, U32), BF16).astype(F32)
    low_first = probe[0:1, 0:1] == 1.0
    half = jnp.where(low_first, j & 1, 1 - (j & 1))
    place = (half * EXPERT_ROWS + (j >> 1)) == sub
    expand = expand_ref[...]

    def group(t8, carry):
        rows8 = pl.ds(pl.multiple_of(t8 * SUBLANE, SUBLANE), SUBLANE)
        wrep = jnp.dot(w_ref[rows8, :].astype(BF16), expand, preferred_element_type=F32)
        for s in range(SUBLANE):
            t = t8 * SUBLANE + s
            words = _gather_rows(tab_ref, idx_ref, t)
            lhs = jnp.where(place, wrep[s:s + 1, :], 0.0).astype(BF16)
            out = jnp.dot(lhs, pltpu.bitcast(words, BF16), preferred_element_type=F32)
            rows = pl.ds(pl.multiple_of(t * SUBLANE, SUBLANE), SUBLANE)
            o_ref[rows, :] = x_ref[rows, :] + out
        return carry

    lax.fori_loop(0, tb // SUBLANE, group, 0)


def _peer_v_call(idx, w, x1, tab, *, tb):
    n, slots = idx.shape
    d = x1.shape[1]
    rows_per_tok = d // LANE
    x3 = x1.reshape(n * rows_per_tok, LANE)
    expand = jnp.repeat(jnp.eye(slots, dtype=BF16), PACKED_ROWS, axis=1)
    out = pl.pallas_call(
        _peer_v_kernel,
        grid=(n // tb,),
        in_specs=[
            pl.BlockSpec((tb, slots), lambda i: (i, 0), memory_space=pltpu.SMEM),
            pl.BlockSpec((tb, slots), lambda i: (i, 0)),
            pl.BlockSpec((tb * rows_per_tok, LANE), lambda i: (i, 0)),
            pl.BlockSpec(tab.shape, lambda i: (0, 0), pipeline_mode=pl.Buffered(1)),
            pl.BlockSpec(expand.shape, lambda i: (0, 0)),
        ],
        out_specs=pl.BlockSpec((tb * rows_per_tok, LANE), lambda i: (i, 0)),
        out_shape=jax.ShapeDtypeStruct((n * rows_per_tok, LANE), F32),
        compiler_params=_cparams(("arbitrary",), VMEM_LIMIT),
    )(idx, w, x3, tab, expand)
    return out.reshape(n, d)


def _ple_kernel(x_ref, p_ref, g_ref, wg_ref, wp_ref, o_ref):
    x = x_ref[...]
    ms = jnp.mean(x * x, axis=-1, keepdims=True)
    hn = (x * lax.rsqrt(ms + RMS_EPS) * g_ref[...]).astype(BF16)
    gate = jax.nn.sigmoid(jnp.dot(hn, wg_ref[...], preferred_element_type=F32))
    pe = jnp.dot(p_ref[...].astype(BF16), wp_ref[...], preferred_element_type=F32)
    o_ref[...] = x + gate * pe


def _ple_call(x2, p, g, wg, wp, *, tm):
    n, d = x2.shape
    return pl.pallas_call(
        _ple_kernel,
        grid=(n // tm,),
        in_specs=[
            pl.BlockSpec((tm, d), lambda i: (i, 0)),
            pl.BlockSpec((tm, p.shape[1]), lambda i: (i, 0)),
            pl.BlockSpec((1, d), lambda i: (0, 0)),
            pl.BlockSpec(wg.shape, lambda i: (0, 0)),
            pl.BlockSpec(wp.shape, lambda i: (0, 0)),
        ],
        out_specs=pl.BlockSpec((tm, d), lambda i: (i, 0)),
        out_shape=jax.ShapeDtypeStruct((n, d), F32),
        compiler_params=_cparams(("arbitrary",), VMEM_LIMIT),
    )(x2, p, g, wg, wp)


def _rope_tables(pos, dim):
    half = dim // 2
    inv = ROPE_BASE ** (-jnp.arange(half, dtype=F32) / half)
    ang = pos.astype(F32)[:, None] * inv[None, :]
    cos, sin = jnp.cos(ang), jnp.sin(ang)
    return jnp.concatenate([cos, cos], axis=1), jnp.concatenate([-sin, sin], axis=1)


def _ret_tables(heads, chunk, n_valid):
    lg = jnp.log1p(-jnp.exp2(-5.0 - jnp.arange(heads, dtype=F32)))
    idx = jnp.arange(chunk, dtype=F32)
    diff = idx[:, None] - idx[None, :]
    causal = diff >= 0
    dec = jnp.where(causal[None], jnp.exp(jnp.where(causal, diff, 0.0)[None] * lg[:, None, None]), 0.0)
    qd = jnp.exp((idx[None, :] + 1.0) * lg[:, None])
    kd = jnp.exp((n_valid - 1.0 - idx)[None, :] * lg[:, None])
    kd = jnp.where(idx[None, :] < n_valid, kd, 0.0)
    sd = jnp.exp(n_valid * lg)
    return dec, qd[:, :, None], kd[:, :, None], sd[:, None, None]


def _token_tail(x2d, oa, ob, proj, p2d, lw, *, tm, tb, gate_col0):
    x1, h2, qp = _merge_call(x2d, oa, ob, proj, lw["wa"], lw["wb"], lw["wo"], lw["gf"], lw["wq"],
                             tm=tm, gate_col0=gate_col0)
    e, g = _peer_topk_call(qp, lw["keys"], tm=tm)
    w = _peer_u_call(e, h2, g, lw["u_tab"], tb=tb)
    x2 = _peer_v_call(e, w, x1, lw["v_tab"], tb=tb)
    return _ple_call(x2, p2d, lw["gp"], lw["wg"], lw["wp"], tm=tm)


def kernel(x_prompt, x_sample, cache_k, cache_v, state_ret, page_table, p_prompt, p_sample, norm_mix_g, w_in,
           q_norm_g, k_norm_g, sb_bias, ret_norm_g, w_a_up, w_b_up, w_out, norm_ffn_g, peer_w_q, peer_sub_keys,
           peer_u, peer_v, norm_ple_g, w_ple_gate, w_ple):
    batch, seq, d = x_prompt.shape
    dec_batch, dec_seq, _ = x_sample.shape
    depth, n_phys, page, sb_heads, sb_dim = cache_k.shape
    _, _, ret_heads, dk, dv = state_ret.shape
    sb_width = sb_heads * sb_dim
    past_len = page_table.shape[1] * page
    chunk = LANE
    rows_pad = SUBLANE
    assert dec_seq <= rows_pad and dk == LANE and sb_width % LANE == 0
    qb_col0 = 3 * sb_width
    gate_col0 = qb_col0 + 2 * ret_heads * dk + 2 * ret_heads * dv

    xp = x_prompt.reshape(batch * seq, d)
    xs = jnp.pad(x_sample, ((0, 0), (0, rows_pad - dec_seq), (0, 0))).reshape(dec_batch * rows_pad, d)
    ns = dec_batch * rows_pad

    cs_p, sn_p = _rope_tables(jnp.arange(seq), dk)
    cs_s, sn_s = _rope_tables(past_len + jnp.arange(rows_pad), dk)
    cs_s, sn_s = jnp.tile(cs_s, (dec_batch, 1)), jnp.tile(sn_s, (dec_batch, 1))
    gm = jnp.kron(jnp.eye(sb_heads, dtype=F32), jnp.full((sb_dim, sb_dim), 1.0 / sb_dim, F32))
    tri = jnp.concatenate([jnp.tril(jnp.ones((LANE, LANE), F32), -1), jnp.ones((LANE, LANE), F32)], axis=1).astype(BF16)
    head_mask = jnp.kron(jnp.eye(sb_heads, dtype=F32), jnp.ones((1, sb_dim), F32))
    dec_p = _ret_tables(ret_heads, chunk, float(chunk))
    dec_s = _ret_tables(ret_heads, chunk, float(dec_seq))

    tm = 256
    tb = 256
    kp, vp, sp, kn, vn, sn_out = [], [], [], [], [], []
    for i in range(depth):
        lw = dict(
            wa=w_a_up[i].astype(BF16), wb=w_b_up[i].astype(BF16), wo=w_out[i].astype(BF16),
            gf=norm_ffn_g[i][None, :], wq=peer_w_q[i].astype(BF16), keys=peer_sub_keys[i],
            u_tab=_pack_table(peer_u[i]), v_tab=_pack_table(peer_v[i]),
            gp=norm_ple_g[i][None, :], wg=w_ple_gate[i].astype(BF16), wp=w_ple[i].astype(BF16),
        )
        w_bf = w_in[i].astype(BF16)
        g_mix = norm_mix_g[i][None, :]
        qg = jnp.tile(q_norm_g[i], sb_heads)[None, :]
        kg = jnp.tile(k_norm_g[i], sb_heads)[None, :]
        rg = ret_norm_g[i][:, None, :]
        kb_scale = dk ** -0.5

        tm_proj = 512 if seq % 512 == 0 else tm
        proj = _proj_call(xp, g_mix, w_bf, cs_p, sn_p, qg, kg, gm, tm=tm_proj, pos_blocks=seq // tm_proj,
                          kb_scale=kb_scale)
        oa = _sb_prompt_call(proj, sb_bias[i], batch=batch, seq=seq, heads=sb_heads, head_dim=sb_dim)
        ob, st_p = _ret_call(proj, None, *dec_p, rg, batch=batch, n_chunks=seq // chunk, rows=chunk,
                             heads=ret_heads, dk=dk, dv=dv, col0=qb_col0)
        xp = _token_tail(xp, oa, ob, proj, p_prompt[i].reshape(batch * seq, -1), lw, tm=tm, tb=tb,
                         gate_col0=gate_col0)
        kp.append(proj[:, sb_width:2 * sb_width].reshape(batch, seq, sb_heads, sb_dim))
        vp.append(proj[:, 2 * sb_width:3 * sb_width].reshape(batch, seq, sb_heads, sb_dim))
        sp.append(st_p)

        proj_s = _proj_call(xs, g_mix, w_bf, cs_s, sn_s, qg, kg, gm, tm=ns, pos_blocks=1, kb_scale=kb_scale)
        bias_col = jnp.tile(sb_bias[i], dec_seq)[:, None]
        oa_s = _sb_sample_call(page_table, proj_s, cache_k[i].reshape(n_phys, page, sb_width),
                               cache_v[i].reshape(n_phys, page, sb_width), tri, head_mask, bias_col,
                               dec_batch=dec_batch, rows_pad=rows_pad, n_new=dec_seq, heads=sb_heads,
                               head_dim=sb_dim)
        ob_s, st_s = _ret_call(proj_s, state_ret[i], *dec_s, rg, batch=dec_batch, n_chunks=1, rows=rows_pad,
                               heads=ret_heads, dk=dk, dv=dv, col0=qb_col0)
        p_s = jnp.pad(p_sample[i], ((0, 0), (0, rows_pad - dec_seq), (0, 0))).reshape(ns, -1)
        xs = _token_tail(xs, oa_s, ob_s, proj_s, p_s, lw, tm=ns, tb=ns, gate_col0=gate_col0)
        ka_s = proj_s[:, sb_width:2 * sb_width].reshape(dec_batch, rows_pad, sb_heads, sb_dim)[:, :dec_seq]
        va_s = proj_s[:, 2 * sb_width:3 * sb_width].reshape(dec_batch, rows_pad, sb_heads, sb_dim)[:, :dec_seq]
        kn.append(ka_s)
        vn.append(va_s)
        sn_out.append(st_s)

    y_prompt = xp.reshape(batch, seq, d)
    y_sample = xs.reshape(dec_batch, rows_pad, d)[:, :dec_seq]
    return (y_prompt, y_sample, jnp.stack(kp), jnp.stack(vp), jnp.stack(sp), jnp.stack(kn), jnp.stack(vn),
            jnp.stack(sn_out))
```

```python
import functools

import numpy as np
import jax
import jax.numpy as jnp
from jax import lax
from jax.experimental import pallas as pl
from jax.experimental.pallas import tpu as pltpu

F32 = jnp.float32
BF16 = jnp.bfloat16
I32 = jnp.int32
U32 = jnp.uint32
HIGHEST = lax.Precision.HIGHEST

RMS_EPS = 1e-6
ROPE_BASE = 10000.0
PEER_TOPK = 16
LANE = 128
SUBLANE = 8
VMEM_LIMIT = 56 * 1024 * 1024
NT_DIMS = (((1,), (1,)), ((), ()))
TN_DIMS = (((0,), (0,)), ((), ()))


def _cparams(sem, vmem=None):
    return pltpu.CompilerParams(dimension_semantics=sem, vmem_limit_bytes=vmem)


PROJ_TN = 512


def _proj_kernel(x_ref, g_ref, w_ref, cs_ref, sn_ref, qg_ref, kg_ref, gm_ref, o_ref, *, kb_scale):
    x = x_ref[...]
    ms = jnp.mean(x * x, axis=-1, keepdims=True)
    hn = (x * lax.rsqrt(ms + RMS_EPS) * g_ref[...]).astype(BF16)
    gm = gm_ref[...]
    for j in range(o_ref.shape[1] // PROJ_TN):
        cols = slice(j * PROJ_TN, (j + 1) * PROJ_TN)
        acc = jnp.dot(hn, w_ref[:, cols], preferred_element_type=F32)
        if j <= 1:
            sq = acc * acc
            hi = sq.astype(BF16)
            lo = (sq - hi.astype(F32)).astype(BF16)
            msq = jnp.dot(hi, gm, preferred_element_type=F32) + jnp.dot(lo, gm, preferred_element_type=F32)
            gain = qg_ref[...] if j == 0 else kg_ref[...]
            o_ref[:, cols] = acc * lax.rsqrt(msq + RMS_EPS) * gain
        elif j in (3, 4):
            cs = cs_ref[...]
            sn = sn_ref[...]
            for h in range(PROJ_TN // LANE):
                xh = acc[:, h * LANE:(h + 1) * LANE]
                r = pltpu.roll(xh, LANE // 2, 1)
                y = xh * cs + r * sn
                lo_col = j * PROJ_TN + h * LANE
                o_ref[:, lo_col:lo_col + LANE] = y * kb_scale if j == 4 else y
        else:
            o_ref[:, cols] = acc


def _proj_call(x2d, g, w_bf, cs, sn, qg, kg, gm, *, tm, pos_blocks, kb_scale):
    n, d = x2d.shape
    pw = w_bf.shape[1]
    const = lambda i: (0, 0)
    return pl.pallas_call(
        functools.partial(_proj_kernel, kb_scale=kb_scale),
        grid=(n // tm,),
        in_specs=[
            pl.BlockSpec((tm, d), lambda i: (i, 0)),
            pl.BlockSpec((1, d), const),
            pl.BlockSpec((d, pw), const, pipeline_mode=pl.Buffered(1)),
            pl.BlockSpec((tm, LANE), lambda i: (i % pos_blocks, 0)),
            pl.BlockSpec((tm, LANE), lambda i: (i % pos_blocks, 0)),
            pl.BlockSpec((1, PROJ_TN), const),
            pl.BlockSpec((1, PROJ_TN), const),
            pl.BlockSpec((PROJ_TN, PROJ_TN), const),
        ],
        out_specs=pl.BlockSpec((tm, pw), lambda i: (i, 0)),
        out_shape=jax.ShapeDtypeStruct((n, pw), F32),
        compiler_params=_cparams(("arbitrary",), VMEM_LIMIT),
    )(x2d, g, w_bf, cs, sn, qg, kg, gm)


def _sb_prompt_kernel(bias_ref, q_ref, k_ref, v_ref, tri_ref, o_ref, qs_ref, acc_ref, car_ref, *, scale):
    i = pl.program_id(1)
    tq = q_ref.shape[0]
    pairs = qs_ref.shape[0]
    lane = lax.broadcasted_iota(I32, (1, LANE), 1)
    hmask = [lane < LANE // 2, lane >= LANE // 2]
    first = lax.broadcasted_iota(I32, (1, 2 * LANE), 1) < LANE
    for t in range(pairs):
        qs_ref[t] = (q_ref[:, t * LANE:(t + 1) * LANE] * scale).astype(BF16)
    tri2 = tri_ref[...]
    acc_ref[...] = jnp.zeros_like(acc_ref)
    car_ref[...] = jnp.zeros_like(car_ref)
    row = lax.broadcasted_iota(I32, (tq, 2 * LANE), 0)
    col = lax.broadcasted_iota(I32, (tq, 2 * LANE), 1) & (LANE - 1)
    causal2 = col < row

    def split_heads(x):
        return jnp.concatenate([jnp.where(hmask[0], x, 0.0), jnp.where(hmask[1], x, 0.0)], axis=0).astype(BF16)

    def block(kb, mask2):
        start = pl.multiple_of(kb * LANE, LANE)
        tiles = [slice(t * LANE, (t + 1) * LANE) for t in range(pairs)]
        z = []
        for t in range(pairs):
            rk = split_heads(k_ref[pl.ds(start, LANE), tiles[t]])
            bias2 = jnp.where(first, bias_ref[2 * t], bias_ref[2 * t + 1])
            z.append(lax.dot_general(qs_ref[t], rk, NT_DIMS, preferred_element_type=F32) + bias2)
        sp = [jnp.maximum(zt, 0.0) + jnp.log(1.0 + jnp.exp(-jnp.abs(zt))) for zt in z]
        spm = sp if mask2 is None else [jnp.where(mask2, s, 0.0) for s in sp]
        cs = []
        for t in range(pairs):
            hi = spm[t].astype(BF16)
            lo = (spm[t] - hi.astype(F32)).astype(BF16)
            cs.append(jnp.dot(hi, tri2, preferred_element_type=F32) + jnp.dot(lo, tri2, preferred_element_type=F32))
        a = []
        for t in range(pairs):
            at = jnp.exp(z[t] - sp[t] - cs[t] - car_ref[t])
            a.append(at if mask2 is None else jnp.where(mask2, at, 0.0))
            tot = [jnp.broadcast_to(jnp.sum(spm[t][:, h * LANE:(h + 1) * LANE], axis=1, keepdims=True), (tq, LANE))
                   for h in range(2)]
            car_ref[t] += jnp.concatenate(tot, axis=1)
        for t in range(pairs):
            rv = split_heads(v_ref[pl.ds(start, LANE), tiles[t]])
            acc_ref[:, tiles[t]] += jnp.dot(a[t].astype(BF16), rv, preferred_element_type=F32)

    block(i, causal2)

    def body(jj, c):
        block(i - 1 - jj, None)
        return c

    lax.fori_loop(0, i, body, 0)
    o_ref[...] = acc_ref[...]


def _sb_prompt_call(proj, bias, *, batch, seq, heads, head_dim):
    assert 2 * head_dim == LANE
    tq = LANE
    nq = seq // tq
    width = heads * head_dim
    pairs = heads // 2
    tri2 = jnp.kron(jnp.eye(2, dtype=F32), jnp.tril(jnp.ones((LANE, LANE), F32), -1)).astype(BF16)
    return pl.pallas_call(
        functools.partial(_sb_prompt_kernel, scale=head_dim ** -0.5),
        grid=(batch, nq),
        in_specs=[
            pl.BlockSpec(memory_space=pltpu.SMEM),
            pl.BlockSpec((tq, width), lambda b, i: (b * nq + i, 0)),
            pl.BlockSpec((seq, width), lambda b, i: (b, 1)),
            pl.BlockSpec((seq, width), lambda b, i: (b, 2)),
            pl.BlockSpec((2 * LANE, 2 * LANE), lambda b, i: (0, 0)),
        ],
        out_specs=pl.BlockSpec((tq, width), lambda b, i: (b * nq + i, 0)),
        out_shape=jax.ShapeDtypeStruct((batch * seq, width), F32),
        scratch_shapes=[
            pltpu.VMEM((pairs, tq, LANE), BF16),
            pltpu.VMEM((tq, width), F32),
            pltpu.VMEM((pairs, tq, 2 * LANE), F32),
        ],
        compiler_params=_cparams(("arbitrary", "arbitrary"), VMEM_LIMIT),
    )(bias, proj, proj, proj, tri2)


def _sb_sample_kernel(pt_ref, q_ref, kn_ref, vn_ref, *rest, scale, pages_per_step):
    kc_refs = rest[:pages_per_step]
    vc_refs = rest[pages_per_step:2 * pages_per_step]
    tri_ref, bias_ref, o_ref, qh_ref, nk_ref, nv_ref, acc_ref, car_ref = rest[2 * pages_per_step:]
    p = pl.program_id(1)
    heads, rq, hd = qh_ref.shape
    tri = tri_ref[...]
    bias = bias_ref[...]
    nn_dims = (((1,), (0,)), ((), ()))

    def blocks(load_k, load_v, n_blocks, mask, token_major):
        qk_dims, av_dims = (NT_DIMS, nn_dims) if token_major else (nn_dims, NT_DIMS)
        z = []
        for j in range(n_blocks):
            zj = [lax.dot_general(qh_ref[h], load_k(j, h).astype(BF16), qk_dims, preferred_element_type=F32)
                  for h in range(heads)]
            z.append(jnp.concatenate(zj, axis=0) + bias)
        sp = [jnp.maximum(zj, 0.0) + jnp.log(1.0 + jnp.exp(-jnp.abs(zj))) for zj in z]
        spm = sp if mask is None else [jnp.where(mask, s, 0.0) for s in sp]
        cs = []
        for s in spm:
            hi = s.astype(BF16)
            lo = (s - hi.astype(F32)).astype(BF16)
            cs.append(jnp.dot(hi, tri, preferred_element_type=F32) + jnp.dot(lo, tri, preferred_element_type=F32))
        car = car_ref[...]
        totals = [None] * heads
        for j in range(n_blocks):
            a = jnp.exp(z[j] - sp[j] - cs[j][:, :LANE] - car)
            if mask is not None:
                a = jnp.where(mask, a, 0.0)
            car = car + cs[j][:, LANE:]
            for h in range(heads):
                d = lax.dot_general(a[h * rq:(h + 1) * rq, :].astype(BF16), load_v(j, h).astype(BF16), av_dims,
                                    preferred_element_type=F32)
                totals[h] = d if totals[h] is None else totals[h] + d
        car_ref[...] = car
        for h in range(heads):
            acc_ref[h * rq:(h + 1) * rq, :] += totals[h]

    @pl.when(p == 0)
    def _():
        nk_ref[...] = jnp.zeros_like(nk_ref)
        nv_ref[...] = jnp.zeros_like(nv_ref)
        for h in range(heads):
            cols = slice(h * hd, (h + 1) * hd)
            qh_ref[h] = (q_ref[:, cols] * scale).astype(BF16)
            nk_ref[h, 0:rq, :] = kn_ref[:, cols]
            nv_ref[h, 0:rq, :] = vn_ref[:, cols]
        acc_ref[...] = jnp.zeros_like(acc_ref)
        car_ref[...] = jnp.zeros_like(car_ref)
        qrow = lax.broadcasted_iota(I32, (heads * rq, LANE), 0) & (rq - 1)
        col = lax.broadcasted_iota(I32, (heads * rq, LANE), 1)
        blocks(lambda j, h: nk_ref[h], lambda j, h: nv_ref[h], 1, col < qrow, True)

    blocks(lambda j, h: kc_refs[j][h], lambda j, h: vc_refs[j][h], pages_per_step, None, False)

    @pl.when(p == pl.num_programs(1) - 1)
    def _():
        for h in range(heads):
            o_ref[:, h * hd:(h + 1) * hd] = acc_ref[h * rq:(h + 1) * rq, :]


def _sb_sample_call(page_table, proj_s, cache_k3, cache_v3, tri, bias_col, *, dec_batch, rows_pad, heads, head_dim):
    n_pages = page_table.shape[1]
    width = heads * head_dim
    page = cache_k3.shape[3]
    rows = rows_pad * heads
    pps = max(c for c in (8, 4, 2, 1) if n_pages % c == 0)

    def page_spec(j):
        return pl.BlockSpec((None, heads, head_dim, page),
                            lambda b, p, pt: (pt[b * n_pages + n_pages - 1 - (p * pps + j)], 0, 0, 0))

    grid_spec = pltpu.PrefetchScalarGridSpec(
        num_scalar_prefetch=1,
        grid=(dec_batch, n_pages // pps),
        in_specs=[
            pl.BlockSpec((rows_pad, width), lambda b, p, pt: (b, 0)),
            pl.BlockSpec((rows_pad, width), lambda b, p, pt: (b, 1)),
            pl.BlockSpec((rows_pad, width), lambda b, p, pt: (b, 2)),
            *[page_spec(j) for j in range(pps)],
            *[page_spec(j) for j in range(pps)],
            pl.BlockSpec((LANE, 2 * LANE), lambda b, p, pt: (0, 0)),
            pl.BlockSpec((rows, 1), lambda b, p, pt: (0, 0)),
        ],
        out_specs=pl.BlockSpec((rows_pad, width), lambda b, p, pt: (b, 0)),
        scratch_shapes=[
            pltpu.VMEM((heads, rows_pad, head_dim), BF16),
            pltpu.VMEM((heads, page, head_dim), F32),
            pltpu.VMEM((heads, page, head_dim), F32),
            pltpu.VMEM((rows, head_dim), F32),
            pltpu.VMEM((rows, LANE), F32),
        ],
    )
    return pl.pallas_call(
        functools.partial(_sb_sample_kernel, scale=head_dim ** -0.5, pages_per_step=pps),
        grid_spec=grid_spec,
        out_shape=jax.ShapeDtypeStruct((dec_batch * rows_pad, width), F32),
        compiler_params=_cparams(("arbitrary", "arbitrary"), VMEM_LIMIT),
    )(page_table.reshape(-1), proj_s, proj_s, proj_s, *([cache_k3] * pps), *([cache_v3] * pps), tri, bias_col)


def _ret_kernel(q_ref, k_ref, v0_ref, v1_ref, g0_ref, g1_ref, st0_ref, dec_ref, qd_ref, kd_ref, sd_ref, rg_ref,
                ob_ref, st_ref, state_ref, qp_ref, kp_ref, vp_ref, *, use_init):
    c = pl.program_id(1)
    heads, dk, dv = state_ref.shape
    rows = q_ref.shape[0]
    chunk = dec_ref.shape[1]
    half = v0_ref.shape[1]

    @pl.when(c == 0)
    def _():
        if use_init:
            state_ref[...] = st0_ref[...]
        else:
            state_ref[...] = jnp.zeros_like(state_ref)

    if rows == chunk:
        q, k = q_ref[...], k_ref[...]
        v = jnp.concatenate([v0_ref[...], v1_ref[...]], axis=1)
    else:
        qp_ref[...] = jnp.zeros_like(qp_ref)
        kp_ref[...] = jnp.zeros_like(kp_ref)
        vp_ref[...] = jnp.zeros_like(vp_ref)
        qp_ref[0:rows, :] = q_ref[...]
        kp_ref[0:rows, :] = k_ref[...]
        vp_ref[0:rows, 0:half] = v0_ref[...]
        vp_ref[0:rows, half:2 * half] = v1_ref[...]
        q, k, v = qp_ref[...], kp_ref[...], vp_ref[...]
    gb = jnp.concatenate([g0_ref[...], g1_ref[...]], axis=1)

    qb = [q[:, h * dk:(h + 1) * dk].astype(BF16) for h in range(heads)]
    vb = [v[:, h * dv:(h + 1) * dv].astype(BF16) for h in range(heads)]
    st = [state_ref[h] for h in range(heads)]
    scores = [lax.dot_general(qb[h], k[:, h * dk:(h + 1) * dk].astype(BF16), NT_DIMS, preferred_element_type=F32)
              * dec_ref[h] for h in range(heads)]
    cross = [jnp.dot(qb[h], st[h].astype(BF16), preferred_element_type=F32) * qd_ref[h] for h in range(heads)]
    kv = [lax.dot_general((k[:, h * dk:(h + 1) * dk] * kd_ref[h]).astype(BF16), vb[h], TN_DIMS,
                          preferred_element_type=F32) for h in range(heads)]
    inner = [jnp.dot(scores[h].astype(BF16), vb[h], preferred_element_type=F32) for h in range(heads)]
    for h in range(heads):
        state_ref[h] = sd_ref[h] * st[h] + kv[h]
        o = (inner[h] + cross[h])[0:rows, :]
        ms = jnp.mean(o * o, axis=-1, keepdims=True)
        g = gb[:, h * dv:(h + 1) * dv]
        ob_ref[:, h * dv:(h + 1) * dv] = (o * lax.rsqrt(ms + RMS_EPS) * rg_ref[h]) * (g * jax.nn.sigmoid(g))

    @pl.when(c == pl.num_programs(1) - 1)
    def _():
        st_ref[...] = state_ref[...]


def _ret_call(proj, state0, dec, qd, kd, sd, rg, *, batch, n_chunks, rows, heads, dk, dv, col0):
    chunk = dec.shape[1]
    qw = heads * dk
    assert heads * dv == 2 * qw and col0 % qw == 0
    qc = col0 // qw
    use_init = state0 is not None
    if state0 is None:
        state0 = jnp.zeros((1, heads, dk, dv), F32)
        st_map = lambda b, c: (0, 0, 0, 0)
    else:
        st_map = lambda b, c: (b, 0, 0, 0)
    pad_rows = chunk if rows != chunk else SUBLANE
    row = lambda j: (lambda b, c: (b * n_chunks + c, qc + j))
    const = lambda b, c: (0, 0, 0)
    return pl.pallas_call(
        functools.partial(_ret_kernel, use_init=use_init),
        grid=(batch, n_chunks),
        in_specs=[
            *[pl.BlockSpec((rows, qw), row(j)) for j in range(6)],
            pl.BlockSpec((None, heads, dk, dv), st_map),
            pl.BlockSpec((heads, chunk, chunk), const),
            pl.BlockSpec((heads, chunk, 1), const),
            pl.BlockSpec((heads, chunk, 1), const),
            pl.BlockSpec((heads, 1, 1), const),
            pl.BlockSpec((heads, 1, dv), const),
        ],
        out_specs=[
            pl.BlockSpec((rows, heads * dv), lambda b, c: (b * n_chunks + c, 0)),
            pl.BlockSpec((None, heads, dk, dv), lambda b, c: (b, 0, 0, 0)),
        ],
        out_shape=[
            jax.ShapeDtypeStruct((batch * n_chunks * rows, heads * dv), F32),
            jax.ShapeDtypeStruct((batch, heads, dk, dv), F32),
        ],
        scratch_shapes=[
            pltpu.VMEM((heads, dk, dv), F32),
            pltpu.VMEM((pad_rows, qw), F32),
            pltpu.VMEM((pad_rows, qw), F32),
            pltpu.VMEM((pad_rows, 2 * qw), F32),
        ],
        compiler_params=_cparams(("arbitrary", "arbitrary"), VMEM_LIMIT),
    )(proj, proj, proj, proj, proj, proj, state0, dec, qd, kd, sd, rg)


def _merge_kernel(x_ref, oa_ref, ob_ref, ga0_ref, ga1_ref, gb0_ref, gb1_ref, wa_ref, wb_ref, wo_ref, gf_ref,
                  wq_ref, x1_ref, h2_ref, qp_ref):
    ya = jnp.dot(oa_ref[...].astype(BF16), wa_ref[...], preferred_element_type=F32)
    yb = jnp.dot(ob_ref[...].astype(BF16), wb_ref[...], preferred_element_type=F32)
    ga = jnp.concatenate([ga0_ref[...], ga1_ref[...]], axis=1)
    gb = jnp.concatenate([gb0_ref[...], gb1_ref[...]], axis=1)
    merged = jax.nn.sigmoid(ga) * ya + jax.nn.sigmoid(gb) * yb
    x1 = x_ref[...] + jnp.dot(merged.astype(BF16), wo_ref[...], preferred_element_type=F32)
    x1_ref[...] = x1
    ms = jnp.mean(x1 * x1, axis=-1, keepdims=True)
    h2 = x1 * lax.rsqrt(ms + RMS_EPS) * gf_ref[...]
    h2_ref[...] = h2
    qp_ref[...] = jnp.dot(h2.astype(BF16), wq_ref[...], preferred_element_type=F32)


def _merge_call(x2d, oa, ob, proj, wa, wb, wo, gf, wq, *, tm, gate_col0):
    n, d = x2d.shape
    half = d // 2
    g0 = gate_col0 // half
    row = lambda i: (i, 0)
    const = lambda i: (0, 0)
    return pl.pallas_call(
        _merge_kernel,
        grid=(n // tm,),
        in_specs=[
            pl.BlockSpec((tm, d), row),
            pl.BlockSpec((tm, oa.shape[1]), row),
            pl.BlockSpec((tm, ob.shape[1]), row),
            pl.BlockSpec((tm, half), lambda i: (i, g0)),
            pl.BlockSpec((tm, half), lambda i: (i, g0 + 1)),
            pl.BlockSpec((tm, half), lambda i: (i, g0 + 2)),
            pl.BlockSpec((tm, half), lambda i: (i, g0 + 3)),
            pl.BlockSpec(wa.shape, const),
            pl.BlockSpec(wb.shape, const),
            pl.BlockSpec(wo.shape, const),
            pl.BlockSpec((1, d), const),
            pl.BlockSpec(wq.shape, const),
        ],
        out_specs=[
            pl.BlockSpec((tm, d), row),
            pl.BlockSpec((tm, d), row),
            pl.BlockSpec((tm, wq.shape[1]), row),
        ],
        out_shape=[
            jax.ShapeDtypeStruct((n, d), F32),
            jax.ShapeDtypeStruct((n, d), F32),
            jax.ShapeDtypeStruct((n, wq.shape[1]), F32),
        ],
        compiler_params=_cparams(("arbitrary",), VMEM_LIMIT),
    )(x2d, oa, ob, proj, proj, proj, proj, wa, wb, wo, gf, wq)


def _topk_rows(s, codes, k):
    t = s.shape[1]
    out_row = lax.broadcasted_iota(I32, (k, t), 0)
    vals = jnp.zeros((k, t), F32)
    picked = jnp.zeros((k, t), F32)
    for r in range(k):
        m = jnp.max(s, axis=0, keepdims=True)
        am = jnp.min(jnp.where(s == m, codes, jnp.inf), axis=0, keepdims=True)
        vals = jnp.where(out_row == r, m, vals)
        picked = jnp.where(out_row == r, am, picked)
        s = jnp.where(codes == am, -jnp.inf, s)
    return vals, picked


def _select_rows(table, sel):
    out = jnp.zeros(sel.shape, table.dtype)
    for a in range(table.shape[0]):
        out = jnp.where(sel == a, table[a:a + 1, :], out)
    return out


def _pair_candidates(s1, s2):
    k, t = s1.shape
    blk = SUBLANE
    a_lo = lax.broadcasted_iota(I32, (blk, t), 0)
    vals, codes = [], []
    for a0 in range(0, k, blk):
        vals.append(s1[a0:a0 + blk, :] + s2[0:1, :])
        codes.append((a_lo + a0) * k)
    for b in range(1, blk):
        ok = a_lo < k // (b + 1)
        vals.append(jnp.where(ok, s1[0:blk, :] + s2[b:b + 1, :], -jnp.inf))
        codes.append(a_lo * k + b)
    for b0 in range(blk, k, blk):
        vals.append(s1[0:1, :] + s2[b0:b0 + blk, :])
        codes.append(a_lo + b0)
    return jnp.concatenate(vals, axis=0), jnp.concatenate(codes, axis=0).astype(F32)


def _peer_topk_kernel(qp_ref, keys_ref, e_ref, g_ref, *, n_keys):
    heads = keys_ref.shape[0]
    half = keys_ref.shape[3]
    k = PEER_TOPK
    t = qp_ref.shape[0]
    key_codes = lax.broadcasted_iota(I32, (n_keys, t), 0).astype(F32)
    e_rows, g_rows = [], []
    for h in range(heads):
        tops = []
        for c in range(2):
            col = (h * 2 + c) * half
            q = qp_ref[:, col:col + half].astype(BF16)
            st = lax.dot_general(keys_ref[h, c].astype(BF16), q, NT_DIMS, preferred_element_type=F32)
            tops.append(_topk_rows(st, key_codes, k))
        (s1, i1), (s2, i2) = tops
        cand, codes = _pair_candidates(s1, s2)
        top, flat = _topk_rows(cand, codes, k)
        flat = flat.astype(I32)
        ia = _select_rows(i1, flat // k)
        ib = _select_rows(i2, flat % k)
        e_rows.append(((ia * n_keys + ib) * EXPERT_ROWS).astype(I32))
        ex = jnp.exp(top - jnp.max(top, axis=0, keepdims=True))
        g_rows.append(ex / jnp.sum(ex, axis=0, keepdims=True))
    e_ref[...] = jnp.transpose(jnp.concatenate(e_rows, axis=0))
    g_ref[...] = jnp.transpose(jnp.concatenate(g_rows, axis=0))


def _peer_topk_call(qp, keys, *, tm):
    n = qp.shape[0]
    heads, _, n_keys, half = keys.shape
    slots = heads * PEER_TOPK
    return pl.pallas_call(
        functools.partial(_peer_topk_kernel, n_keys=n_keys),
        grid=(n // tm,),
        in_specs=[
            pl.BlockSpec((tm, qp.shape[1]), lambda i: (i, 0)),
            pl.BlockSpec(keys.shape, lambda i: (0, 0, 0, 0)),
        ],
        out_specs=[
            pl.BlockSpec((tm, slots), lambda i: (i, 0)),
            pl.BlockSpec((tm, slots), lambda i: (i, 0)),
        ],
        out_shape=[
            jax.ShapeDtypeStruct((n, slots), I32),
            jax.ShapeDtypeStruct((n, slots), F32),
        ],
        compiler_params=_cparams(("arbitrary",), VMEM_LIMIT),
    )(qp, keys)


EXPERT_ROWS = 4


def _pack_table(tab):
    n, d = tab.shape
    assert d == 2 * EXPERT_ROWS * LANE
    bits = lax.bitcast_convert_type(tab.astype(BF16), jnp.uint16).astype(U32)
    packed = bits[:, :d // 2] | (bits[:, d // 2:] << 16)
    return packed.reshape(n * EXPERT_ROWS, LANE)


PACKED_ROWS = 2 * EXPERT_ROWS


def _gather_rows(tab_ref, idx_ref, t):
    rows = []
    for r in range(idx_ref.shape[1]):
        off = pl.multiple_of(idx_ref[t, r], EXPERT_ROWS)
        rows.append(tab_ref[pl.ds(off, EXPERT_ROWS), :])
    return jnp.concatenate(rows, axis=0)


def _pack_bf16_pair(lo, hi):
    lo_bits = lax.bitcast_convert_type(lo.astype(BF16).astype(F32), U32) >> 16
    hi_bits = lax.bitcast_convert_type(hi.astype(BF16).astype(F32), U32) & jnp.uint32(0xFFFF0000)
    return lo_bits | hi_bits


def _peer_u_kernel(idx_ref, h_ref, g_ref, tab_ref, fold_ref, w_ref):
    tb, slots = idx_ref.shape
    cols = slots * PACKED_ROWS
    sub = lax.broadcasted_iota(I32, (SUBLANE, cols), 0)
    lane = lax.broadcasted_iota(I32, (SUBLANE, cols), 1)
    diag = (lane & (PACKED_ROWS - 1)) == sub
    fold = fold_ref[...]

    def group(t8, carry):
        v8 = jnp.zeros((SUBLANE, cols), F32)
        for s in range(SUBLANE):
            t = t8 * SUBLANE + s
            words = _gather_rows(tab_ref, idx_ref, t)
            base = pl.multiple_of(t * SUBLANE, SUBLANE)
            hw = _pack_bf16_pair(h_ref[pl.ds(base, EXPERT_ROWS), :], h_ref[pl.ds(base + EXPERT_ROWS, EXPERT_ROWS), :])
            m = lax.dot_general(pltpu.bitcast(hw, BF16), pltpu.bitcast(words, BF16), NT_DIMS,
                                preferred_element_type=F32)
            v = jnp.sum(jnp.where(diag, m, 0.0), axis=0, keepdims=True)
            v8 = jnp.where(sub == s, v, v8)
        hi = v8.astype(BF16)
        lo = (v8 - hi.astype(F32)).astype(BF16)
        act = jnp.dot(hi, fold, preferred_element_type=F32) + jnp.dot(lo, fold, preferred_element_type=F32)
        rows = pl.ds(pl.multiple_of(t8 * SUBLANE, SUBLANE), SUBLANE)
        w_ref[rows, :] = g_ref[rows, :] * jax.nn.gelu(act)
        return carry

    lax.fori_loop(0, tb // SUBLANE, group, 0)


def _peer_u_call(idx, h2, g, tab, *, tb):
    n, slots = idx.shape
    d = h2.shape[1]
    rows_per_tok = d // LANE
    h3 = h2.reshape(n * rows_per_tok, LANE)
    fold = jnp.repeat(jnp.eye(slots, dtype=BF16), PACKED_ROWS, axis=0)
    return pl.pallas_call(
        _peer_u_kernel,
        grid=(n // tb,),
        in_specs=[
            pl.BlockSpec((tb, slots), lambda i: (i, 0), memory_space=pltpu.SMEM),
            pl.BlockSpec((tb * rows_per_tok, LANE), lambda i: (i, 0)),
            pl.BlockSpec((tb, slots), lambda i: (i, 0)),
            pl.BlockSpec(tab.shape, lambda i: (0, 0), pipeline_mode=pl.Buffered(1)),
            pl.BlockSpec(fold.shape, lambda i: (0, 0)),
        ],
        out_specs=pl.BlockSpec((tb, slots), lambda i: (i, 0)),
        out_shape=jax.ShapeDtypeStruct((n, slots), F32),
        compiler_params=_cparams(("arbitrary",), VMEM_LIMIT),
    )(idx, h3, g, tab, fold)


def _peer_v_kernel(idx_ref, w_ref, x_ref, tab_ref, expand_ref, o_ref):
    tb, slots = idx_ref.shape
    cols = slots * PACKED_ROWS
    sub = lax.broadcasted_iota(I32, (SUBLANE, cols), 0)
    j = lax.broadcasted_iota(I32, (SUBLANE, cols), 1) & (PACKED_ROWS - 1)
    probe = pltpu.bitcast(jnp.full((EXPERT_ROWS, LANE), 0x40003F80, U32), BF16).astype(F32)
    low_first = probe[0:1, 0:1] == 1.0
    half = jnp.where(low_first, j & 1, 1 - (j & 1))
    place = (half * EXPERT_ROWS + (j >> 1)) == sub
    expand = expand_ref[...]

    def group(t8, carry):
        rows8 = pl.ds(pl.multiple_of(t8 * SUBLANE, SUBLANE), SUBLANE)
        wrep = jnp.dot(w_ref[rows8, :].astype(BF16), expand, preferred_element_type=F32)
        for s in range(SUBLANE):
            t = t8 * SUBLANE + s
            words = _gather_rows(tab_ref, idx_ref, t)
            lhs = jnp.where(place, wrep[s:s + 1, :], 0.0).astype(BF16)
            out = jnp.dot(lhs, pltpu.bitcast(words, BF16), preferred_element_type=F32)
            rows = pl.ds(pl.multiple_of(t * SUBLANE, SUBLANE), SUBLANE)
            o_ref[rows, :] = x_ref[rows, :] + out
        return carry

    lax.fori_loop(0, tb // SUBLANE, group, 0)


def _peer_v_call(idx, w, x1, tab, *, tb):
    n, slots = idx.shape
    d = x1.shape[1]
    rows_per_tok = d // LANE
    x3 = x1.reshape(n * rows_per_tok, LANE)
    expand = jnp.repeat(jnp.eye(slots, dtype=BF16), PACKED_ROWS, axis=1)
    out = pl.pallas_call(
        _peer_v_kernel,
        grid=(n // tb,),
        in_specs=[
            pl.BlockSpec((tb, slots), lambda i: (i, 0), memory_space=pltpu.SMEM),
            pl.BlockSpec((tb, slots), lambda i: (i, 0)),
            pl.BlockSpec((tb * rows_per_tok, LANE), lambda i: (i, 0)),
            pl.BlockSpec(tab.shape, lambda i: (0, 0), pipeline_mode=pl.Buffered(1)),
            pl.BlockSpec(expand.shape, lambda i: (0, 0)),
        ],
        out_specs=pl.BlockSpec((tb * rows_per_tok, LANE), lambda i: (i, 0)),
        out_shape=jax.ShapeDtypeStruct((n * rows_per_tok, LANE), F32),
        compiler_params=_cparams(("arbitrary",), VMEM_LIMIT),
    )(idx, w, x3, tab, expand)
    return out.reshape(n, d)


def _ple_kernel(x_ref, p_ref, g_ref, wg_ref, wp_ref, o_ref):
    x = x_ref[...]
    ms = jnp.mean(x * x, axis=-1, keepdims=True)
    hn = (x * lax.rsqrt(ms + RMS_EPS) * g_ref[...]).astype(BF16)
    gate = jax.nn.sigmoid(jnp.dot(hn, wg_ref[...], preferred_element_type=F32))
    pe = jnp.dot(p_ref[...].astype(BF16), wp_ref[...], preferred_element_type=F32)
    o_ref[...] = x + gate * pe


def _ple_call(x2, p, g, wg, wp, *, tm):
    n, d = x2.shape
    return pl.pallas_call(
        _ple_kernel,
        grid=(n // tm,),
        in_specs=[
            pl.BlockSpec((tm, d), lambda i: (i, 0)),
            pl.BlockSpec((tm, p.shape[1]), lambda i: (i, 0)),
            pl.BlockSpec((1, d), lambda i: (0, 0)),
            pl.BlockSpec(wg.shape, lambda i: (0, 0)),
            pl.BlockSpec(wp.shape, lambda i: (0, 0)),
        ],
        out_specs=pl.BlockSpec((tm, d), lambda i: (i, 0)),
        out_shape=jax.ShapeDtypeStruct((n, d), F32),
        compiler_params=_cparams(("arbitrary",), VMEM_LIMIT),
    )(x2, p, g, wg, wp)


def _rope_tables(pos, dim):
    half = dim // 2
    inv = ROPE_BASE ** (-jnp.arange(half, dtype=F32) / half)
    ang = pos.astype(F32)[:, None] * inv[None, :]
    cos, sin = jnp.cos(ang), jnp.sin(ang)
    return jnp.concatenate([cos, cos], axis=1), jnp.concatenate([-sin, sin], axis=1)


def _ret_tables(heads, chunk, n_valid):
    lg = jnp.log1p(-jnp.exp2(-5.0 - jnp.arange(heads, dtype=F32)))
    idx = jnp.arange(chunk, dtype=F32)
    diff = idx[:, None] - idx[None, :]
    causal = diff >= 0
    dec = jnp.where(causal[None], jnp.exp(jnp.where(causal, diff, 0.0)[None] * lg[:, None, None]), 0.0)
    qd = jnp.exp((idx[None, :] + 1.0) * lg[:, None])
    kd = jnp.exp((n_valid - 1.0 - idx)[None, :] * lg[:, None])
    kd = jnp.where(idx[None, :] < n_valid, kd, 0.0)
    sd = jnp.exp(n_valid * lg)
    return dec, qd[:, :, None], kd[:, :, None], sd[:, None, None]


def _token_tail(x2d, oa, ob, proj, p2d, lw, *, tm, tb, gate_col0):
    x1, h2, qp = _merge_call(x2d, oa, ob, proj, lw["wa"], lw["wb"], lw["wo"], lw["gf"], lw["wq"],
                             tm=tm, gate_col0=gate_col0)
    e, g = _peer_topk_call(qp, lw["keys"], tm=tm)
    w = _peer_u_call(e, h2, g, lw["u_tab"], tb=tb)
    x2 = _peer_v_call(e, w, x1, lw["v_tab"], tb=tb)
    return _ple_call(x2, p2d, lw["gp"], lw["wg"], lw["wp"], tm=tm)


def kernel(x_prompt, x_sample, cache_k, cache_v, state_ret, page_table, p_prompt, p_sample, norm_mix_g, w_in,
           q_norm_g, k_norm_g, sb_bias, ret_norm_g, w_a_up, w_b_up, w_out, norm_ffn_g, peer_w_q, peer_sub_keys,
           peer_u, peer_v, norm_ple_g, w_ple_gate, w_ple):
    batch, seq, d = x_prompt.shape
    dec_batch, dec_seq, _ = x_sample.shape
    depth, n_phys, page, sb_heads, sb_dim = cache_k.shape
    _, _, ret_heads, dk, dv = state_ret.shape
    sb_width = sb_heads * sb_dim
    past_len = page_table.shape[1] * page
    chunk = LANE
    rows_pad = SUBLANE
    assert dec_seq <= rows_pad and dk == LANE and sb_width % LANE == 0
    qb_col0 = 3 * sb_width
    gate_col0 = qb_col0 + 2 * ret_heads * dk + 2 * ret_heads * dv

    xp = x_prompt.reshape(batch * seq, d)
    xs = jnp.pad(x_sample, ((0, 0), (0, rows_pad - dec_seq), (0, 0))).reshape(dec_batch * rows_pad, d)
    ns = dec_batch * rows_pad

    cs_p, sn_p = _rope_tables(jnp.arange(seq), dk)
    cs_s, sn_s = _rope_tables(past_len + jnp.arange(rows_pad), dk)
    cs_s, sn_s = jnp.tile(cs_s, (dec_batch, 1)), jnp.tile(sn_s, (dec_batch, 1))
    assert sb_dim & (sb_dim - 1) == 0 and sb_width == PROJ_TN
    gm = jnp.kron(jnp.eye(sb_heads, dtype=F32), jnp.full((sb_dim, sb_dim), 1.0 / sb_dim, F32)).astype(BF16)
    tri = jnp.concatenate([jnp.tril(jnp.ones((LANE, LANE), F32), -1), jnp.ones((LANE, LANE), F32)], axis=1).astype(BF16)
    dec_p = _ret_tables(ret_heads, chunk, float(chunk))
    dec_s = _ret_tables(ret_heads, chunk, float(dec_seq))

    tm = 256
    tb = 256
    kp, vp, sp, kn, vn, sn_out = [], [], [], [], [], []
    for i in range(depth):
        lw = dict(
            wa=w_a_up[i].astype(BF16), wb=w_b_up[i].astype(BF16), wo=w_out[i].astype(BF16),
            gf=norm_ffn_g[i][None, :], wq=peer_w_q[i].astype(BF16), keys=peer_sub_keys[i],
            u_tab=_pack_table(peer_u[i]), v_tab=_pack_table(peer_v[i]),
            gp=norm_ple_g[i][None, :], wg=w_ple_gate[i].astype(BF16), wp=w_ple[i].astype(BF16),
        )
        w_bf = w_in[i].astype(BF16)
        g_mix = norm_mix_g[i][None, :]
        qg = jnp.tile(q_norm_g[i], sb_heads)[None, :]
        kg = jnp.tile(k_norm_g[i], sb_heads)[None, :]
        rg = ret_norm_g[i][:, None, :]
        kb_scale = dk ** -0.5

        proj = _proj_call(xp, g_mix, w_bf, cs_p, sn_p, qg, kg, gm, tm=tm, pos_blocks=seq // tm, kb_scale=kb_scale)
        oa = _sb_prompt_call(proj, sb_bias[i], batch=batch, seq=seq, heads=sb_heads, head_dim=sb_dim)
        ob, st_p = _ret_call(proj, None, *dec_p, rg, batch=batch, n_chunks=seq // chunk, rows=chunk,
                             heads=ret_heads, dk=dk, dv=dv, col0=qb_col0)
        xp = _token_tail(xp, oa, ob, proj, p_prompt[i].reshape(batch * seq, -1), lw, tm=tm, tb=tb,
                         gate_col0=gate_col0)
        kp.append(proj[:, sb_width:2 * sb_width].reshape(batch, seq, sb_heads, sb_dim))
        vp.append(proj[:, 2 * sb_width:3 * sb_width].reshape(batch, seq, sb_heads, sb_dim))
        sp.append(st_p)

        proj_s = _proj_call(xs, g_mix, w_bf, cs_s, sn_s, qg, kg, gm, tm=ns, pos_blocks=1, kb_scale=kb_scale)
        bias_col = jnp.repeat(sb_bias[i], rows_pad)[:, None]
        oa_s = _sb_sample_call(page_table, proj_s, jnp.transpose(cache_k[i], (0, 2, 3, 1)),
                               jnp.transpose(cache_v[i], (0, 2, 3, 1)), tri, bias_col,
                               dec_batch=dec_batch, rows_pad=rows_pad, heads=sb_heads, head_dim=sb_dim)
        ob_s, st_s = _ret_call(proj_s, state_ret[i], *dec_s, rg, batch=dec_batch, n_chunks=1, rows=rows_pad,
                               heads=ret_heads, dk=dk, dv=dv, col0=qb_col0)
        p_s = jnp.pad(p_sample[i], ((0, 0), (0, rows_pad - dec_seq), (0, 0))).reshape(ns, -1)
        xs = _token_tail(xs, oa_s, ob_s, proj_s, p_s, lw, tm=ns, tb=ns, gate_col0=gate_col0)
        ka_s = proj_s[:, sb_width:2 * sb_width].reshape(dec_batch, rows_pad, sb_heads, sb_dim)[:, :dec_seq]
        va_s = proj_s[:, 2 * sb_width:3 * sb_width].reshape(dec_batch, rows_pad, sb_heads, sb_dim)[:, :dec_seq]
        kn.append(ka_s)
        vn.append(va_s)
        sn_out.append(st_s)

    y_prompt = xp.reshape(batch, seq, d)
    y_sample = xs.reshape(dec_batch, rows_pad, d)[:, :dec_seq]
    return (y_prompt, y_sample, jnp.stack(kp), jnp.stack(vp), jnp.stack(sp), jnp.stack(kn), jnp.stack(vn),
            jnp.stack(sn_out))
```

```python
import functools

import numpy as np
import jax
import jax.numpy as jnp
from jax import lax
from jax.experimental import pallas as pl
from jax.experimental.pallas import tpu as pltpu

F32 = jnp.float32
BF16 = jnp.bfloat16
I32 = jnp.int32
U32 = jnp.uint32
HIGHEST = lax.Precision.HIGHEST

RMS_EPS = 1e-6
ROPE_BASE = 10000.0
PEER_TOPK = 16
LANE = 128
SUBLANE = 8
VMEM_LIMIT = 56 * 1024 * 1024
NT_DIMS = (((1,), (1,)), ((), ()))
TN_DIMS = (((0,), (0,)), ((), ()))


def _cparams(sem, vmem=None):
    return pltpu.CompilerParams(dimension_semantics=sem, vmem_limit_bytes=vmem)


PROJ_TN = 512


def _proj_kernel(x_ref, g_ref, w_ref, cs_ref, sn_ref, qg_ref, kg_ref, gm_ref, o_ref, *, kb_scale):
    x = x_ref[...]
    ms = jnp.mean(x * x, axis=-1, keepdims=True)
    hn = (x * lax.rsqrt(ms + RMS_EPS) * g_ref[...]).astype(BF16)
    gm = gm_ref[...]
    for j in range(o_ref.shape[1] // PROJ_TN):
        cols = slice(j * PROJ_TN, (j + 1) * PROJ_TN)
        acc = jnp.dot(hn, w_ref[:, cols], preferred_element_type=F32)
        if j <= 1:
            sq = acc * acc
            hi = sq.astype(BF16)
            lo = (sq - hi.astype(F32)).astype(BF16)
            msq = jnp.dot(hi, gm, preferred_element_type=F32) + jnp.dot(lo, gm, preferred_element_type=F32)
            gain = qg_ref[...] if j == 0 else kg_ref[...]
            o_ref[:, cols] = acc * lax.rsqrt(msq + RMS_EPS) * gain
        elif j in (3, 4):
            cs = cs_ref[...]
            sn = sn_ref[...]
            for h in range(PROJ_TN // LANE):
                xh = acc[:, h * LANE:(h + 1) * LANE]
                r = pltpu.roll(xh, LANE // 2, 1)
                y = xh * cs + r * sn
                lo_col = j * PROJ_TN + h * LANE
                o_ref[:, lo_col:lo_col + LANE] = y * kb_scale if j == 4 else y
        else:
            o_ref[:, cols] = acc


def _proj_call(x2d, g, w_bf, cs, sn, qg, kg, gm, *, tm, pos_blocks, kb_scale):
    n, d = x2d.shape
    pw = w_bf.shape[1]
    const = lambda i: (0, 0)
    return pl.pallas_call(
        functools.partial(_proj_kernel, kb_scale=kb_scale),
        grid=(n // tm,),
        in_specs=[
            pl.BlockSpec((tm, d), lambda i: (i, 0)),
            pl.BlockSpec((1, d), const),
            pl.BlockSpec((d, pw), const, pipeline_mode=pl.Buffered(1)),
            pl.BlockSpec((tm, LANE), lambda i: (i % pos_blocks, 0)),
            pl.BlockSpec((tm, LANE), lambda i: (i % pos_blocks, 0)),
            pl.BlockSpec((1, PROJ_TN), const),
            pl.BlockSpec((1, PROJ_TN), const),
            pl.BlockSpec((PROJ_TN, PROJ_TN), const),
        ],
        out_specs=pl.BlockSpec((tm, pw), lambda i: (i, 0)),
        out_shape=jax.ShapeDtypeStruct((n, pw), F32),
        compiler_params=_cparams(("arbitrary",), VMEM_LIMIT),
    )(x2d, g, w_bf, cs, sn, qg, kg, gm)


def _sb_prompt_kernel(bias_ref, q_ref, k_ref, v_ref, tri_ref, o_ref, qs_ref, acc_ref, car_ref, *, scale):
    i = pl.program_id(1)
    tq = q_ref.shape[0]
    pairs = qs_ref.shape[0]
    lane = lax.broadcasted_iota(I32, (1, LANE), 1)
    hmask = [lane < LANE // 2, lane >= LANE // 2]
    first = lax.broadcasted_iota(I32, (1, 2 * LANE), 1) < LANE

    def split_heads(x):
        return jnp.concatenate([jnp.where(hmask[0], x, 0.0), jnp.where(hmask[1], x, 0.0)], axis=0).astype(BF16)

    for t in range(pairs):
        qs_ref[t] = (q_ref[:, t * LANE:(t + 1) * LANE] * scale).astype(BF16)
    tri2 = tri_ref[...]
    acc_ref[...] = jnp.zeros_like(acc_ref)
    car_ref[...] = jnp.zeros_like(car_ref)
    row = lax.broadcasted_iota(I32, (tq, 2 * LANE), 0)
    col = lax.broadcasted_iota(I32, (tq, 2 * LANE), 1) & (LANE - 1)
    causal2 = col < row

    def block(kbs, mask2):
        starts = [pl.multiple_of(kb * LANE, LANE) for kb in kbs]
        tiles = [slice(t * LANE, (t + 1) * LANE) for t in range(pairs)]
        units = [(n, t) for n in range(len(kbs)) for t in range(pairs)]
        z = {}
        for n, t in units:
            rk = split_heads(k_ref[pl.ds(starts[n], LANE), tiles[t]])
            bias2 = jnp.where(first, bias_ref[2 * t], bias_ref[2 * t + 1])
            z[n, t] = lax.dot_general(qs_ref[t], rk, NT_DIMS, preferred_element_type=F32) + bias2
        sp = {u: jnp.maximum(z[u], 0.0) + jnp.log(1.0 + jnp.exp(-jnp.abs(z[u]))) for u in units}
        spm = sp if mask2 is None else {u: jnp.where(mask2, sp[u], 0.0) for u in units}
        cs = {}
        for u in units:
            hi = spm[u].astype(BF16)
            lo = (spm[u] - hi.astype(F32)).astype(BF16)
            cs[u] = jnp.dot(hi, tri2, preferred_element_type=F32) + jnp.dot(lo, tri2, preferred_element_type=F32)
        a = {}
        for t in range(pairs):
            car = car_ref[t]
            for n in range(len(kbs)):
                at = jnp.exp(z[n, t] - sp[n, t] - cs[n, t] - car)
                a[n, t] = at if mask2 is None else jnp.where(mask2, at, 0.0)
                tot = [jnp.broadcast_to(jnp.sum(spm[n, t][:, h * LANE:(h + 1) * LANE], axis=1, keepdims=True),
                                        (tq, LANE)) for h in range(2)]
                car = car + jnp.concatenate(tot, axis=1)
            car_ref[t] = car
        for t in range(pairs):
            contrib = None
            for n in range(len(kbs)):
                rv = split_heads(v_ref[pl.ds(starts[n], LANE), tiles[t]])
                d = jnp.dot(a[n, t].astype(BF16), rv, preferred_element_type=F32)
                contrib = d if contrib is None else contrib + d
            acc_ref[:, tiles[t]] += contrib

    block([i], causal2)

    def body(jj, c):
        block([i - 1 - 2 * jj, i - 2 - 2 * jj], None)
        return c

    lax.fori_loop(0, lax.shift_right_logical(i, 1), body, 0)

    @pl.when((i & 1) == 1)
    def _():
        block([0], None)

    o_ref[...] = acc_ref[...]


def _sb_prompt_call(proj, bias, *, batch, seq, heads, head_dim):
    assert 2 * head_dim == LANE
    tq = LANE
    nq = seq // tq
    width = heads * head_dim
    pairs = heads // 2
    tri2 = jnp.kron(jnp.eye(2, dtype=F32), jnp.tril(jnp.ones((LANE, LANE), F32), -1)).astype(BF16)
    return pl.pallas_call(
        functools.partial(_sb_prompt_kernel, scale=head_dim ** -0.5),
        grid=(batch, nq),
        in_specs=[
            pl.BlockSpec(memory_space=pltpu.SMEM),
            pl.BlockSpec((tq, width), lambda b, i: (b * nq + i, 0)),
            pl.BlockSpec((seq, width), lambda b, i: (b, 1)),
            pl.BlockSpec((seq, width), lambda b, i: (b, 2)),
            pl.BlockSpec((2 * LANE, 2 * LANE), lambda b, i: (0, 0)),
        ],
        out_specs=pl.BlockSpec((tq, width), lambda b, i: (b * nq + i, 0)),
        out_shape=jax.ShapeDtypeStruct((batch * seq, width), F32),
        scratch_shapes=[
            pltpu.VMEM((pairs, tq, LANE), BF16),
            pltpu.VMEM((tq, width), F32),
            pltpu.VMEM((pairs, tq, 2 * LANE), F32),
        ],
        compiler_params=_cparams(("arbitrary", "arbitrary"), VMEM_LIMIT),
    )(bias, proj, proj, proj, tri2)


def _sb_sample_kernel(pt_ref, q_ref, kn_ref, vn_ref, *rest, scale, pages_per_step):
    kc_refs = rest[:pages_per_step]
    vc_refs = rest[pages_per_step:2 * pages_per_step]
    tri_ref, bias_ref, o_ref, qh_ref, nk_ref, nv_ref, acc_ref, car_ref = rest[2 * pages_per_step:]
    p = pl.program_id(1)
    heads, rq, hd = qh_ref.shape
    tri = tri_ref[...]
    bias = bias_ref[...]
    nn_dims = (((1,), (0,)), ((), ()))

    def blocks(load_k, load_v, n_blocks, mask, token_major):
        qk_dims, av_dims = (NT_DIMS, nn_dims) if token_major else (nn_dims, NT_DIMS)
        z = []
        for j in range(n_blocks):
            zj = [lax.dot_general(qh_ref[h], load_k(j, h).astype(BF16), qk_dims, preferred_element_type=F32)
                  for h in range(heads)]
            z.append(jnp.concatenate(zj, axis=0) + bias)
        sp = [jnp.maximum(zj, 0.0) + jnp.log(1.0 + jnp.exp(-jnp.abs(zj))) for zj in z]
        spm = sp if mask is None else [jnp.where(mask, s, 0.0) for s in sp]
        cs = []
        for s in spm:
            hi = s.astype(BF16)
            lo = (s - hi.astype(F32)).astype(BF16)
            cs.append(jnp.dot(hi, tri, preferred_element_type=F32) + jnp.dot(lo, tri, preferred_element_type=F32))
        car = car_ref[...]
        totals = [None] * heads
        for j in range(n_blocks):
            a = jnp.exp(z[j] - sp[j] - cs[j][:, :LANE] - car)
            if mask is not None:
                a = jnp.where(mask, a, 0.0)
            car = car + cs[j][:, LANE:]
            for h in range(heads):
                d = lax.dot_general(a[h * rq:(h + 1) * rq, :].astype(BF16), load_v(j, h).astype(BF16), av_dims,
                                    preferred_element_type=F32)
                totals[h] = d if totals[h] is None else totals[h] + d
        car_ref[...] = car
        for h in range(heads):
            acc_ref[h * rq:(h + 1) * rq, :] += totals[h]

    @pl.when(p == 0)
    def _():
        nk_ref[...] = jnp.zeros_like(nk_ref)
        nv_ref[...] = jnp.zeros_like(nv_ref)
        for h in range(heads):
            cols = slice(h * hd, (h + 1) * hd)
            qh_ref[h] = (q_ref[:, cols] * scale).astype(BF16)
            nk_ref[h, 0:rq, :] = kn_ref[:, cols]
            nv_ref[h, 0:rq, :] = vn_ref[:, cols]
        acc_ref[...] = jnp.zeros_like(acc_ref)
        car_ref[...] = jnp.zeros_like(car_ref)
        qrow = lax.broadcasted_iota(I32, (heads * rq, LANE), 0) & (rq - 1)
        col = lax.broadcasted_iota(I32, (heads * rq, LANE), 1)
        blocks(lambda j, h: nk_ref[h], lambda j, h: nv_ref[h], 1, col < qrow, True)

    blocks(lambda j, h: kc_refs[j][h], lambda j, h: vc_refs[j][h], pages_per_step, None, False)

    @pl.when(p == pl.num_programs(1) - 1)
    def _():
        for h in range(heads):
            o_ref[:, h * hd:(h + 1) * hd] = acc_ref[h * rq:(h + 1) * rq, :]


def _sb_sample_call(page_table, proj_s, cache_k3, cache_v3, tri, bias_col, *, dec_batch, rows_pad, heads, head_dim):
    n_pages = page_table.shape[1]
    width = heads * head_dim
    page = cache_k3.shape[3]
    rows = rows_pad * heads
    pps = max(c for c in (8, 4, 2, 1) if n_pages % c == 0)

    def page_spec(j):
        return pl.BlockSpec((None, heads, head_dim, page),
                            lambda b, p, pt: (pt[b * n_pages + n_pages - 1 - (p * pps + j)], 0, 0, 0))

    grid_spec = pltpu.PrefetchScalarGridSpec(
        num_scalar_prefetch=1,
        grid=(dec_batch, n_pages // pps),
        in_specs=[
            pl.BlockSpec((rows_pad, width), lambda b, p, pt: (b, 0)),
            pl.BlockSpec((rows_pad, width), lambda b, p, pt: (b, 1)),
            pl.BlockSpec((rows_pad, width), lambda b, p, pt: (b, 2)),
            *[page_spec(j) for j in range(pps)],
            *[page_spec(j) for j in range(pps)],
            pl.BlockSpec((LANE, 2 * LANE), lambda b, p, pt: (0, 0)),
            pl.BlockSpec((rows, 1), lambda b, p, pt: (0, 0)),
        ],
        out_specs=pl.BlockSpec((rows_pad, width), lambda b, p, pt: (b, 0)),
        scratch_shapes=[
            pltpu.VMEM((heads, rows_pad, head_dim), BF16),
            pltpu.VMEM((heads, page, head_dim), F32),
            pltpu.VMEM((heads, page, head_dim), F32),
            pltpu.VMEM((rows, head_dim), F32),
            pltpu.VMEM((rows, LANE), F32),
        ],
    )
    return pl.pallas_call(
        functools.partial(_sb_sample_kernel, scale=head_dim ** -0.5, pages_per_step=pps),
        grid_spec=grid_spec,
        out_shape=jax.ShapeDtypeStruct((dec_batch * rows_pad, width), F32),
        compiler_params=_cparams(("arbitrary", "arbitrary"), VMEM_LIMIT),
    )(page_table.reshape(-1), proj_s, proj_s, proj_s, *([cache_k3] * pps), *([cache_v3] * pps), tri, bias_col)


def _ret_kernel(q_ref, k_ref, v0_ref, v1_ref, g0_ref, g1_ref, st0_ref, dec_ref, qd_ref, kd_ref, sd_ref, rg_ref,
                ob_ref, st_ref, state_ref, qp_ref, kp_ref, vp_ref, *, use_init):
    c = pl.program_id(1)
    heads, dk, dv = state_ref.shape
    rows = q_ref.shape[0]
    chunk = dec_ref.shape[1]
    half = v0_ref.shape[1]

    @pl.when(c == 0)
    def _():
        if use_init:
            state_ref[...] = st0_ref[...]
        else:
            state_ref[...] = jnp.zeros_like(state_ref)

    if rows == chunk:
        q, k = q_ref[...], k_ref[...]
        v = jnp.concatenate([v0_ref[...], v1_ref[...]], axis=1)
    else:
        qp_ref[...] = jnp.zeros_like(qp_ref)
        kp_ref[...] = jnp.zeros_like(kp_ref)
        vp_ref[...] = jnp.zeros_like(vp_ref)
        qp_ref[0:rows, :] = q_ref[...]
        kp_ref[0:rows, :] = k_ref[...]
        vp_ref[0:rows, 0:half] = v0_ref[...]
        vp_ref[0:rows, half:2 * half] = v1_ref[...]
        q, k, v = qp_ref[...], kp_ref[...], vp_ref[...]
    gb = jnp.concatenate([g0_ref[...], g1_ref[...]], axis=1)

    qb = [q[:, h * dk:(h + 1) * dk].astype(BF16) for h in range(heads)]
    vb = [v[:, h * dv:(h + 1) * dv].astype(BF16) for h in range(heads)]
    st = [state_ref[h] for h in range(heads)]
    scores = [lax.dot_general(qb[h], k[:, h * dk:(h + 1) * dk].astype(BF16), NT_DIMS, preferred_element_type=F32)
              * dec_ref[h] for h in range(heads)]
    cross = [jnp.dot(qb[h], st[h].astype(BF16), preferred_element_type=F32) * qd_ref[h] for h in range(heads)]
    kv = [lax.dot_general((k[:, h * dk:(h + 1) * dk] * kd_ref[h]).astype(BF16), vb[h], TN_DIMS,
                          preferred_element_type=F32) for h in range(heads)]
    inner = [jnp.dot(scores[h].astype(BF16), vb[h], preferred_element_type=F32) for h in range(heads)]
    for h in range(heads):
        state_ref[h] = sd_ref[h] * st[h] + kv[h]
        o = (inner[h] + cross[h])[0:rows, :]
        ms = jnp.mean(o * o, axis=-1, keepdims=True)
        g = gb[:, h * dv:(h + 1) * dv]
        ob_ref[:, h * dv:(h + 1) * dv] = (o * lax.rsqrt(ms + RMS_EPS) * rg_ref[h]) * (g * jax.nn.sigmoid(g))

    @pl.when(c == pl.num_programs(1) - 1)
    def _():
        st_ref[...] = state_ref[...]


def _ret_call(proj, state0, dec, qd, kd, sd, rg, *, batch, n_chunks, rows, heads, dk, dv, col0):
    chunk = dec.shape[1]
    qw = heads * dk
    assert heads * dv == 2 * qw and col0 % qw == 0
    qc = col0 // qw
    use_init = state0 is not None
    if state0 is None:
        state0 = jnp.zeros((1, heads, dk, dv), F32)
        st_map = lambda b, c: (0, 0, 0, 0)
    else:
        st_map = lambda b, c: (b, 0, 0, 0)
    pad_rows = chunk if rows != chunk else SUBLANE
    row = lambda j: (lambda b, c: (b * n_chunks + c, qc + j))
    const = lambda b, c: (0, 0, 0)
    return pl.pallas_call(
        functools.partial(_ret_kernel, use_init=use_init),
        grid=(batch, n_chunks),
        in_specs=[
            *[pl.BlockSpec((rows, qw), row(j)) for j in range(6)],
            pl.BlockSpec((None, heads, dk, dv), st_map),
            pl.BlockSpec((heads, chunk, chunk), const),
            pl.BlockSpec((heads, chunk, 1), const),
            pl.BlockSpec((heads, chunk, 1), const),
            pl.BlockSpec((heads, 1, 1), const),
            pl.BlockSpec((heads, 1, dv), const),
        ],
        out_specs=[
            pl.BlockSpec((rows, heads * dv), lambda b, c: (b * n_chunks + c, 0)),
            pl.BlockSpec((None, heads, dk, dv), lambda b, c: (b, 0, 0, 0)),
        ],
        out_shape=[
            jax.ShapeDtypeStruct((batch * n_chunks * rows, heads * dv), F32),
            jax.ShapeDtypeStruct((batch, heads, dk, dv), F32),
        ],
        scratch_shapes=[
            pltpu.VMEM((heads, dk, dv), F32),
            pltpu.VMEM((pad_rows, qw), F32),
            pltpu.VMEM((pad_rows, qw), F32),
            pltpu.VMEM((pad_rows, 2 * qw), F32),
        ],
        compiler_params=_cparams(("arbitrary", "arbitrary"), VMEM_LIMIT),
    )(proj, proj, proj, proj, proj, proj, state0, dec, qd, kd, sd, rg)


def _merge_kernel(x_ref, oa_ref, ob_ref, ga0_ref, ga1_ref, gb0_ref, gb1_ref, wa_ref, wb_ref, wo_ref, gf_ref,
                  wq_ref, x1_ref, h2_ref, qp_ref):
    ya = jnp.dot(oa_ref[...].astype(BF16), wa_ref[...], preferred_element_type=F32)
    yb = jnp.dot(ob_ref[...].astype(BF16), wb_ref[...], preferred_element_type=F32)
    ga = jnp.concatenate([ga0_ref[...], ga1_ref[...]], axis=1)
    gb = jnp.concatenate([gb0_ref[...], gb1_ref[...]], axis=1)
    merged = jax.nn.sigmoid(ga) * ya + jax.nn.sigmoid(gb) * yb
    x1 = x_ref[...] + jnp.dot(merged.astype(BF16), wo_ref[...], preferred_element_type=F32)
    x1_ref[...] = x1
    ms = jnp.mean(x1 * x1, axis=-1, keepdims=True)
    h2 = x1 * lax.rsqrt(ms + RMS_EPS) * gf_ref[...]
    h2_ref[...] = h2
    qp_ref[...] = jnp.dot(h2.astype(BF16), wq_ref[...], preferred_element_type=F32)


def _merge_call(x2d, oa, ob, proj, wa, wb, wo, gf, wq, *, tm, gate_col0):
    n, d = x2d.shape
    half = d // 2
    g0 = gate_col0 // half
    row = lambda i: (i, 0)
    const = lambda i: (0, 0)
    return pl.pallas_call(
        _merge_kernel,
        grid=(n // tm,),
        in_specs=[
            pl.BlockSpec((tm, d), row),
            pl.BlockSpec((tm, oa.shape[1]), row),
            pl.BlockSpec((tm, ob.shape[1]), row),
            pl.BlockSpec((tm, half), lambda i: (i, g0)),
            pl.BlockSpec((tm, half), lambda i: (i, g0 + 1)),
            pl.BlockSpec((tm, half), lambda i: (i, g0 + 2)),
            pl.BlockSpec((tm, half), lambda i: (i, g0 + 3)),
            pl.BlockSpec(wa.shape, const),
            pl.BlockSpec(wb.shape, const),
            pl.BlockSpec(wo.shape, const),
            pl.BlockSpec((1, d), const),
            pl.BlockSpec(wq.shape, const),
        ],
        out_specs=[
            pl.BlockSpec((tm, d), row),
            pl.BlockSpec((tm, d), row),
            pl.BlockSpec((tm, wq.shape[1]), row),
        ],
        out_shape=[
            jax.ShapeDtypeStruct((n, d), F32),
            jax.ShapeDtypeStruct((n, d), F32),
            jax.ShapeDtypeStruct((n, wq.shape[1]), F32),
        ],
        compiler_params=_cparams(("arbitrary",), VMEM_LIMIT),
    )(x2d, oa, ob, proj, proj, proj, proj, wa, wb, wo, gf, wq)


def _topk_rows(s, codes, k):
    t = s.shape[1]
    out_row = lax.broadcasted_iota(I32, (k, t), 0)
    vals = jnp.zeros((k, t), F32)
    picked = jnp.zeros((k, t), F32)
    for r in range(k):
        m = jnp.max(s, axis=0, keepdims=True)
        am = jnp.min(jnp.where(s == m, codes, jnp.inf), axis=0, keepdims=True)
        vals = jnp.where(out_row == r, m, vals)
        picked = jnp.where(out_row == r, am, picked)
        s = jnp.where(codes == am, -jnp.inf, s)
    return vals, picked


def _select_rows(table, sel):
    out = jnp.zeros(sel.shape, table.dtype)
    for a in range(table.shape[0]):
        out = jnp.where(sel == a, table[a:a + 1, :], out)
    return out


def _pair_candidates(s1, s2):
    k, t = s1.shape
    blk = SUBLANE
    a_lo = lax.broadcasted_iota(I32, (blk, t), 0)
    vals, codes = [], []
    for a0 in range(0, k, blk):
        vals.append(s1[a0:a0 + blk, :] + s2[0:1, :])
        codes.append((a_lo + a0) * k)
    for b in range(1, blk):
        ok = a_lo < k // (b + 1)
        vals.append(jnp.where(ok, s1[0:blk, :] + s2[b:b + 1, :], -jnp.inf))
        codes.append(a_lo * k + b)
    for b0 in range(blk, k, blk):
        vals.append(s1[0:1, :] + s2[b0:b0 + blk, :])
        codes.append(a_lo + b0)
    return jnp.concatenate(vals, axis=0), jnp.concatenate(codes, axis=0).astype(F32)


def _peer_topk_kernel(qp_ref, keys_ref, e_ref, g_ref, *, n_keys):
    heads = keys_ref.shape[0]
    half = keys_ref.shape[3]
    k = PEER_TOPK
    t = qp_ref.shape[0]
    key_codes = lax.broadcasted_iota(I32, (n_keys, t), 0).astype(F32)
    e_rows, g_rows = [], []
    for h in range(heads):
        tops = []
        for c in range(2):
            col = (h * 2 + c) * half
            q = qp_ref[:, col:col + half].astype(BF16)
            st = lax.dot_general(keys_ref[h, c].astype(BF16), q, NT_DIMS, preferred_element_type=F32)
            tops.append(_topk_rows(st, key_codes, k))
        (s1, i1), (s2, i2) = tops
        cand, codes = _pair_candidates(s1, s2)
        top, flat = _topk_rows(cand, codes, k)
        flat = flat.astype(I32)
        ia = _select_rows(i1, flat // k)
        ib = _select_rows(i2, flat % k)
        e_rows.append(((ia * n_keys + ib) * EXPERT_ROWS).astype(I32))
        ex = jnp.exp(top - jnp.max(top, axis=0, keepdims=True))
        g_rows.append(ex / jnp.sum(ex, axis=0, keepdims=True))
    e_ref[...] = jnp.transpose(jnp.concatenate(e_rows, axis=0))
    g_ref[...] = jnp.transpose(jnp.concatenate(g_rows, axis=0))


def _peer_topk_call(qp, keys, *, tm):
    n = qp.shape[0]
    heads, _, n_keys, half = keys.shape
    slots = heads * PEER_TOPK
    return pl.pallas_call(
        functools.partial(_peer_topk_kernel, n_keys=n_keys),
        grid=(n // tm,),
        in_specs=[
            pl.BlockSpec((tm, qp.shape[1]), lambda i: (i, 0)),
            pl.BlockSpec(keys.shape, lambda i: (0, 0, 0, 0)),
        ],
        out_specs=[
            pl.BlockSpec((tm, slots), lambda i: (i, 0)),
            pl.BlockSpec((tm, slots), lambda i: (i, 0)),
        ],
        out_shape=[
            jax.ShapeDtypeStruct((n, slots), I32),
            jax.ShapeDtypeStruct((n, slots), F32),
        ],
        compiler_params=_cparams(("arbitrary",), VMEM_LIMIT),
    )(qp, keys)


EXPERT_ROWS = 4


def _pack_table(tab):
    n, d = tab.shape
    assert d == 2 * EXPERT_ROWS * LANE
    bits = lax.bitcast_convert_type(tab.astype(BF16), jnp.uint16).astype(U32)
    packed = bits[:, :d // 2] | (bits[:, d // 2:] << 16)
    return packed.reshape(n * EXPERT_ROWS, LANE)


PACKED_ROWS = 2 * EXPERT_ROWS
GROUPS_PER_ITER = 4


def _for_token_groups(n, group):
    per = max(c for c in range(1, GROUPS_PER_ITER + 1) if n % c == 0)

    def body(i, carry):
        for u in range(per):
            group(i * per + u, carry)
        return carry

    lax.fori_loop(0, n // per, body, 0)


def _gather_rows(tab_ref, idx_ref, t):
    rows = []
    for r in range(idx_ref.shape[1]):
        off = pl.multiple_of(idx_ref[t, r], EXPERT_ROWS)
        rows.append(tab_ref[pl.ds(off, EXPERT_ROWS), :])
    return jnp.concatenate(rows, axis=0)


def _pack_bf16_pair(lo, hi):
    lo_bits = lax.bitcast_convert_type(lo.astype(BF16).astype(F32), U32) >> 16
    hi_bits = lax.bitcast_convert_type(hi.astype(BF16).astype(F32), U32) & jnp.uint32(0xFFFF0000)
    return lo_bits | hi_bits


def _peer_u_kernel(idx_ref, h_ref, g_ref, tab_ref, fold_ref, w_ref):
    tb, slots = idx_ref.shape
    cols = slots * PACKED_ROWS
    sub = lax.broadcasted_iota(I32, (SUBLANE, cols), 0)
    lane = lax.broadcasted_iota(I32, (SUBLANE, cols), 1)
    diag = (lane & (PACKED_ROWS - 1)) == sub
    fold = fold_ref[...]

    def group(t8, carry):
        rows = pl.ds(pl.multiple_of(t8 * SUBLANE, SUBLANE), SUBLANE)
        h8 = h_ref[rows, :]
        half = h8.shape[1] // 2
        packed = _pack_bf16_pair(h8[:, :half], h8[:, half:])
        v8 = jnp.zeros((SUBLANE, cols), F32)
        for s in range(SUBLANE):
            t = t8 * SUBLANE + s
            words = _gather_rows(tab_ref, idx_ref, t)
            hw = jnp.concatenate([packed[s:s + 1, j * LANE:(j + 1) * LANE] for j in range(EXPERT_ROWS)], axis=0)
            m = lax.dot_general(pltpu.bitcast(hw, BF16), pltpu.bitcast(words, BF16), NT_DIMS,
                                preferred_element_type=F32)
            v = jnp.sum(jnp.where(diag, m, 0.0), axis=0, keepdims=True)
            v8 = jnp.where(sub == s, v, v8)
        hi = v8.astype(BF16)
        lo = (v8 - hi.astype(F32)).astype(BF16)
        act = jnp.dot(hi, fold, preferred_element_type=F32) + jnp.dot(lo, fold, preferred_element_type=F32)
        w_ref[rows, :] = g_ref[rows, :] * jax.nn.gelu(act)
        return carry

    _for_token_groups(tb // SUBLANE, group)


def _peer_u_call(idx, h2, g, tab, *, tb):
    n, slots = idx.shape
    d = h2.shape[1]
    fold = jnp.repeat(jnp.eye(slots, dtype=BF16), PACKED_ROWS, axis=0)
    return pl.pallas_call(
        _peer_u_kernel,
        grid=(n // tb,),
        in_specs=[
            pl.BlockSpec((tb, slots), lambda i: (i, 0), memory_space=pltpu.SMEM),
            pl.BlockSpec((tb, d), lambda i: (i, 0)),
            pl.BlockSpec((tb, slots), lambda i: (i, 0)),
            pl.BlockSpec(tab.shape, lambda i: (0, 0), pipeline_mode=pl.Buffered(1)),
            pl.BlockSpec(fold.shape, lambda i: (0, 0)),
        ],
        out_specs=pl.BlockSpec((tb, slots), lambda i: (i, 0)),
        out_shape=jax.ShapeDtypeStruct((n, slots), F32),
        compiler_params=_cparams(("arbitrary",), VMEM_LIMIT),
    )(idx, h2, g, tab, fold)


def _peer_v_kernel(idx_ref, w_ref, x_ref, tab_ref, expand_ref, o_ref):
    tb, slots = idx_ref.shape
    cols = slots * PACKED_ROWS
    sub = lax.broadcasted_iota(I32, (SUBLANE, cols), 0)
    j = lax.broadcasted_iota(I32, (SUBLANE, cols), 1) & (PACKED_ROWS - 1)
    probe = pltpu.bitcast(jnp.full((EXPERT_ROWS, LANE), 0x40003F80, U32), BF16).astype(F32)
    low_first = probe[0:1, 0:1] == 1.0
    half = jnp.where(low_first, j & 1, 1 - (j & 1))
    place = (half * EXPERT_ROWS + (j >> 1)) == sub
    expand = expand_ref[...]

    def group(t8, carry):
        rows8 = pl.ds(pl.multiple_of(t8 * SUBLANE, SUBLANE), SUBLANE)
        wrep = jnp.dot(w_ref[rows8, :].astype(BF16), expand, preferred_element_type=F32)
        outs = []
        for s in range(SUBLANE):
            t = t8 * SUBLANE + s
            words = _gather_rows(tab_ref, idx_ref, t)
            lhs = jnp.where(place, wrep[s:s + 1, :], 0.0).astype(BF16)
            outs.append(jnp.dot(lhs, pltpu.bitcast(words, BF16), preferred_element_type=F32))
        peer = jnp.concatenate(
            [jnp.concatenate([outs[s][i:i + 1, :] for s in range(SUBLANE)], axis=0) for i in range(PACKED_ROWS)],
            axis=1)
        o_ref[rows8, :] = x_ref[rows8, :] + peer
        return carry

    _for_token_groups(tb // SUBLANE, group)


def _peer_v_call(idx, w, x1, tab, *, tb):
    n, slots = idx.shape
    d = x1.shape[1]
    assert d == PACKED_ROWS * LANE
    expand = jnp.repeat(jnp.eye(slots, dtype=BF16), PACKED_ROWS, axis=1)
    return pl.pallas_call(
        _peer_v_kernel,
        grid=(n // tb,),
        in_specs=[
            pl.BlockSpec((tb, slots), lambda i: (i, 0), memory_space=pltpu.SMEM),
            pl.BlockSpec((tb, slots), lambda i: (i, 0)),
            pl.BlockSpec((tb, d), lambda i: (i, 0)),
            pl.BlockSpec(tab.shape, lambda i: (0, 0), pipeline_mode=pl.Buffered(1)),
            pl.BlockSpec(expand.shape, lambda i: (0, 0)),
        ],
        out_specs=pl.BlockSpec((tb, d), lambda i: (i, 0)),
        out_shape=jax.ShapeDtypeStruct((n, d), F32),
        compiler_params=_cparams(("arbitrary",), VMEM_LIMIT),
    )(idx, w, x1, tab, expand)


def _ple_kernel(x_ref, p_ref, g_ref, wg_ref, wp_ref, o_ref):
    x = x_ref[...]
    ms = jnp.mean(x * x, axis=-1, keepdims=True)
    hn = (x * lax.rsqrt(ms + RMS_EPS) * g_ref[...]).astype(BF16)
    gate = jax.nn.sigmoid(jnp.dot(hn, wg_ref[...], preferred_element_type=F32))
    pe = jnp.dot(p_ref[...].astype(BF16), wp_ref[...], preferred_element_type=F32)
    o_ref[...] = x + gate * pe


def _ple_call(x2, p, g, wg, wp, *, tm):
    n, d = x2.shape
    return pl.pallas_call(
        _ple_kernel,
        grid=(n // tm,),
        in_specs=[
            pl.BlockSpec((tm, d), lambda i: (i, 0)),
            pl.BlockSpec((tm, p.shape[1]), lambda i: (i, 0)),
            pl.BlockSpec((1, d), lambda i: (0, 0)),
            pl.BlockSpec(wg.shape, lambda i: (0, 0)),
            pl.BlockSpec(wp.shape, lambda i: (0, 0)),
        ],
        out_specs=pl.BlockSpec((tm, d), lambda i: (i, 0)),
        out_shape=jax.ShapeDtypeStruct((n, d), F32),
        compiler_params=_cparams(("arbitrary",), VMEM_LIMIT),
    )(x2, p, g, wg, wp)


def _rope_tables(pos, dim):
    half = dim // 2
    inv = ROPE_BASE ** (-jnp.arange(half, dtype=F32) / half)
    ang = pos.astype(F32)[:, None] * inv[None, :]
    cos, sin = jnp.cos(ang), jnp.sin(ang)
    return jnp.concatenate([cos, cos], axis=1), jnp.concatenate([-sin, sin], axis=1)


def _ret_tables(heads, chunk, n_valid):
    lg = jnp.log1p(-jnp.exp2(-5.0 - jnp.arange(heads, dtype=F32)))
    idx = jnp.arange(chunk, dtype=F32)
    diff = idx[:, None] - idx[None, :]
    causal = diff >= 0
    dec = jnp.where(causal[None], jnp.exp(jnp.where(causal, diff, 0.0)[None] * lg[:, None, None]), 0.0)
    qd = jnp.exp((idx[None, :] + 1.0) * lg[:, None])
    kd = jnp.exp((n_valid - 1.0 - idx)[None, :] * lg[:, None])
    kd = jnp.where(idx[None, :] < n_valid, kd, 0.0)
    sd = jnp.exp(n_valid * lg)
    return dec, qd[:, :, None], kd[:, :, None], sd[:, None, None]


def _token_tail(x2d, oa, ob, proj, p2d, lw, *, tm, tb, gate_col0):
    x1, h2, qp = _merge_call(x2d, oa, ob, proj, lw["wa"], lw["wb"], lw["wo"], lw["gf"], lw["wq"],
                             tm=tm, gate_col0=gate_col0)
    e, g = _peer_topk_call(qp, lw["keys"], tm=tm)
    w = _peer_u_call(e, h2, g, lw["u_tab"], tb=tb)
    x2 = _peer_v_call(e, w, x1, lw["v_tab"], tb=tb)
    return _ple_call(x2, p2d, lw["gp"], lw["wg"], lw["wp"], tm=tm)


def kernel(x_prompt, x_sample, cache_k, cache_v, state_ret, page_table, p_prompt, p_sample, norm_mix_g, w_in,
           q_norm_g, k_norm_g, sb_bias, ret_norm_g, w_a_up, w_b_up, w_out, norm_ffn_g, peer_w_q, peer_sub_keys,
           peer_u, peer_v, norm_ple_g, w_ple_gate, w_ple):
    batch, seq, d = x_prompt.shape
    dec_batch, dec_seq, _ = x_sample.shape
    depth, n_phys, page, sb_heads, sb_dim = cache_k.shape
    _, _, ret_heads, dk, dv = state_ret.shape
    sb_width = sb_heads * sb_dim
    past_len = page_table.shape[1] * page
    chunk = LANE
    rows_pad = SUBLANE
    assert dec_seq <= rows_pad and dk == LANE and sb_width % LANE == 0
    qb_col0 = 3 * sb_width
    gate_col0 = qb_col0 + 2 * ret_heads * dk + 2 * ret_heads * dv

    xp = x_prompt.reshape(batch * seq, d)
    xs = jnp.pad(x_sample, ((0, 0), (0, rows_pad - dec_seq), (0, 0))).reshape(dec_batch * rows_pad, d)
    ns = dec_batch * rows_pad

    cs_p, sn_p = _rope_tables(jnp.arange(seq), dk)
    cs_s, sn_s = _rope_tables(past_len + jnp.arange(rows_pad), dk)
    cs_s, sn_s = jnp.tile(cs_s, (dec_batch, 1)), jnp.tile(sn_s, (dec_batch, 1))
    assert sb_dim & (sb_dim - 1) == 0 and sb_width == PROJ_TN
    gm = jnp.kron(jnp.eye(sb_heads, dtype=F32), jnp.full((sb_dim, sb_dim), 1.0 / sb_dim, F32)).astype(BF16)
    tri = jnp.concatenate([jnp.tril(jnp.ones((LANE, LANE), F32), -1), jnp.ones((LANE, LANE), F32)], axis=1).astype(BF16)
    dec_p = _ret_tables(ret_heads, chunk, float(chunk))
    dec_s = _ret_tables(ret_heads, chunk, float(dec_seq))

    tm = 256
    tb = 256
    kp, vp, sp, kn, vn, sn_out = [], [], [], [], [], []
    for i in range(depth):
        lw = dict(
            wa=w_a_up[i].astype(BF16), wb=w_b_up[i].astype(BF16), wo=w_out[i].astype(BF16),
            gf=norm_ffn_g[i][None, :], wq=peer_w_q[i].astype(BF16), keys=peer_sub_keys[i],
            u_tab=_pack_table(peer_u[i]), v_tab=_pack_table(peer_v[i]),
            gp=norm_ple_g[i][None, :], wg=w_ple_gate[i].astype(BF16), wp=w_ple[i].astype(BF16),
        )
        w_bf = w_in[i].astype(BF16)
        g_mix = norm_mix_g[i][None, :]
        qg = jnp.tile(q_norm_g[i], sb_heads)[None, :]
        kg = jnp.tile(k_norm_g[i], sb_heads)[None, :]
        rg = ret_norm_g[i][:, None, :]
        kb_scale = dk ** -0.5

        proj = _proj_call(xp, g_mix, w_bf, cs_p, sn_p, qg, kg, gm, tm=tm, pos_blocks=seq // tm, kb_scale=kb_scale)
        oa = _sb_prompt_call(proj, sb_bias[i], batch=batch, seq=seq, heads=sb_heads, head_dim=sb_dim)
        ob, st_p = _ret_call(proj, None, *dec_p, rg, batch=batch, n_chunks=seq // chunk, rows=chunk,
                             heads=ret_heads, dk=dk, dv=dv, col0=qb_col0)
        xp = _token_tail(xp, oa, ob, proj, p_prompt[i].reshape(batch * seq, -1), lw, tm=tm, tb=tb,
                         gate_col0=gate_col0)
        kp.append(proj[:, sb_width:2 * sb_width].reshape(batch, seq, sb_heads, sb_dim))
        vp.append(proj[:, 2 * sb_width:3 * sb_width].reshape(batch, seq, sb_heads, sb_dim))
        sp.append(st_p)

        proj_s = _proj_call(xs, g_mix, w_bf, cs_s, sn_s, qg, kg, gm, tm=ns, pos_blocks=1, kb_scale=kb_scale)
        bias_col = jnp.repeat(sb_bias[i], rows_pad)[:, None]
        oa_s = _sb_sample_call(page_table, proj_s, jnp.transpose(cache_k[i], (0, 2, 3, 1)),
                               jnp.transpose(cache_v[i], (0, 2, 3, 1)), tri, bias_col,
                               dec_batch=dec_batch, rows_pad=rows_pad, heads=sb_heads, head_dim=sb_dim)
        ob_s, st_s = _ret_call(proj_s, state_ret[i], *dec_s, rg, batch=dec_batch, n_chunks=1, rows=rows_pad,
                               heads=ret_heads, dk=dk, dv=dv, col0=qb_col0)
        p_s = jnp.pad(p_sample[i], ((0, 0), (0, rows_pad - dec_seq), (0, 0))).reshape(ns, -1)
        xs = _token_tail(xs, oa_s, ob_s, proj_s, p_s, lw, tm=ns, tb=ns, gate_col0=gate_col0)
        ka_s = proj_s[:, sb_width:2 * sb_width].reshape(dec_batch, rows_pad, sb_heads, sb_dim)[:, :dec_seq]
        va_s = proj_s[:, 2 * sb_width:3 * sb_width].reshape(dec_batch, rows_pad, sb_heads, sb_dim)[:, :dec_seq]
        kn.append(ka_s)
        vn.append(va_s)
        sn_out.append(st_s)

    y_prompt = xp.reshape(batch, seq, d)
    y_sample = xs.reshape(dec_batch, rows_pad, d)[:, :dec_seq]
    return (y_prompt, y_sample, jnp.stack(kp), jnp.stack(vp), jnp.stack(sp), jnp.stack(kn), jnp.stack(vn),
            jnp.stack(sn_out))
```

```python
import functools

import numpy as np
import jax
import jax.numpy as jnp
from jax import lax
from jax.experimental import pallas as pl
from jax.experimental.pallas import tpu as pltpu

F32 = jnp.float32
BF16 = jnp.bfloat16
I32 = jnp.int32
U32 = jnp.uint32
HIGHEST = lax.Precision.HIGHEST

RMS_EPS = 1e-6
ROPE_BASE = 10000.0
PEER_TOPK = 16
LANE = 128
SUBLANE = 8
VMEM_LIMIT = 56 * 1024 * 1024
NT_DIMS = (((1,), (1,)), ((), ()))
TN_DIMS = (((0,), (0,)), ((), ()))


def _cparams(sem, vmem=None):
    return pltpu.CompilerParams(dimension_semantics=sem, vmem_limit_bytes=vmem)


PROJ_TN = 512


def _proj_kernel(x_ref, g_ref, w_ref, cs_ref, sn_ref, qg_ref, kg_ref, gm_ref, o_ref, *, kb_scale):
    x = x_ref[...]
    ms = jnp.mean(x * x, axis=-1, keepdims=True)
    hn = (x * lax.rsqrt(ms + RMS_EPS) * g_ref[...]).astype(BF16)
    gm = gm_ref[...]
    for j in range(o_ref.shape[1] // PROJ_TN):
        cols = slice(j * PROJ_TN, (j + 1) * PROJ_TN)
        acc = jnp.dot(hn, w_ref[:, cols], preferred_element_type=F32)
        if j <= 1:
            sq = acc * acc
            hi = sq.astype(BF16)
            lo = (sq - hi.astype(F32)).astype(BF16)
            msq = jnp.dot(hi, gm, preferred_element_type=F32) + jnp.dot(lo, gm, preferred_element_type=F32)
            gain = qg_ref[...] if j == 0 else kg_ref[...]
            o_ref[:, cols] = acc * lax.rsqrt(msq + RMS_EPS) * gain
        elif j in (3, 4):
            cs = cs_ref[...]
            sn = sn_ref[...]
            for h in range(PROJ_TN // LANE):
                xh = acc[:, h * LANE:(h + 1) * LANE]
                r = pltpu.roll(xh, LANE // 2, 1)
                y = xh * cs + r * sn
                lo_col = j * PROJ_TN + h * LANE
                o_ref[:, lo_col:lo_col + LANE] = y * kb_scale if j == 4 else y
        else:
            o_ref[:, cols] = acc


def _proj_call(x2d, g, w_bf, cs, sn, qg, kg, gm, *, tm, pos_blocks, kb_scale):
    n, d = x2d.shape
    pw = w_bf.shape[1]
    const = lambda i: (0, 0)
    return pl.pallas_call(
        functools.partial(_proj_kernel, kb_scale=kb_scale),
        grid=(n // tm,),
        in_specs=[
            pl.BlockSpec((tm, d), lambda i: (i, 0)),
            pl.BlockSpec((1, d), const),
            pl.BlockSpec((d, pw), const, pipeline_mode=pl.Buffered(1)),
            pl.BlockSpec((tm, LANE), lambda i: (i % pos_blocks, 0)),
            pl.BlockSpec((tm, LANE), lambda i: (i % pos_blocks, 0)),
            pl.BlockSpec((1, PROJ_TN), const),
            pl.BlockSpec((1, PROJ_TN), const),
            pl.BlockSpec((PROJ_TN, PROJ_TN), const),
        ],
        out_specs=pl.BlockSpec((tm, pw), lambda i: (i, 0)),
        out_shape=jax.ShapeDtypeStruct((n, pw), F32),
        compiler_params=_cparams(("arbitrary",), VMEM_LIMIT),
    )(x2d, g, w_bf, cs, sn, qg, kg, gm)


def _sb_prompt_kernel(bias_ref, q_ref, k_ref, v_ref, tri_ref, o_ref, qs_ref, acc_ref, car_ref, *, scale):
    i = pl.program_id(1)
    tq = q_ref.shape[0]
    pairs = qs_ref.shape[0]
    lane = lax.broadcasted_iota(I32, (1, LANE), 1)
    hmask = [lane < LANE // 2, lane >= LANE // 2]
    first = lax.broadcasted_iota(I32, (1, 2 * LANE), 1) < LANE

    def split_heads(x):
        return jnp.concatenate([jnp.where(hmask[0], x, 0.0), jnp.where(hmask[1], x, 0.0)], axis=0).astype(BF16)

    for t in range(pairs):
        qs_ref[t] = (q_ref[:, t * LANE:(t + 1) * LANE] * scale).astype(BF16)
    tri2 = tri_ref[...]
    acc_ref[...] = jnp.zeros_like(acc_ref)
    car_ref[...] = jnp.zeros_like(car_ref)
    row = lax.broadcasted_iota(I32, (tq, 2 * LANE), 0)
    col = lax.broadcasted_iota(I32, (tq, 2 * LANE), 1) & (LANE - 1)
    causal2 = col < row

    def block(kbs, mask2):
        starts = [pl.multiple_of(kb * LANE, LANE) for kb in kbs]
        tiles = [slice(t * LANE, (t + 1) * LANE) for t in range(pairs)]
        units = [(n, t) for n in range(len(kbs)) for t in range(pairs)]
        z = {}
        for n, t in units:
            rk = split_heads(k_ref[pl.ds(starts[n], LANE), tiles[t]])
            bias2 = jnp.where(first, bias_ref[2 * t], bias_ref[2 * t + 1])
            z[n, t] = lax.dot_general(qs_ref[t], rk, NT_DIMS, preferred_element_type=F32) + bias2
        sp = {u: jnp.maximum(z[u], 0.0) + jnp.log(1.0 + jnp.exp(-jnp.abs(z[u]))) for u in units}
        spm = sp if mask2 is None else {u: jnp.where(mask2, sp[u], 0.0) for u in units}
        cs = {}
        for u in units:
            hi = spm[u].astype(BF16)
            lo = (spm[u] - hi.astype(F32)).astype(BF16)
            cs[u] = jnp.dot(hi, tri2, preferred_element_type=F32) + jnp.dot(lo, tri2, preferred_element_type=F32)
        a = {}
        for t in range(pairs):
            car = car_ref[t]
            for n in range(len(kbs)):
                at = jnp.exp(z[n, t] - sp[n, t] - cs[n, t] - car)
                a[n, t] = at if mask2 is None else jnp.where(mask2, at, 0.0)
                tot = [jnp.broadcast_to(jnp.sum(spm[n, t][:, h * LANE:(h + 1) * LANE], axis=1, keepdims=True),
                                        (tq, LANE)) for h in range(2)]
                car = car + jnp.concatenate(tot, axis=1)
            car_ref[t] = car
        for t in range(pairs):
            contrib = None
            for n in range(len(kbs)):
                rv = split_heads(v_ref[pl.ds(starts[n], LANE), tiles[t]])
                d = jnp.dot(a[n, t].astype(BF16), rv, preferred_element_type=F32)
                contrib = d if contrib is None else contrib + d
            acc_ref[:, tiles[t]] += contrib

    block([i], causal2)

    def body(jj, c):
        block([i - 1 - 2 * jj, i - 2 - 2 * jj], None)
        return c

    lax.fori_loop(0, lax.shift_right_logical(i, 1), body, 0)

    @pl.when((i & 1) == 1)
    def _():
        block([0], None)

    o_ref[...] = acc_ref[...]


def _sb_prompt_call(proj, bias, *, batch, seq, heads, head_dim):
    assert 2 * head_dim == LANE
    tq = LANE
    nq = seq // tq
    width = heads * head_dim
    pairs = heads // 2
    tri2 = jnp.kron(jnp.eye(2, dtype=F32), jnp.tril(jnp.ones((LANE, LANE), F32), -1)).astype(BF16)
    return pl.pallas_call(
        functools.partial(_sb_prompt_kernel, scale=head_dim ** -0.5),
        grid=(batch, nq),
        in_specs=[
            pl.BlockSpec(memory_space=pltpu.SMEM),
            pl.BlockSpec((tq, width), lambda b, i: (b * nq + i, 0)),
            pl.BlockSpec((seq, width), lambda b, i: (b, 1)),
            pl.BlockSpec((seq, width), lambda b, i: (b, 2)),
            pl.BlockSpec((2 * LANE, 2 * LANE), lambda b, i: (0, 0)),
        ],
        out_specs=pl.BlockSpec((tq, width), lambda b, i: (b * nq + i, 0)),
        out_shape=jax.ShapeDtypeStruct((batch * seq, width), F32),
        scratch_shapes=[
            pltpu.VMEM((pairs, tq, LANE), BF16),
            pltpu.VMEM((tq, width), F32),
            pltpu.VMEM((pairs, tq, 2 * LANE), F32),
        ],
        compiler_params=_cparams(("arbitrary", "arbitrary"), VMEM_LIMIT),
    )(bias, proj, proj, proj, tri2)


def _sb_sample_kernel(pt_ref, q_ref, kn_ref, vn_ref, *rest, scale, pages_per_step):
    kc_refs = rest[:pages_per_step]
    vc_refs = rest[pages_per_step:2 * pages_per_step]
    tri_ref, bias_ref, o_ref, qh_ref, nk_ref, nv_ref, acc_ref, car_ref = rest[2 * pages_per_step:]
    p = pl.program_id(1)
    heads, rq, hd = qh_ref.shape
    tri = tri_ref[...]
    bias = bias_ref[...]
    nn_dims = (((1,), (0,)), ((), ()))

    def blocks(load_k, load_v, n_blocks, mask, token_major):
        qk_dims, av_dims = (NT_DIMS, nn_dims) if token_major else (nn_dims, NT_DIMS)
        z = []
        for j in range(n_blocks):
            zj = [lax.dot_general(qh_ref[h], load_k(j, h).astype(BF16), qk_dims, preferred_element_type=F32)
                  for h in range(heads)]
            z.append(jnp.concatenate(zj, axis=0) + bias)
        sp = [jnp.maximum(zj, 0.0) + jnp.log(1.0 + jnp.exp(-jnp.abs(zj))) for zj in z]
        spm = sp if mask is None else [jnp.where(mask, s, 0.0) for s in sp]
        cs = []
        for s in spm:
            hi = s.astype(BF16)
            lo = (s - hi.astype(F32)).astype(BF16)
            cs.append(jnp.dot(hi, tri, preferred_element_type=F32) + jnp.dot(lo, tri, preferred_element_type=F32))
        car = car_ref[...]
        totals = [None] * heads
        for j in range(n_blocks):
            a = jnp.exp(z[j] - sp[j] - cs[j][:, :LANE] - car)
            if mask is not None:
                a = jnp.where(mask, a, 0.0)
            car = car + cs[j][:, LANE:]
            for h in range(heads):
                d = lax.dot_general(a[h * rq:(h + 1) * rq, :].astype(BF16), load_v(j, h).astype(BF16), av_dims,
                                    preferred_element_type=F32)
                totals[h] = d if totals[h] is None else totals[h] + d
        car_ref[...] = car
        for h in range(heads):
            acc_ref[h * rq:(h + 1) * rq, :] += totals[h]

    @pl.when(p == 0)
    def _():
        nk_ref[...] = jnp.zeros_like(nk_ref)
        nv_ref[...] = jnp.zeros_like(nv_ref)
        for h in range(heads):
            cols = slice(h * hd, (h + 1) * hd)
            qh_ref[h] = (q_ref[:, cols] * scale).astype(BF16)
            nk_ref[h, 0:rq, :] = kn_ref[:, cols]
            nv_ref[h, 0:rq, :] = vn_ref[:, cols]
        acc_ref[...] = jnp.zeros_like(acc_ref)
        car_ref[...] = jnp.zeros_like(car_ref)
        qrow = lax.broadcasted_iota(I32, (heads * rq, LANE), 0) & (rq - 1)
        col = lax.broadcasted_iota(I32, (heads * rq, LANE), 1)
        blocks(lambda j, h: nk_ref[h], lambda j, h: nv_ref[h], 1, col < qrow, True)

    blocks(lambda j, h: kc_refs[j][h], lambda j, h: vc_refs[j][h], pages_per_step, None, False)

    @pl.when(p == pl.num_programs(1) - 1)
    def _():
        for h in range(heads):
            o_ref[:, h * hd:(h + 1) * hd] = acc_ref[h * rq:(h + 1) * rq, :]


def _sb_sample_call(page_table, proj_s, cache_k3, cache_v3, tri, bias_col, *, dec_batch, rows_pad, heads, head_dim):
    n_pages = page_table.shape[1]
    width = heads * head_dim
    page = cache_k3.shape[3]
    rows = rows_pad * heads
    pps = max(c for c in (8, 4, 2, 1) if n_pages % c == 0)

    def page_spec(j):
        return pl.BlockSpec((None, heads, head_dim, page),
                            lambda b, p, pt: (pt[b * n_pages + n_pages - 1 - (p * pps + j)], 0, 0, 0))

    grid_spec = pltpu.PrefetchScalarGridSpec(
        num_scalar_prefetch=1,
        grid=(dec_batch, n_pages // pps),
        in_specs=[
            pl.BlockSpec((rows_pad, width), lambda b, p, pt: (b, 0)),
            pl.BlockSpec((rows_pad, width), lambda b, p, pt: (b, 1)),
            pl.BlockSpec((rows_pad, width), lambda b, p, pt: (b, 2)),
            *[page_spec(j) for j in range(pps)],
            *[page_spec(j) for j in range(pps)],
            pl.BlockSpec((LANE, 2 * LANE), lambda b, p, pt: (0, 0)),
            pl.BlockSpec((rows, 1), lambda b, p, pt: (0, 0)),
        ],
        out_specs=pl.BlockSpec((rows_pad, width), lambda b, p, pt: (b, 0)),
        scratch_shapes=[
            pltpu.VMEM((heads, rows_pad, head_dim), BF16),
            pltpu.VMEM((heads, page, head_dim), F32),
            pltpu.VMEM((heads, page, head_dim), F32),
            pltpu.VMEM((rows, head_dim), F32),
            pltpu.VMEM((rows, LANE), F32),
        ],
    )
    return pl.pallas_call(
        functools.partial(_sb_sample_kernel, scale=head_dim ** -0.5, pages_per_step=pps),
        grid_spec=grid_spec,
        out_shape=jax.ShapeDtypeStruct((dec_batch * rows_pad, width), F32),
        compiler_params=_cparams(("arbitrary", "arbitrary"), VMEM_LIMIT),
    )(page_table.reshape(-1), proj_s, proj_s, proj_s, *([cache_k3] * pps), *([cache_v3] * pps), tri, bias_col)


def _ret_kernel(q_ref, k_ref, v0_ref, v1_ref, g0_ref, g1_ref, st0_ref, dec_ref, qd_ref, kd_ref, sd_ref, rg_ref,
                ob_ref, st_ref, state_ref, qp_ref, kp_ref, vp_ref, *, use_init):
    c = pl.program_id(1)
    heads, dk, dv = state_ref.shape
    rows = q_ref.shape[0]
    chunk = dec_ref.shape[1]
    half = v0_ref.shape[1]

    @pl.when(c == 0)
    def _():
        if use_init:
            state_ref[...] = st0_ref[...]
        else:
            state_ref[...] = jnp.zeros_like(state_ref)

    if rows == chunk:
        q, k = q_ref[...], k_ref[...]
        v = jnp.concatenate([v0_ref[...], v1_ref[...]], axis=1)
    else:
        qp_ref[...] = jnp.zeros_like(qp_ref)
        kp_ref[...] = jnp.zeros_like(kp_ref)
        vp_ref[...] = jnp.zeros_like(vp_ref)
        qp_ref[0:rows, :] = q_ref[...]
        kp_ref[0:rows, :] = k_ref[...]
        vp_ref[0:rows, 0:half] = v0_ref[...]
        vp_ref[0:rows, half:2 * half] = v1_ref[...]
        q, k, v = qp_ref[...], kp_ref[...], vp_ref[...]
    gb = jnp.concatenate([g0_ref[...], g1_ref[...]], axis=1)

    qb = [q[:, h * dk:(h + 1) * dk].astype(BF16) for h in range(heads)]
    vb = [v[:, h * dv:(h + 1) * dv].astype(BF16) for h in range(heads)]
    st = [state_ref[h] for h in range(heads)]
    scores = [lax.dot_general(qb[h], k[:, h * dk:(h + 1) * dk].astype(BF16), NT_DIMS, preferred_element_type=F32)
              * dec_ref[h] for h in range(heads)]
    cross = [jnp.dot(qb[h], st[h].astype(BF16), preferred_element_type=F32) * qd_ref[h] for h in range(heads)]
    kv = [lax.dot_general((k[:, h * dk:(h + 1) * dk] * kd_ref[h]).astype(BF16), vb[h], TN_DIMS,
                          preferred_element_type=F32) for h in range(heads)]
    inner = [jnp.dot(scores[h].astype(BF16), vb[h], preferred_element_type=F32) for h in range(heads)]
    for h in range(heads):
        state_ref[h] = sd_ref[h] * st[h] + kv[h]
        o = (inner[h] + cross[h])[0:rows, :]
        ms = jnp.mean(o * o, axis=-1, keepdims=True)
        g = gb[:, h * dv:(h + 1) * dv]
        ob_ref[:, h * dv:(h + 1) * dv] = (o * lax.rsqrt(ms + RMS_EPS) * rg_ref[h]) * (g * jax.nn.sigmoid(g))

    @pl.when(c == pl.num_programs(1) - 1)
    def _():
        st_ref[...] = state_ref[...]


def _ret_call(proj, state0, dec, qd, kd, sd, rg, *, batch, n_chunks, rows, heads, dk, dv, col0):
    chunk = dec.shape[1]
    qw = heads * dk
    assert heads * dv == 2 * qw and col0 % qw == 0
    qc = col0 // qw
    use_init = state0 is not None
    if state0 is None:
        state0 = jnp.zeros((1, heads, dk, dv), F32)
        st_map = lambda b, c: (0, 0, 0, 0)
    else:
        st_map = lambda b, c: (b, 0, 0, 0)
    pad_rows = chunk if rows != chunk else SUBLANE
    row = lambda j: (lambda b, c: (b * n_chunks + c, qc + j))
    const = lambda b, c: (0, 0, 0)
    return pl.pallas_call(
        functools.partial(_ret_kernel, use_init=use_init),
        grid=(batch, n_chunks),
        in_specs=[
            *[pl.BlockSpec((rows, qw), row(j)) for j in range(6)],
            pl.BlockSpec((None, heads, dk, dv), st_map),
            pl.BlockSpec((heads, chunk, chunk), const),
            pl.BlockSpec((heads, chunk, 1), const),
            pl.BlockSpec((heads, chunk, 1), const),
            pl.BlockSpec((heads, 1, 1), const),
            pl.BlockSpec((heads, 1, dv), const),
        ],
        out_specs=[
            pl.BlockSpec((rows, heads * dv), lambda b, c: (b * n_chunks + c, 0)),
            pl.BlockSpec((None, heads, dk, dv), lambda b, c: (b, 0, 0, 0)),
        ],
        out_shape=[
            jax.ShapeDtypeStruct((batch * n_chunks * rows, heads * dv), F32),
            jax.ShapeDtypeStruct((batch, heads, dk, dv), F32),
        ],
        scratch_shapes=[
            pltpu.VMEM((heads, dk, dv), F32),
            pltpu.VMEM((pad_rows, qw), F32),
            pltpu.VMEM((pad_rows, qw), F32),
            pltpu.VMEM((pad_rows, 2 * qw), F32),
        ],
        compiler_params=_cparams(("arbitrary", "arbitrary"), VMEM_LIMIT),
    )(proj, proj, proj, proj, proj, proj, state0, dec, qd, kd, sd, rg)


def _merge_kernel(x_ref, oa_ref, ob_ref, ga0_ref, ga1_ref, gb0_ref, gb1_ref, wa_ref, wb_ref, wo_ref, gf_ref,
                  wq_ref, x1_ref, h2_ref, qp_ref):
    ya = jnp.dot(oa_ref[...].astype(BF16), wa_ref[...], preferred_element_type=F32)
    yb = jnp.dot(ob_ref[...].astype(BF16), wb_ref[...], preferred_element_type=F32)
    ga = jnp.concatenate([ga0_ref[...], ga1_ref[...]], axis=1)
    gb = jnp.concatenate([gb0_ref[...], gb1_ref[...]], axis=1)
    merged = jax.nn.sigmoid(ga) * ya + jax.nn.sigmoid(gb) * yb
    x1 = x_ref[...] + jnp.dot(merged.astype(BF16), wo_ref[...], preferred_element_type=F32)
    x1_ref[...] = x1
    ms = jnp.mean(x1 * x1, axis=-1, keepdims=True)
    h2 = x1 * lax.rsqrt(ms + RMS_EPS) * gf_ref[...]
    h2_ref[...] = h2
    qp_ref[...] = jnp.dot(h2.astype(BF16), wq_ref[...], preferred_element_type=F32)


def _merge_call(x2d, oa, ob, proj, wa, wb, wo, gf, wq, *, tm, gate_col0):
    n, d = x2d.shape
    half = d // 2
    g0 = gate_col0 // half
    row = lambda i: (i, 0)
    const = lambda i: (0, 0)
    return pl.pallas_call(
        _merge_kernel,
        grid=(n // tm,),
        in_specs=[
            pl.BlockSpec((tm, d), row),
            pl.BlockSpec((tm, oa.shape[1]), row),
            pl.BlockSpec((tm, ob.shape[1]), row),
            pl.BlockSpec((tm, half), lambda i: (i, g0)),
            pl.BlockSpec((tm, half), lambda i: (i, g0 + 1)),
            pl.BlockSpec((tm, half), lambda i: (i, g0 + 2)),
            pl.BlockSpec((tm, half), lambda i: (i, g0 + 3)),
            pl.BlockSpec(wa.shape, const),
            pl.BlockSpec(wb.shape, const),
            pl.BlockSpec(wo.shape, const),
            pl.BlockSpec((1, d), const),
            pl.BlockSpec(wq.shape, const),
        ],
        out_specs=[
            pl.BlockSpec((tm, d), row),
            pl.BlockSpec((tm, d), row),
            pl.BlockSpec((tm, wq.shape[1]), row),
        ],
        out_shape=[
            jax.ShapeDtypeStruct((n, d), F32),
            jax.ShapeDtypeStruct((n, d), F32),
            jax.ShapeDtypeStruct((n, wq.shape[1]), F32),
        ],
        compiler_params=_cparams(("arbitrary",), VMEM_LIMIT),
    )(x2d, oa, ob, proj, proj, proj, proj, wa, wb, wo, gf, wq)


def _topk_rows(s, codes, k):
    t = s.shape[1]
    out_row = lax.broadcasted_iota(I32, (k, t), 0)
    vals = jnp.zeros((k, t), F32)
    picked = jnp.zeros((k, t), F32)
    for r in range(k):
        m = jnp.max(s, axis=0, keepdims=True)
        am = jnp.min(jnp.where(s == m, codes, jnp.inf), axis=0, keepdims=True)
        vals = jnp.where(out_row == r, m, vals)
        picked = jnp.where(out_row == r, am, picked)
        s = jnp.where(codes == am, -jnp.inf, s)
    return vals, picked


def _select_rows(table, sel):
    out = jnp.zeros(sel.shape, table.dtype)
    for a in range(table.shape[0]):
        out = jnp.where(sel == a, table[a:a + 1, :], out)
    return out


def _pair_candidates(s1, s2):
    k, t = s1.shape
    blk = SUBLANE
    a_lo = lax.broadcasted_iota(I32, (blk, t), 0)
    vals, codes = [], []
    for a0 in range(0, k, blk):
        vals.append(s1[a0:a0 + blk, :] + s2[0:1, :])
        codes.append((a_lo + a0) * k)
    for b in range(1, blk):
        ok = a_lo < k // (b + 1)
        vals.append(jnp.where(ok, s1[0:blk, :] + s2[b:b + 1, :], -jnp.inf))
        codes.append(a_lo * k + b)
    for b0 in range(blk, k, blk):
        vals.append(s1[0:1, :] + s2[b0:b0 + blk, :])
        codes.append(a_lo + b0)
    return jnp.concatenate(vals, axis=0), jnp.concatenate(codes, axis=0).astype(F32)


def _peer_topk_kernel(qp_ref, keys_ref, e_ref, g_ref, *, n_keys):
    heads = keys_ref.shape[0]
    half = keys_ref.shape[3]
    k = PEER_TOPK
    t = qp_ref.shape[0]
    key_codes = lax.broadcasted_iota(I32, (n_keys, t), 0).astype(F32)
    e_rows, g_rows = [], []
    for h in range(heads):
        tops = []
        for c in range(2):
            col = (h * 2 + c) * half
            q = qp_ref[:, col:col + half].astype(BF16)
            st = lax.dot_general(keys_ref[h, c].astype(BF16), q, NT_DIMS, preferred_element_type=F32)
            tops.append(_topk_rows(st, key_codes, k))
        (s1, i1), (s2, i2) = tops
        cand, codes = _pair_candidates(s1, s2)
        top, flat = _topk_rows(cand, codes, k)
        flat = flat.astype(I32)
        ia = _select_rows(i1, flat // k)
        ib = _select_rows(i2, flat % k)
        e_rows.append(((ia * n_keys + ib) * EXPERT_ROWS).astype(I32))
        ex = jnp.exp(top - jnp.max(top, axis=0, keepdims=True))
        g_rows.append(ex / jnp.sum(ex, axis=0, keepdims=True))
    e_ref[...] = jnp.transpose(jnp.concatenate(e_rows, axis=0))
    g_ref[...] = jnp.transpose(jnp.concatenate(g_rows, axis=0))


def _peer_topk_call(qp, keys, *, tm):
    n = qp.shape[0]
    heads, _, n_keys, half = keys.shape
    slots = heads * PEER_TOPK
    return pl.pallas_call(
        functools.partial(_peer_topk_kernel, n_keys=n_keys),
        grid=(n // tm,),
        in_specs=[
            pl.BlockSpec((tm, qp.shape[1]), lambda i: (i, 0)),
            pl.BlockSpec(keys.shape, lambda i: (0, 0, 0, 0)),
        ],
        out_specs=[
            pl.BlockSpec((tm, slots), lambda i: (i, 0)),
            pl.BlockSpec((tm, slots), lambda i: (i, 0)),
        ],
        out_shape=[
            jax.ShapeDtypeStruct((n, slots), I32),
            jax.ShapeDtypeStruct((n, slots), F32),
        ],
        compiler_params=_cparams(("arbitrary",), VMEM_LIMIT),
    )(qp, keys)


EXPERT_ROWS = 4


def _pack_table(tab):
    n, d = tab.shape
    assert d == 2 * EXPERT_ROWS * LANE
    bits = lax.bitcast_convert_type(tab.astype(BF16), jnp.uint16).astype(U32)
    packed = bits[:, :d // 2] | (bits[:, d // 2:] << 16)
    return packed.reshape(n * EXPERT_ROWS, LANE)


PACKED_ROWS = 2 * EXPERT_ROWS
GROUPS_PER_ITER = 4


def _for_token_groups(n, group):
    per = max(c for c in range(1, GROUPS_PER_ITER + 1) if n % c == 0)

    def body(i, carry):
        for u in range(per):
            group(i * per + u, carry)
        return carry

    lax.fori_loop(0, n // per, body, 0)


IDX_WAYS = 4


def _split_index(e):
    n, slots = e.shape
    e3 = e.reshape(n, slots // IDX_WAYS, IDX_WAYS)
    return [e3[:, :, j].reshape(-1) for j in range(IDX_WAYS)]


def _gather_rows(tab_ref, idx_refs, t, slots):
    per_way = slots // IDX_WAYS
    rows = []
    for k in range(per_way):
        pos = t * per_way + k
        for j in range(IDX_WAYS):
            off = pl.multiple_of(idx_refs[j][pos], EXPERT_ROWS)
            rows.append(tab_ref[pl.ds(off, EXPERT_ROWS), :])
    return jnp.concatenate(rows, axis=0)


def _pack_bf16_pair(lo, hi):
    lo_bits = lax.bitcast_convert_type(lo.astype(BF16).astype(F32), U32) >> 16
    hi_bits = lax.bitcast_convert_type(hi.astype(BF16).astype(F32), U32) & jnp.uint32(0xFFFF0000)
    return lo_bits | hi_bits


def _peer_u_kernel(*refs):
    idx_refs = refs[:IDX_WAYS]
    h_ref, g_ref, tab_ref, fold_ref, w_ref = refs[IDX_WAYS:]
    tb, slots = g_ref.shape
    cols = slots * PACKED_ROWS
    sub = lax.broadcasted_iota(I32, (SUBLANE, cols), 0)
    lane = lax.broadcasted_iota(I32, (SUBLANE, cols), 1)
    diag = (lane & (PACKED_ROWS - 1)) == sub
    fold = fold_ref[...]

    def group(t8, carry):
        rows = pl.ds(pl.multiple_of(t8 * SUBLANE, SUBLANE), SUBLANE)
        h8 = h_ref[rows, :]
        half = h8.shape[1] // 2
        packed = _pack_bf16_pair(h8[:, :half], h8[:, half:])
        v8 = jnp.zeros((SUBLANE, cols), F32)
        for s in range(SUBLANE):
            t = t8 * SUBLANE + s
            words = _gather_rows(tab_ref, idx_refs, t, slots)
            hw = jnp.concatenate([packed[s:s + 1, j * LANE:(j + 1) * LANE] for j in range(EXPERT_ROWS)], axis=0)
            m = lax.dot_general(pltpu.bitcast(hw, BF16), pltpu.bitcast(words, BF16), NT_DIMS,
                                preferred_element_type=F32)
            v = jnp.sum(jnp.where(diag, m, 0.0), axis=0, keepdims=True)
            v8 = jnp.where(sub == s, v, v8)
        hi = v8.astype(BF16)
        lo = (v8 - hi.astype(F32)).astype(BF16)
        act = jnp.dot(hi, fold, preferred_element_type=F32) + jnp.dot(lo, fold, preferred_element_type=F32)
        w_ref[rows, :] = g_ref[rows, :] * jax.nn.gelu(act)
        return carry

    _for_token_groups(tb // SUBLANE, group)


def _index_specs(tb, slots):
    per_way = slots // IDX_WAYS
    return [pl.BlockSpec((tb * per_way,), lambda i: (i,), memory_space=pltpu.SMEM) for _ in range(IDX_WAYS)]


def _peer_u_call(idx_ways, h2, g, tab, *, tb):
    n, slots = g.shape
    d = h2.shape[1]
    fold = jnp.repeat(jnp.eye(slots, dtype=BF16), PACKED_ROWS, axis=0)
    return pl.pallas_call(
        _peer_u_kernel,
        grid=(n // tb,),
        in_specs=[
            *_index_specs(tb, slots),
            pl.BlockSpec((tb, d), lambda i: (i, 0)),
            pl.BlockSpec((tb, slots), lambda i: (i, 0)),
            pl.BlockSpec(tab.shape, lambda i: (0, 0), pipeline_mode=pl.Buffered(1)),
            pl.BlockSpec(fold.shape, lambda i: (0, 0)),
        ],
        out_specs=pl.BlockSpec((tb, slots), lambda i: (i, 0)),
        out_shape=jax.ShapeDtypeStruct((n, slots), F32),
        compiler_params=_cparams(("arbitrary",), VMEM_LIMIT),
    )(*idx_ways, h2, g, tab, fold)


def _peer_v_kernel(*refs):
    idx_refs = refs[:IDX_WAYS]
    w_ref, x_ref, tab_ref, expand_ref, o_ref = refs[IDX_WAYS:]
    tb, slots = w_ref.shape
    cols = slots * PACKED_ROWS
    sub = lax.broadcasted_iota(I32, (SUBLANE, cols), 0)
    j = lax.broadcasted_iota(I32, (SUBLANE, cols), 1) & (PACKED_ROWS - 1)
    probe = pltpu.bitcast(jnp.full((EXPERT_ROWS, LANE), 0x40003F80, U32), BF16).astype(F32)
    low_first = probe[0:1, 0:1] == 1.0
    half = jnp.where(low_first, j & 1, 1 - (j & 1))
    place = (half * EXPERT_ROWS + (j >> 1)) == sub
    expand = expand_ref[...]

    def group(t8, carry):
        rows8 = pl.ds(pl.multiple_of(t8 * SUBLANE, SUBLANE), SUBLANE)
        wrep = jnp.dot(w_ref[rows8, :].astype(BF16), expand, preferred_element_type=F32)
        outs = []
        for s in range(SUBLANE):
            t = t8 * SUBLANE + s
            words = _gather_rows(tab_ref, idx_refs, t, slots)
            lhs = jnp.where(place, wrep[s:s + 1, :], 0.0).astype(BF16)
            outs.append(jnp.dot(lhs, pltpu.bitcast(words, BF16), preferred_element_type=F32))
        peer = jnp.concatenate(
            [jnp.concatenate([outs[s][i:i + 1, :] for s in range(SUBLANE)], axis=0) for i in range(PACKED_ROWS)],
            axis=1)
        o_ref[rows8, :] = x_ref[rows8, :] + peer
        return carry

    _for_token_groups(tb // SUBLANE, group)


def _peer_v_call(idx_ways, w, x1, tab, *, tb):
    n, slots = w.shape
    d = x1.shape[1]
    assert d == PACKED_ROWS * LANE
    expand = jnp.repeat(jnp.eye(slots, dtype=BF16), PACKED_ROWS, axis=1)
    return pl.pallas_call(
        _peer_v_kernel,
        grid=(n // tb,),
        in_specs=[
            *_index_specs(tb, slots),
            pl.BlockSpec((tb, slots), lambda i: (i, 0)),
            pl.BlockSpec((tb, d), lambda i: (i, 0)),
            pl.BlockSpec(tab.shape, lambda i: (0, 0), pipeline_mode=pl.Buffered(1)),
            pl.BlockSpec(expand.shape, lambda i: (0, 0)),
        ],
        out_specs=pl.BlockSpec((tb, d), lambda i: (i, 0)),
        out_shape=jax.ShapeDtypeStruct((n, d), F32),
        compiler_params=_cparams(("arbitrary",), VMEM_LIMIT),
    )(*idx_ways, w, x1, tab, expand)


def _ple_kernel(x_ref, p_ref, g_ref, wg_ref, wp_ref, o_ref):
    x = x_ref[...]
    ms = jnp.mean(x * x, axis=-1, keepdims=True)
    hn = (x * lax.rsqrt(ms + RMS_EPS) * g_ref[...]).astype(BF16)
    gate = jax.nn.sigmoid(jnp.dot(hn, wg_ref[...], preferred_element_type=F32))
    pe = jnp.dot(p_ref[...].astype(BF16), wp_ref[...], preferred_element_type=F32)
    o_ref[...] = x + gate * pe


def _ple_call(x2, p, g, wg, wp, *, tm):
    n, d = x2.shape
    return pl.pallas_call(
        _ple_kernel,
        grid=(n // tm,),
        in_specs=[
            pl.BlockSpec((tm, d), lambda i: (i, 0)),
            pl.BlockSpec((tm, p.shape[1]), lambda i: (i, 0)),
            pl.BlockSpec((1, d), lambda i: (0, 0)),
            pl.BlockSpec(wg.shape, lambda i: (0, 0)),
            pl.BlockSpec(wp.shape, lambda i: (0, 0)),
        ],
        out_specs=pl.BlockSpec((tm, d), lambda i: (i, 0)),
        out_shape=jax.ShapeDtypeStruct((n, d), F32),
        compiler_params=_cparams(("arbitrary",), VMEM_LIMIT),
    )(x2, p, g, wg, wp)


def _rope_tables(pos, dim):
    half = dim // 2
    inv = ROPE_BASE ** (-jnp.arange(half, dtype=F32) / half)
    ang = pos.astype(F32)[:, None] * inv[None, :]
    cos, sin = jnp.cos(ang), jnp.sin(ang)
    return jnp.concatenate([cos, cos], axis=1), jnp.concatenate([-sin, sin], axis=1)


def _ret_tables(heads, chunk, n_valid):
    lg = jnp.log1p(-jnp.exp2(-5.0 - jnp.arange(heads, dtype=F32)))
    idx = jnp.arange(chunk, dtype=F32)
    diff = idx[:, None] - idx[None, :]
    causal = diff >= 0
    dec = jnp.where(causal[None], jnp.exp(jnp.where(causal, diff, 0.0)[None] * lg[:, None, None]), 0.0)
    qd = jnp.exp((idx[None, :] + 1.0) * lg[:, None])
    kd = jnp.exp((n_valid - 1.0 - idx)[None, :] * lg[:, None])
    kd = jnp.where(idx[None, :] < n_valid, kd, 0.0)
    sd = jnp.exp(n_valid * lg)
    return dec, qd[:, :, None], kd[:, :, None], sd[:, None, None]


def _token_tail(x2d, oa, ob, proj, p2d, lw, *, tm, tb, gate_col0):
    x1, h2, qp = _merge_call(x2d, oa, ob, proj, lw["wa"], lw["wb"], lw["wo"], lw["gf"], lw["wq"],
                             tm=tm, gate_col0=gate_col0)
    e, g = _peer_topk_call(qp, lw["keys"], tm=tm)
    idx_ways = _split_index(e)
    w = _peer_u_call(idx_ways, h2, g, lw["u_tab"], tb=tb)
    x2 = _peer_v_call(idx_ways, w, x1, lw["v_tab"], tb=tb)
    return _ple_call(x2, p2d, lw["gp"], lw["wg"], lw["wp"], tm=tm)


def kernel(x_prompt, x_sample, cache_k, cache_v, state_ret, page_table, p_prompt, p_sample, norm_mix_g, w_in,
           q_norm_g, k_norm_g, sb_bias, ret_norm_g, w_a_up, w_b_up, w_out, norm_ffn_g, peer_w_q, peer_sub_keys,
           peer_u, peer_v, norm_ple_g, w_ple_gate, w_ple):
    batch, seq, d = x_prompt.shape
    dec_batch, dec_seq, _ = x_sample.shape
    depth, n_phys, page, sb_heads, sb_dim = cache_k.shape
    _, _, ret_heads, dk, dv = state_ret.shape
    sb_width = sb_heads * sb_dim
    past_len = page_table.shape[1] * page
    chunk = LANE
    rows_pad = SUBLANE
    assert dec_seq <= rows_pad and dk == LANE and sb_width % LANE == 0
    qb_col0 = 3 * sb_width
    gate_col0 = qb_col0 + 2 * ret_heads * dk + 2 * ret_heads * dv

    xp = x_prompt.reshape(batch * seq, d)
    xs = jnp.pad(x_sample, ((0, 0), (0, rows_pad - dec_seq), (0, 0))).reshape(dec_batch * rows_pad, d)
    ns = dec_batch * rows_pad

    cs_p, sn_p = _rope_tables(jnp.arange(seq), dk)
    cs_s, sn_s = _rope_tables(past_len + jnp.arange(rows_pad), dk)
    cs_s, sn_s = jnp.tile(cs_s, (dec_batch, 1)), jnp.tile(sn_s, (dec_batch, 1))
    assert sb_dim & (sb_dim - 1) == 0 and sb_width == PROJ_TN
    gm = jnp.kron(jnp.eye(sb_heads, dtype=F32), jnp.full((sb_dim, sb_dim), 1.0 / sb_dim, F32)).astype(BF16)
    tri = jnp.concatenate([jnp.tril(jnp.ones((LANE, LANE), F32), -1), jnp.ones((LANE, LANE), F32)], axis=1).astype(BF16)
    dec_p = _ret_tables(ret_heads, chunk, float(chunk))
    dec_s = _ret_tables(ret_heads, chunk, float(dec_seq))

    tm = 256
    tb = 256
    kp, vp, sp, kn, vn, sn_out = [], [], [], [], [], []
    for i in range(depth):
        lw = dict(
            wa=w_a_up[i].astype(BF16), wb=w_b_up[i].astype(BF16), wo=w_out[i].astype(BF16),
            gf=norm_ffn_g[i][None, :], wq=peer_w_q[i].astype(BF16), keys=peer_sub_keys[i],
            u_tab=_pack_table(peer_u[i]), v_tab=_pack_table(peer_v[i]),
            gp=norm_ple_g[i][None, :], wg=w_ple_gate[i].astype(BF16), wp=w_ple[i].astype(BF16),
        )
        w_bf = w_in[i].astype(BF16)
        g_mix = norm_mix_g[i][None, :]
        qg = jnp.tile(q_norm_g[i], sb_heads)[None, :]
        kg = jnp.tile(k_norm_g[i], sb_heads)[None, :]
        rg = ret_norm_g[i][:, None, :]
        kb_scale = dk ** -0.5

        proj = _proj_call(xp, g_mix, w_bf, cs_p, sn_p, qg, kg, gm, tm=tm, pos_blocks=seq // tm, kb_scale=kb_scale)
        oa = _sb_prompt_call(proj, sb_bias[i], batch=batch, seq=seq, heads=sb_heads, head_dim=sb_dim)
        ob, st_p = _ret_call(proj, None, *dec_p, rg, batch=batch, n_chunks=seq // chunk, rows=chunk,
                             heads=ret_heads, dk=dk, dv=dv, col0=qb_col0)
        xp = _token_tail(xp, oa, ob, proj, p_prompt[i].reshape(batch * seq, -1), lw, tm=tm, tb=tb,
                         gate_col0=gate_col0)
        kp.append(proj[:, sb_width:2 * sb_width].reshape(batch, seq, sb_heads, sb_dim))
        vp.append(proj[:, 2 * sb_width:3 * sb_width].reshape(batch, seq, sb_heads, sb_dim))
        sp.append(st_p)

        proj_s = _proj_call(xs, g_mix, w_bf, cs_s, sn_s, qg, kg, gm, tm=ns, pos_blocks=1, kb_scale=kb_scale)
        bias_col = jnp.repeat(sb_bias[i], rows_pad)[:, None]
        oa_s = _sb_sample_call(page_table, proj_s, jnp.transpose(cache_k[i], (0, 2, 3, 1)),
                               jnp.transpose(cache_v[i], (0, 2, 3, 1)), tri, bias_col,
                               dec_batch=dec_batch, rows_pad=rows_pad, heads=sb_heads, head_dim=sb_dim)
        ob_s, st_s = _ret_call(proj_s, state_ret[i], *dec_s, rg, batch=dec_batch, n_chunks=1, rows=rows_pad,
                               heads=ret_heads, dk=dk, dv=dv, col0=qb_col0)
        p_s = jnp.pad(p_sample[i], ((0, 0), (0, rows_pad - dec_seq), (0, 0))).reshape(ns, -1)
        xs = _token_tail(xs, oa_s, ob_s, proj_s, p_s, lw, tm=ns, tb=ns, gate_col0=gate_col0)
        ka_s = proj_s[:, sb_width:2 * sb_width].reshape(dec_batch, rows_pad, sb_heads, sb_dim)[:, :dec_seq]
        va_s = proj_s[:, 2 * sb_width:3 * sb_width].reshape(dec_batch, rows_pad, sb_heads, sb_dim)[:, :dec_seq]
        kn.append(ka_s)
        vn.append(va_s)
        sn_out.append(st_s)

    y_prompt = xp.reshape(batch, seq, d)
    y_sample = xs.reshape(dec_batch, rows_pad, d)[:, :dec_seq]
    return (y_prompt, y_sample, jnp.stack(kp), jnp.stack(vp), jnp.stack(sp), jnp.stack(kn), jnp.stack(vn),
            jnp.stack(sn_out))
```

```python
import functools

import jax
import jax.numpy as jnp
from jax import lax
from jax.experimental import pallas as pl
from jax.experimental.pallas import tpu as pltpu

F32 = jnp.float32
BF16 = jnp.bfloat16
I32 = jnp.int32
U32 = jnp.uint32

RMS_EPS = 1e-6
ROPE_BASE = 10000.0
PEER_TOPK = 16
LANE = 128
SUBLANE = 8
VMEM_BYTES_V7X = 64 * 1024 * 1024
VMEM_LIMIT = VMEM_BYTES_V7X - 8 * 1024 * 1024
ROW_TILE = 256
PEER_TOKEN_BLOCK = 256
NT_DIMS = (((1,), (1,)), ((), ()))
TN_DIMS = (((0,), (0,)), ((), ()))


def _cparams(sem, vmem=None):
    return pltpu.CompilerParams(dimension_semantics=sem, vmem_limit_bytes=vmem)


PROJ_TN = 512


def _proj_kernel(x_ref, g_ref, w_ref, cs_ref, sn_ref, qg_ref, kg_ref, gm_ref, o_ref, *, kb_scale):
    x = x_ref[...]
    ms = jnp.mean(x * x, axis=-1, keepdims=True)
    hn = (x * lax.rsqrt(ms + RMS_EPS) * g_ref[...]).astype(BF16)
    gm = gm_ref[...]
    for j in range(o_ref.shape[1] // PROJ_TN):
        cols = slice(j * PROJ_TN, (j + 1) * PROJ_TN)
        acc = jnp.dot(hn, w_ref[:, cols], preferred_element_type=F32)
        if j <= 1:
            sq = acc * acc
            hi = sq.astype(BF16)
            lo = (sq - hi.astype(F32)).astype(BF16)
            msq = jnp.dot(hi, gm, preferred_element_type=F32) + jnp.dot(lo, gm, preferred_element_type=F32)
            gain = qg_ref[...] if j == 0 else kg_ref[...]
            o_ref[:, cols] = acc * lax.rsqrt(msq + RMS_EPS) * gain
        elif j in (3, 4):
            cs = cs_ref[...]
            sn = sn_ref[...]
            for h in range(PROJ_TN // LANE):
                xh = acc[:, h * LANE:(h + 1) * LANE]
                r = pltpu.roll(xh, LANE // 2, 1)
                y = xh * cs + r * sn
                lo_col = j * PROJ_TN + h * LANE
                o_ref[:, lo_col:lo_col + LANE] = y * kb_scale if j == 4 else y
        else:
            o_ref[:, cols] = acc


def _proj_call(x2d, g, w_bf, cs, sn, qg, kg, gm, *, tm, pos_blocks, kb_scale):
    n, d = x2d.shape
    pw = w_bf.shape[1]
    const = lambda i: (0, 0)
    return pl.pallas_call(
        functools.partial(_proj_kernel, kb_scale=kb_scale),
        grid=(n // tm,),
        in_specs=[
            pl.BlockSpec((tm, d), lambda i: (i, 0)),
            pl.BlockSpec((1, d), const),
            pl.BlockSpec((d, pw), const, pipeline_mode=pl.Buffered(1)),
            pl.BlockSpec((tm, LANE), lambda i: (i % pos_blocks, 0)),
            pl.BlockSpec((tm, LANE), lambda i: (i % pos_blocks, 0)),
            pl.BlockSpec((1, PROJ_TN), const),
            pl.BlockSpec((1, PROJ_TN), const),
            pl.BlockSpec((PROJ_TN, PROJ_TN), const),
        ],
        out_specs=pl.BlockSpec((tm, pw), lambda i: (i, 0)),
        out_shape=jax.ShapeDtypeStruct((n, pw), F32),
        compiler_params=_cparams(("arbitrary",), VMEM_LIMIT),
    )(x2d, g, w_bf, cs, sn, qg, kg, gm)


def _sb_prompt_kernel(bias_ref, q_ref, k_ref, v_ref, tri_ref, o_ref, qs_ref, acc_ref, car_ref, *, scale):
    i = pl.program_id(1)
    tq = q_ref.shape[0]
    pairs = qs_ref.shape[0]
    lane = lax.broadcasted_iota(I32, (1, LANE), 1)
    hmask = [lane < LANE // 2, lane >= LANE // 2]
    first = lax.broadcasted_iota(I32, (1, 2 * LANE), 1) < LANE

    def split_heads(x):
        return jnp.concatenate([jnp.where(hmask[0], x, 0.0), jnp.where(hmask[1], x, 0.0)], axis=0).astype(BF16)

    for t in range(pairs):
        qs_ref[t] = (q_ref[:, t * LANE:(t + 1) * LANE] * scale).astype(BF16)
    tri2 = tri_ref[...]
    acc_ref[...] = jnp.zeros_like(acc_ref)
    car_ref[...] = jnp.zeros_like(car_ref)
    row = lax.broadcasted_iota(I32, (tq, 2 * LANE), 0)
    col = lax.broadcasted_iota(I32, (tq, 2 * LANE), 1) & (LANE - 1)
    causal2 = col < row

    def block(kbs, mask2):
        starts = [pl.multiple_of(kb * LANE, LANE) for kb in kbs]
        tiles = [slice(t * LANE, (t + 1) * LANE) for t in range(pairs)]
        units = [(n, t) for n in range(len(kbs)) for t in range(pairs)]
        z = {}
        for n, t in units:
            rk = split_heads(k_ref[pl.ds(starts[n], LANE), tiles[t]])
            bias2 = jnp.where(first, bias_ref[2 * t], bias_ref[2 * t + 1])
            z[n, t] = lax.dot_general(qs_ref[t], rk, NT_DIMS, preferred_element_type=F32) + bias2
        sp = {u: jnp.maximum(z[u], 0.0) + jnp.log(1.0 + jnp.exp(-jnp.abs(z[u]))) for u in units}
        spm = sp if mask2 is None else {u: jnp.where(mask2, sp[u], 0.0) for u in units}
        cs = {}
        for u in units:
            hi = spm[u].astype(BF16)
            lo = (spm[u] - hi.astype(F32)).astype(BF16)
            cs[u] = jnp.dot(hi, tri2, preferred_element_type=F32) + jnp.dot(lo, tri2, preferred_element_type=F32)
        a = {}
        for t in range(pairs):
            car = car_ref[t]
            for n in range(len(kbs)):
                at = jnp.exp(z[n, t] - sp[n, t] - cs[n, t] - car)
                a[n, t] = at if mask2 is None else jnp.where(mask2, at, 0.0)
                tot = [jnp.broadcast_to(jnp.sum(spm[n, t][:, h * LANE:(h + 1) * LANE], axis=1, keepdims=True),
                                        (tq, LANE)) for h in range(2)]
                car = car + jnp.concatenate(tot, axis=1)
            car_ref[t] = car
        for t in range(pairs):
            contrib = None
            for n in range(len(kbs)):
                rv = split_heads(v_ref[pl.ds(starts[n], LANE), tiles[t]])
                d = jnp.dot(a[n, t].astype(BF16), rv, preferred_element_type=F32)
                contrib = d if contrib is None else contrib + d
            acc_ref[:, tiles[t]] += contrib

    block([i], causal2)

    def body(jj, c):
        block([i - 1 - 2 * jj, i - 2 - 2 * jj], None)
        return c

    lax.fori_loop(0, lax.shift_right_logical(i, 1), body, 0)

    @pl.when((i & 1) == 1)
    def _():
        block([0], None)

    o_ref[...] = acc_ref[...]


def _sb_prompt_call(proj, bias, *, batch, seq, heads, head_dim):
    assert 2 * head_dim == LANE
    tq = LANE
    nq = seq // tq
    width = heads * head_dim
    pairs = heads // 2
    tri2 = jnp.kron(jnp.eye(2, dtype=F32), jnp.tril(jnp.ones((LANE, LANE), F32), -1)).astype(BF16)
    return pl.pallas_call(
        functools.partial(_sb_prompt_kernel, scale=head_dim ** -0.5),
        grid=(batch, nq),
        in_specs=[
            pl.BlockSpec(memory_space=pltpu.SMEM),
            pl.BlockSpec((tq, width), lambda b, i: (b * nq + i, 0)),
            pl.BlockSpec((seq, width), lambda b, i: (b, 1)),
            pl.BlockSpec((seq, width), lambda b, i: (b, 2)),
            pl.BlockSpec((2 * LANE, 2 * LANE), lambda b, i: (0, 0)),
        ],
        out_specs=pl.BlockSpec((tq, width), lambda b, i: (b * nq + i, 0)),
        out_shape=jax.ShapeDtypeStruct((batch * seq, width), F32),
        scratch_shapes=[
            pltpu.VMEM((pairs, tq, LANE), BF16),
            pltpu.VMEM((tq, width), F32),
            pltpu.VMEM((pairs, tq, 2 * LANE), F32),
        ],
        compiler_params=_cparams(("arbitrary", "arbitrary"), VMEM_LIMIT),
    )(bias, proj, proj, proj, tri2)


def _sb_sample_kernel(pt_ref, q_ref, kn_ref, vn_ref, *rest, scale, pages_per_step):
    kc_refs = rest[:pages_per_step]
    vc_refs = rest[pages_per_step:2 * pages_per_step]
    tri_ref, bias_ref, o_ref, qh_ref, nk_ref, nv_ref, acc_ref, car_ref = rest[2 * pages_per_step:]
    p = pl.program_id(1)
    heads, rq, hd = qh_ref.shape
    tri = tri_ref[...]
    bias = bias_ref[...]
    nn_dims = (((1,), (0,)), ((), ()))

    def blocks(load_k, load_v, n_blocks, mask, token_major):
        qk_dims, av_dims = (NT_DIMS, nn_dims) if token_major else (nn_dims, NT_DIMS)
        z = []
        for j in range(n_blocks):
            zj = [lax.dot_general(qh_ref[h], load_k(j, h).astype(BF16), qk_dims, preferred_element_type=F32)
                  for h in range(heads)]
            z.append(jnp.concatenate(zj, axis=0) + bias)
        sp = [jnp.maximum(zj, 0.0) + jnp.log(1.0 + jnp.exp(-jnp.abs(zj))) for zj in z]
        spm = sp if mask is None else [jnp.where(mask, s, 0.0) for s in sp]
        cs = []
        for s in spm:
            hi = s.astype(BF16)
            lo = (s - hi.astype(F32)).astype(BF16)
            cs.append(jnp.dot(hi, tri, preferred_element_type=F32) + jnp.dot(lo, tri, preferred_element_type=F32))
        car = car_ref[...]
        totals = [None] * heads
        for j in range(n_blocks):
            a = jnp.exp(z[j] - sp[j] - cs[j][:, :LANE] - car)
            if mask is not None:
                a = jnp.where(mask, a, 0.0)
            car = car + cs[j][:, LANE:]
            for h in range(heads):
                d = lax.dot_general(a[h * rq:(h + 1) * rq, :].astype(BF16), load_v(j, h).astype(BF16), av_dims,
                                    preferred_element_type=F32)
                totals[h] = d if totals[h] is None else totals[h] + d
        car_ref[...] = car
        for h in range(heads):
            acc_ref[h * rq:(h + 1) * rq, :] += totals[h]

    @pl.when(p == 0)
    def _():
        nk_ref[...] = jnp.zeros_like(nk_ref)
        nv_ref[...] = jnp.zeros_like(nv_ref)
        for h in range(heads):
            cols = slice(h * hd, (h + 1) * hd)
            qh_ref[h] = (q_ref[:, cols] * scale).astype(BF16)
            nk_ref[h, 0:rq, :] = kn_ref[:, cols]
            nv_ref[h, 0:rq, :] = vn_ref[:, cols]
        acc_ref[...] = jnp.zeros_like(acc_ref)
        car_ref[...] = jnp.zeros_like(car_ref)
        qrow = lax.broadcasted_iota(I32, (heads * rq, LANE), 0) & (rq - 1)
        col = lax.broadcasted_iota(I32, (heads * rq, LANE), 1)
        blocks(lambda j, h: nk_ref[h], lambda j, h: nv_ref[h], 1, col < qrow, True)

    blocks(lambda j, h: kc_refs[j][h], lambda j, h: vc_refs[j][h], pages_per_step, None, False)

    @pl.when(p == pl.num_programs(1) - 1)
    def _():
        for h in range(heads):
            o_ref[:, h * hd:(h + 1) * hd] = acc_ref[h * rq:(h + 1) * rq, :]


def _sb_sample_call(page_table, proj_s, cache_k3, cache_v3, tri, bias_col, *, dec_batch, rows_pad, heads, head_dim):
    n_pages = page_table.shape[1]
    width = heads * head_dim
    page = cache_k3.shape[3]
    rows = rows_pad * heads
    pps = max(c for c in (16, 8, 4, 2, 1) if n_pages % c == 0)

    def page_spec(j):
        return pl.BlockSpec((None, heads, head_dim, page),
                            lambda b, p, pt: (pt[b * n_pages + n_pages - 1 - (p * pps + j)], 0, 0, 0))

    grid_spec = pltpu.PrefetchScalarGridSpec(
        num_scalar_prefetch=1,
        grid=(dec_batch, n_pages // pps),
        in_specs=[
            pl.BlockSpec((rows_pad, width), lambda b, p, pt: (b, 0)),
            pl.BlockSpec((rows_pad, width), lambda b, p, pt: (b, 1)),
            pl.BlockSpec((rows_pad, width), lambda b, p, pt: (b, 2)),
            *[page_spec(j) for j in range(pps)],
            *[page_spec(j) for j in range(pps)],
            pl.BlockSpec((LANE, 2 * LANE), lambda b, p, pt: (0, 0)),
            pl.BlockSpec((rows, 1), lambda b, p, pt: (0, 0)),
        ],
        out_specs=pl.BlockSpec((rows_pad, width), lambda b, p, pt: (b, 0)),
        scratch_shapes=[
            pltpu.VMEM((heads, rows_pad, head_dim), BF16),
            pltpu.VMEM((heads, page, head_dim), F32),
            pltpu.VMEM((heads, page, head_dim), F32),
            pltpu.VMEM((rows, head_dim), F32),
            pltpu.VMEM((rows, LANE), F32),
        ],
    )
    return pl.pallas_call(
        functools.partial(_sb_sample_kernel, scale=head_dim ** -0.5, pages_per_step=pps),
        grid_spec=grid_spec,
        out_shape=jax.ShapeDtypeStruct((dec_batch * rows_pad, width), F32),
        compiler_params=_cparams(("arbitrary", "arbitrary"), VMEM_LIMIT),
    )(page_table.reshape(-1), proj_s, proj_s, proj_s, *([cache_k3] * pps), *([cache_v3] * pps), tri, bias_col)


def _ret_kernel(q_ref, k_ref, v0_ref, v1_ref, g0_ref, g1_ref, st0_ref, dec_ref, qd_ref, kd_ref, sd_ref, rg_ref,
                ob_ref, st_ref, state_ref, qp_ref, kp_ref, vp_ref, *, use_init):
    c = pl.program_id(1)
    heads, dk, dv = state_ref.shape
    rows = q_ref.shape[0]
    chunk = dec_ref.shape[1]
    half = v0_ref.shape[1]

    @pl.when(c == 0)
    def _():
        if use_init:
            state_ref[...] = st0_ref[...]
        else:
            state_ref[...] = jnp.zeros_like(state_ref)

    if rows == chunk:
        q, k = q_ref[...], k_ref[...]
        v = jnp.concatenate([v0_ref[...], v1_ref[...]], axis=1)
    else:
        qp_ref[...] = jnp.zeros_like(qp_ref)
        kp_ref[...] = jnp.zeros_like(kp_ref)
        vp_ref[...] = jnp.zeros_like(vp_ref)
        qp_ref[0:rows, :] = q_ref[...]
        kp_ref[0:rows, :] = k_ref[...]
        vp_ref[0:rows, 0:half] = v0_ref[...]
        vp_ref[0:rows, half:2 * half] = v1_ref[...]
        q, k, v = qp_ref[...], kp_ref[...], vp_ref[...]
    gb = jnp.concatenate([g0_ref[...], g1_ref[...]], axis=1)

    qb = [q[:, h * dk:(h + 1) * dk].astype(BF16) for h in range(heads)]
    vb = [v[:, h * dv:(h + 1) * dv].astype(BF16) for h in range(heads)]
    st = [state_ref[h] for h in range(heads)]
    scores = [lax.dot_general(qb[h], k[:, h * dk:(h + 1) * dk].astype(BF16), NT_DIMS, preferred_element_type=F32)
              * dec_ref[h] for h in range(heads)]
    cross = [jnp.dot(qb[h], st[h].astype(BF16), preferred_element_type=F32) * qd_ref[h] for h in range(heads)]
    kv = [lax.dot_general((k[:, h * dk:(h + 1) * dk] * kd_ref[h]).astype(BF16), vb[h], TN_DIMS,
                          preferred_element_type=F32) for h in range(heads)]
    inner = [jnp.dot(scores[h].astype(BF16), vb[h], preferred_element_type=F32) for h in range(heads)]
    for h in range(heads):
        state_ref[h] = sd_ref[h] * st[h] + kv[h]
        o = (inner[h] + cross[h])[0:rows, :]
        ms = jnp.mean(o * o, axis=-1, keepdims=True)
        g = gb[:, h * dv:(h + 1) * dv]
        ob_ref[:, h * dv:(h + 1) * dv] = (o * lax.rsqrt(ms + RMS_EPS) * rg_ref[h]) * (g * jax.nn.sigmoid(g))

    @pl.when(c == pl.num_programs(1) - 1)
    def _():
        st_ref[...] = state_ref[...]


def _ret_call(proj, state0, dec, qd, kd, sd, rg, *, batch, n_chunks, rows, heads, dk, dv, col0):
    chunk = dec.shape[1]
    qw = heads * dk
    assert heads * dv == 2 * qw and col0 % qw == 0
    qc = col0 // qw
    use_init = state0 is not None
    if state0 is None:
        state0 = jnp.zeros((1, heads, dk, dv), F32)
        st_map = lambda b, c: (0, 0, 0, 0)
    else:
        st_map = lambda b, c: (b, 0, 0, 0)
    pad_rows = chunk if rows != chunk else SUBLANE
    row = lambda j: (lambda b, c: (b * n_chunks + c, qc + j))
    const = lambda b, c: (0, 0, 0)
    return pl.pallas_call(
        functools.partial(_ret_kernel, use_init=use_init),
        grid=(batch, n_chunks),
        in_specs=[
            *[pl.BlockSpec((rows, qw), row(j)) for j in range(6)],
            pl.BlockSpec((None, heads, dk, dv), st_map),
            pl.BlockSpec((heads, chunk, chunk), const),
            pl.BlockSpec((heads, chunk, 1), const),
            pl.BlockSpec((heads, chunk, 1), const),
            pl.BlockSpec((heads, 1, 1), const),
            pl.BlockSpec((heads, 1, dv), const),
        ],
        out_specs=[
            pl.BlockSpec((rows, heads * dv), lambda b, c: (b * n_chunks + c, 0)),
            pl.BlockSpec((None, heads, dk, dv), lambda b, c: (b, 0, 0, 0)),
        ],
        out_shape=[
            jax.ShapeDtypeStruct((batch * n_chunks * rows, heads * dv), F32),
            jax.ShapeDtypeStruct((batch, heads, dk, dv), F32),
        ],
        scratch_shapes=[
            pltpu.VMEM((heads, dk, dv), F32),
            pltpu.VMEM((pad_rows, qw), F32),
            pltpu.VMEM((pad_rows, qw), F32),
            pltpu.VMEM((pad_rows, 2 * qw), F32),
        ],
        compiler_params=_cparams(("arbitrary", "arbitrary"), VMEM_LIMIT),
    )(proj, proj, proj, proj, proj, proj, state0, dec, qd, kd, sd, rg)


def _merge_kernel(x_ref, oa_ref, ob_ref, ga0_ref, ga1_ref, gb0_ref, gb1_ref, wa_ref, wb_ref, wo_ref, gf_ref,
                  wq_ref, x1_ref, h2_ref, qp_ref):
    ya = jnp.dot(oa_ref[...].astype(BF16), wa_ref[...], preferred_element_type=F32)
    yb = jnp.dot(ob_ref[...].astype(BF16), wb_ref[...], preferred_element_type=F32)
    ga = jnp.concatenate([ga0_ref[...], ga1_ref[...]], axis=1)
    gb = jnp.concatenate([gb0_ref[...], gb1_ref[...]], axis=1)
    merged = jax.nn.sigmoid(ga) * ya + jax.nn.sigmoid(gb) * yb
    x1 = x_ref[...] + jnp.dot(merged.astype(BF16), wo_ref[...], preferred_element_type=F32)
    x1_ref[...] = x1
    ms = jnp.mean(x1 * x1, axis=-1, keepdims=True)
    h2 = x1 * lax.rsqrt(ms + RMS_EPS) * gf_ref[...]
    h2_ref[...] = h2
    qp_ref[...] = jnp.dot(h2.astype(BF16), wq_ref[...], preferred_element_type=F32)


def _merge_call(x2d, oa, ob, proj, wa, wb, wo, gf, wq, *, tm, gate_col0):
    n, d = x2d.shape
    half = d // 2
    g0 = gate_col0 // half
    row = lambda i: (i, 0)
    const = lambda i: (0, 0)
    return pl.pallas_call(
        _merge_kernel,
        grid=(n // tm,),
        in_specs=[
            pl.BlockSpec((tm, d), row),
            pl.BlockSpec((tm, oa.shape[1]), row),
            pl.BlockSpec((tm, ob.shape[1]), row),
            pl.BlockSpec((tm, half), lambda i: (i, g0)),
            pl.BlockSpec((tm, half), lambda i: (i, g0 + 1)),
            pl.BlockSpec((tm, half), lambda i: (i, g0 + 2)),
            pl.BlockSpec((tm, half), lambda i: (i, g0 + 3)),
            pl.BlockSpec(wa.shape, const),
            pl.BlockSpec(wb.shape, const),
            pl.BlockSpec(wo.shape, const),
            pl.BlockSpec((1, d), const),
            pl.BlockSpec(wq.shape, const),
        ],
        out_specs=[
            pl.BlockSpec((tm, d), row),
            pl.BlockSpec((tm, d), row),
            pl.BlockSpec((tm, wq.shape[1]), row),
        ],
        out_shape=[
            jax.ShapeDtypeStruct((n, d), F32),
            jax.ShapeDtypeStruct((n, d), F32),
            jax.ShapeDtypeStruct((n, wq.shape[1]), F32),
        ],
        compiler_params=_cparams(("arbitrary",), VMEM_LIMIT),
    )(x2d, oa, ob, proj, proj, proj, proj, wa, wb, wo, gf, wq)


def _topk_rows(s, codes, k):
    t = s.shape[1]
    out_row = lax.broadcasted_iota(I32, (k, t), 0)
    vals = jnp.zeros((k, t), F32)
    picked = jnp.zeros((k, t), F32)
    for r in range(k):
        m = jnp.max(s, axis=0, keepdims=True)
        am = jnp.min(jnp.where(s == m, codes, jnp.inf), axis=0, keepdims=True)
        vals = jnp.where(out_row == r, m, vals)
        picked = jnp.where(out_row == r, am, picked)
        s = jnp.where(codes == am, -jnp.inf, s)
    return vals, picked


def _select_rows(table, sel):
    out = jnp.zeros(sel.shape, table.dtype)
    for a in range(table.shape[0]):
        out = jnp.where(sel == a, table[a:a + 1, :], out)
    return out


def _pair_candidates(s1, s2):
    k, t = s1.shape
    blk = SUBLANE
    a_lo = lax.broadcasted_iota(I32, (blk, t), 0)
    vals, codes = [], []
    for a0 in range(0, k, blk):
        vals.append(s1[a0:a0 + blk, :] + s2[0:1, :])
        codes.append((a_lo + a0) * k)
    for b in range(1, blk):
        ok = a_lo < k // (b + 1)
        vals.append(jnp.where(ok, s1[0:blk, :] + s2[b:b + 1, :], -jnp.inf))
        codes.append(a_lo * k + b)
    for b0 in range(blk, k, blk):
        vals.append(s1[0:1, :] + s2[b0:b0 + blk, :])
        codes.append(a_lo + b0)
    return jnp.concatenate(vals, axis=0), jnp.concatenate(codes, axis=0).astype(F32)


def _peer_topk_kernel(qp_ref, keys_ref, e_ref, g_ref, *, n_keys):
    heads = keys_ref.shape[0]
    half = keys_ref.shape[3]
    k = PEER_TOPK
    t = qp_ref.shape[0]
    key_codes = lax.broadcasted_iota(I32, (n_keys, t), 0).astype(F32)
    e_rows, g_rows = [], []
    for h in range(heads):
        tops = []
        for c in range(2):
            col = (h * 2 + c) * half
            q = qp_ref[:, col:col + half].astype(BF16)
            st = lax.dot_general(keys_ref[h, c].astype(BF16), q, NT_DIMS, preferred_element_type=F32)
            tops.append(_topk_rows(st, key_codes, k))
        (s1, i1), (s2, i2) = tops
        cand, codes = _pair_candidates(s1, s2)
        top, flat = _topk_rows(cand, codes, k)
        flat = flat.astype(I32)
        ia = _select_rows(i1, flat // k)
        ib = _select_rows(i2, flat % k)
        e_rows.append(((ia * n_keys + ib) * EXPERT_ROWS).astype(I32))
        ex = jnp.exp(top - jnp.max(top, axis=0, keepdims=True))
        g_rows.append(ex / jnp.sum(ex, axis=0, keepdims=True))
    e_ref[...] = jnp.transpose(jnp.concatenate(e_rows, axis=0))
    g_ref[...] = jnp.transpose(jnp.concatenate(g_rows, axis=0))


def _peer_topk_call(qp, keys, *, tm):
    n = qp.shape[0]
    heads, _, n_keys, half = keys.shape
    slots = heads * PEER_TOPK
    return pl.pallas_call(
        functools.partial(_peer_topk_kernel, n_keys=n_keys),
        grid=(n // tm,),
        in_specs=[
            pl.BlockSpec((tm, qp.shape[1]), lambda i: (i, 0)),
            pl.BlockSpec(keys.shape, lambda i: (0, 0, 0, 0)),
        ],
        out_specs=[
            pl.BlockSpec((tm, slots), lambda i: (i, 0)),
            pl.BlockSpec((tm, slots), lambda i: (i, 0)),
        ],
        out_shape=[
            jax.ShapeDtypeStruct((n, slots), I32),
            jax.ShapeDtypeStruct((n, slots), F32),
        ],
        compiler_params=_cparams(("arbitrary",), VMEM_LIMIT),
    )(qp, keys)


EXPERT_ROWS = 4


def _pack_table(tab):
    n, d = tab.shape
    assert d == 2 * EXPERT_ROWS * LANE
    bits = lax.bitcast_convert_type(tab.astype(BF16), jnp.uint16).astype(U32)
    packed = bits[:, :d // 2] | (bits[:, d // 2:] << 16)
    return packed.reshape(n * EXPERT_ROWS, LANE)


PACKED_ROWS = 2 * EXPERT_ROWS
GROUPS_PER_ITER = 4


def _for_token_groups(n, group):
    per = max(c for c in range(1, GROUPS_PER_ITER + 1) if n % c == 0)

    def body(i, carry):
        for u in range(per):
            group(i * per + u, carry)
        return carry

    lax.fori_loop(0, n // per, body, 0)


IDX_WAYS = 4


def _split_index(e):
    n, slots = e.shape
    e3 = e.reshape(n, slots // IDX_WAYS, IDX_WAYS)
    return [e3[:, :, j].reshape(-1) for j in range(IDX_WAYS)]


def _gather_rows(tab_ref, idx_refs, t, slots):
    per_way = slots // IDX_WAYS
    rows = []
    for k in range(per_way):
        pos = t * per_way + k
        for j in range(IDX_WAYS):
            off = pl.multiple_of(idx_refs[j][pos], EXPERT_ROWS)
            rows.append(tab_ref[pl.ds(off, EXPERT_ROWS), :])
    return jnp.concatenate(rows, axis=0)


def _pack_bf16_pair(lo, hi):
    lo_bits = lax.bitcast_convert_type(lo.astype(BF16).astype(F32), U32) >> 16
    hi_bits = lax.bitcast_convert_type(hi.astype(BF16).astype(F32), U32) & jnp.uint32(0xFFFF0000)
    return lo_bits | hi_bits


def _peer_u_kernel(*refs):
    idx_refs = refs[:IDX_WAYS]
    h_ref, g_ref, tab_ref, fold_ref, w_ref = refs[IDX_WAYS:]
    tb, slots = g_ref.shape
    cols = slots * PACKED_ROWS
    sub = lax.broadcasted_iota(I32, (SUBLANE, cols), 0)
    lane = lax.broadcasted_iota(I32, (SUBLANE, cols), 1)
    diag = (lane & (PACKED_ROWS - 1)) == sub
    fold = fold_ref[...]

    def group(t8, carry):
        rows = pl.ds(pl.multiple_of(t8 * SUBLANE, SUBLANE), SUBLANE)
        h8 = h_ref[rows, :]
        half = h8.shape[1] // 2
        packed = _pack_bf16_pair(h8[:, :half], h8[:, half:])
        v8 = jnp.zeros((SUBLANE, cols), F32)
        for s in range(SUBLANE):
            t = t8 * SUBLANE + s
            words = _gather_rows(tab_ref, idx_refs, t, slots)
            hw = jnp.concatenate([packed[s:s + 1, j * LANE:(j + 1) * LANE] for j in range(EXPERT_ROWS)], axis=0)
            m = lax.dot_general(pltpu.bitcast(hw, BF16), pltpu.bitcast(words, BF16), NT_DIMS,
                                preferred_element_type=F32)
            v = jnp.sum(jnp.where(diag, m, 0.0), axis=0, keepdims=True)
            v8 = jnp.where(sub == s, v, v8)
        hi = v8.astype(BF16)
        lo = (v8 - hi.astype(F32)).astype(BF16)
        act = jnp.dot(hi, fold, preferred_element_type=F32) + jnp.dot(lo, fold, preferred_element_type=F32)
        w_ref[rows, :] = g_ref[rows, :] * jax.nn.gelu(act)
        return carry

    _for_token_groups(tb // SUBLANE, group)


def _index_specs(tb, slots):
    per_way = slots // IDX_WAYS
    return [pl.BlockSpec((tb * per_way,), lambda i: (i,), memory_space=pltpu.SMEM) for _ in range(IDX_WAYS)]


def _peer_u_call(idx_ways, h2, g, tab, *, tb):
    n, slots = g.shape
    d = h2.shape[1]
    fold = jnp.repeat(jnp.eye(slots, dtype=BF16), PACKED_ROWS, axis=0)
    return pl.pallas_call(
        _peer_u_kernel,
        grid=(n // tb,),
        in_specs=[
            *_index_specs(tb, slots),
            pl.BlockSpec((tb, d), lambda i: (i, 0)),
            pl.BlockSpec((tb, slots), lambda i: (i, 0)),
            pl.BlockSpec(tab.shape, lambda i: (0, 0), pipeline_mode=pl.Buffered(1)),
            pl.BlockSpec(fold.shape, lambda i: (0, 0)),
        ],
        out_specs=pl.BlockSpec((tb, slots), lambda i: (i, 0)),
        out_shape=jax.ShapeDtypeStruct((n, slots), F32),
        compiler_params=_cparams(("arbitrary",), VMEM_LIMIT),
    )(*idx_ways, h2, g, tab, fold)


def _peer_v_kernel(*refs):
    idx_refs = refs[:IDX_WAYS]
    w_ref, x_ref, tab_ref, expand_ref, o_ref = refs[IDX_WAYS:]
    tb, slots = w_ref.shape
    cols = slots * PACKED_ROWS
    sub = lax.broadcasted_iota(I32, (SUBLANE, cols), 0)
    j = lax.broadcasted_iota(I32, (SUBLANE, cols), 1) & (PACKED_ROWS - 1)
    probe = pltpu.bitcast(jnp.full((EXPERT_ROWS, LANE), 0x40003F80, U32), BF16).astype(F32)
    low_first = probe[0:1, 0:1] == 1.0
    half = jnp.where(low_first, j & 1, 1 - (j & 1))
    place = (half * EXPERT_ROWS + (j >> 1)) == sub
    expand = expand_ref[...]

    def group(t8, carry):
        rows8 = pl.ds(pl.multiple_of(t8 * SUBLANE, SUBLANE), SUBLANE)
        wrep = jnp.dot(w_ref[rows8, :].astype(BF16), expand, preferred_element_type=F32)
        outs = []
        for s in range(SUBLANE):
            t = t8 * SUBLANE + s
            words = _gather_rows(tab_ref, idx_refs, t, slots)
            lhs = jnp.where(place, wrep[s:s + 1, :], 0.0).astype(BF16)
            outs.append(jnp.dot(lhs, pltpu.bitcast(words, BF16), preferred_element_type=F32))
        peer = jnp.concatenate(
            [jnp.concatenate([outs[s][i:i + 1, :] for s in range(SUBLANE)], axis=0) for i in range(PACKED_ROWS)],
            axis=1)
        o_ref[rows8, :] = x_ref[rows8, :] + peer
        return carry

    _for_token_groups(tb // SUBLANE, group)


def _peer_v_call(idx_ways, w, x1, tab, *, tb):
    n, slots = w.shape
    d = x1.shape[1]
    assert d == PACKED_ROWS * LANE
    expand = jnp.repeat(jnp.eye(slots, dtype=BF16), PACKED_ROWS, axis=1)
    return pl.pallas_call(
        _peer_v_kernel,
        grid=(n // tb,),
        in_specs=[
            *_index_specs(tb, slots),
            pl.BlockSpec((tb, slots), lambda i: (i, 0)),
            pl.BlockSpec((tb, d), lambda i: (i, 0)),
            pl.BlockSpec(tab.shape, lambda i: (0, 0), pipeline_mode=pl.Buffered(1)),
            pl.BlockSpec(expand.shape, lambda i: (0, 0)),
        ],
        out_specs=pl.BlockSpec((tb, d), lambda i: (i, 0)),
        out_shape=jax.ShapeDtypeStruct((n, d), F32),
        compiler_params=_cparams(("arbitrary",), VMEM_LIMIT),
    )(*idx_ways, w, x1, tab, expand)


def _ple_kernel(x_ref, p_ref, g_ref, wg_ref, wp_ref, o_ref):
    x = x_ref[...]
    ms = jnp.mean(x * x, axis=-1, keepdims=True)
    hn = (x * lax.rsqrt(ms + RMS_EPS) * g_ref[...]).astype(BF16)
    gate = jax.nn.sigmoid(jnp.dot(hn, wg_ref[...], preferred_element_type=F32))
    pe = jnp.dot(p_ref[...].astype(BF16), wp_ref[...], preferred_element_type=F32)
    o_ref[...] = x + gate * pe


def _ple_call(x2, p, g, wg, wp, *, tm):
    n, d = x2.shape
    return pl.pallas_call(
        _ple_kernel,
        grid=(n // tm,),
        in_specs=[
            pl.BlockSpec((tm, d), lambda i: (i, 0)),
            pl.BlockSpec((tm, p.shape[1]), lambda i: (i, 0)),
            pl.BlockSpec((1, d), lambda i: (0, 0)),
            pl.BlockSpec(wg.shape, lambda i: (0, 0)),
            pl.BlockSpec(wp.shape, lambda i: (0, 0)),
        ],
        out_specs=pl.BlockSpec((tm, d), lambda i: (i, 0)),
        out_shape=jax.ShapeDtypeStruct((n, d), F32),
        compiler_params=_cparams(("arbitrary",), VMEM_LIMIT),
    )(x2, p, g, wg, wp)


def _rope_tables(pos, dim):
    half = dim // 2
    inv = ROPE_BASE ** (-jnp.arange(half, dtype=F32) / half)
    ang = pos.astype(F32)[:, None] * inv[None, :]
    cos, sin = jnp.cos(ang), jnp.sin(ang)
    return jnp.concatenate([cos, cos], axis=1), jnp.concatenate([-sin, sin], axis=1)


def _ret_tables(heads, chunk, n_valid):
    lg = jnp.log1p(-jnp.exp2(-5.0 - jnp.arange(heads, dtype=F32)))
    idx = jnp.arange(chunk, dtype=F32)
    diff = idx[:, None] - idx[None, :]
    causal = diff >= 0
    dec = jnp.where(causal[None], jnp.exp(jnp.where(causal, diff, 0.0)[None] * lg[:, None, None]), 0.0)
    qd = jnp.exp((idx[None, :] + 1.0) * lg[:, None])
    kd = jnp.exp((n_valid - 1.0 - idx)[None, :] * lg[:, None])
    kd = jnp.where(idx[None, :] < n_valid, kd, 0.0)
    sd = jnp.exp(n_valid * lg)
    return dec, qd[:, :, None], kd[:, :, None], sd[:, None, None]


def _token_tail(x2d, oa, ob, proj, p2d, lw, *, tm, tb, gate_col0):
    x1, h2, qp = _merge_call(x2d, oa, ob, proj, lw["wa"], lw["wb"], lw["wo"], lw["gf"], lw["wq"],
                             tm=tm, gate_col0=gate_col0)
    e, g = _peer_topk_call(qp, lw["keys"], tm=tm)
    idx_ways = _split_index(e)
    w = _peer_u_call(idx_ways, h2, g, lw["u_tab"], tb=tb)
    x2 = _peer_v_call(idx_ways, w, x1, lw["v_tab"], tb=tb)
    return _ple_call(x2, p2d, lw["gp"], lw["wg"], lw["wp"], tm=tm)


def kernel(x_prompt, x_sample, cache_k, cache_v, state_ret, page_table, p_prompt, p_sample, norm_mix_g, w_in,
           q_norm_g, k_norm_g, sb_bias, ret_norm_g, w_a_up, w_b_up, w_out, norm_ffn_g, peer_w_q, peer_sub_keys,
           peer_u, peer_v, norm_ple_g, w_ple_gate, w_ple):
    batch, seq, d = x_prompt.shape
    dec_batch, dec_seq, _ = x_sample.shape
    depth, n_phys, page, sb_heads, sb_dim = cache_k.shape
    _, _, ret_heads, dk, dv = state_ret.shape
    sb_width = sb_heads * sb_dim
    past_len = page_table.shape[1] * page
    chunk = LANE
    rows_pad = SUBLANE
    assert dec_seq <= rows_pad and dk == LANE and sb_width % LANE == 0
    qb_col0 = 3 * sb_width
    gate_col0 = qb_col0 + 2 * ret_heads * dk + 2 * ret_heads * dv

    xp = x_prompt.reshape(batch * seq, d)
    xs = jnp.pad(x_sample, ((0, 0), (0, rows_pad - dec_seq), (0, 0))).reshape(dec_batch * rows_pad, d)
    ns = dec_batch * rows_pad

    cs_p, sn_p = _rope_tables(jnp.arange(seq), dk)
    cs_s, sn_s = _rope_tables(past_len + jnp.arange(rows_pad), dk)
    cs_s, sn_s = jnp.tile(cs_s, (dec_batch, 1)), jnp.tile(sn_s, (dec_batch, 1))
    assert sb_dim & (sb_dim - 1) == 0 and sb_width == PROJ_TN
    gm = jnp.kron(jnp.eye(sb_heads, dtype=F32), jnp.full((sb_dim, sb_dim), 1.0 / sb_dim, F32)).astype(BF16)
    tri = jnp.concatenate([jnp.tril(jnp.ones((LANE, LANE), F32), -1), jnp.ones((LANE, LANE), F32)], axis=1).astype(BF16)
    dec_p = _ret_tables(ret_heads, chunk, float(chunk))
    dec_s = _ret_tables(ret_heads, chunk, float(dec_seq))

    tm = ROW_TILE
    tb = PEER_TOKEN_BLOCK
    assert (batch * seq) % tm == 0 and seq % tm == 0 and (batch * seq) % tb == 0
    kp, vp, sp, kn, vn, sn_out = [], [], [], [], [], []
    for i in range(depth):
        lw = dict(
            wa=w_a_up[i].astype(BF16), wb=w_b_up[i].astype(BF16), wo=w_out[i].astype(BF16),
            gf=norm_ffn_g[i][None, :], wq=peer_w_q[i].astype(BF16), keys=peer_sub_keys[i],
            u_tab=_pack_table(peer_u[i]), v_tab=_pack_table(peer_v[i]),
            gp=norm_ple_g[i][None, :], wg=w_ple_gate[i].astype(BF16), wp=w_ple[i].astype(BF16),
        )
        w_bf = w_in[i].astype(BF16)
        g_mix = norm_mix_g[i][None, :]
        qg = jnp.tile(q_norm_g[i], sb_heads)[None, :]
        kg = jnp.tile(k_norm_g[i], sb_heads)[None, :]
        rg = ret_norm_g[i][:, None, :]
        kb_scale = dk ** -0.5

        proj = _proj_call(xp, g_mix, w_bf, cs_p, sn_p, qg, kg, gm, tm=tm, pos_blocks=seq // tm, kb_scale=kb_scale)
        oa = _sb_prompt_call(proj, sb_bias[i], batch=batch, seq=seq, heads=sb_heads, head_dim=sb_dim)
        ob, st_p = _ret_call(proj, None, *dec_p, rg, batch=batch, n_chunks=seq // chunk, rows=chunk,
                             heads=ret_heads, dk=dk, dv=dv, col0=qb_col0)
        xp = _token_tail(xp, oa, ob, proj, p_prompt[i].reshape(batch * seq, -1), lw, tm=tm, tb=tb,
                         gate_col0=gate_col0)
        kp.append(proj[:, sb_width:2 * sb_width].reshape(batch, seq, sb_heads, sb_dim))
        vp.append(proj[:, 2 * sb_width:3 * sb_width].reshape(batch, seq, sb_heads, sb_dim))
        sp.append(st_p)

        proj_s = _proj_call(xs, g_mix, w_bf, cs_s, sn_s, qg, kg, gm, tm=ns, pos_blocks=1, kb_scale=kb_scale)
        bias_col = jnp.repeat(sb_bias[i], rows_pad)[:, None]
        oa_s = _sb_sample_call(page_table, proj_s, jnp.transpose(cache_k[i], (0, 2, 3, 1)),
                               jnp.transpose(cache_v[i], (0, 2, 3, 1)), tri, bias_col,
                               dec_batch=dec_batch, rows_pad=rows_pad, heads=sb_heads, head_dim=sb_dim)
        ob_s, st_s = _ret_call(proj_s, state_ret[i], *dec_s, rg, batch=dec_batch, n_chunks=1, rows=rows_pad,
                               heads=ret_heads, dk=dk, dv=dv, col0=qb_col0)
        p_s = jnp.pad(p_sample[i], ((0, 0), (0, rows_pad - dec_seq), (0, 0))).reshape(ns, -1)
        xs = _token_tail(xs, oa_s, ob_s, proj_s, p_s, lw, tm=ns, tb=ns, gate_col0=gate_col0)
        ka_s = proj_s[:, sb_width:2 * sb_width].reshape(dec_batch, rows_pad, sb_heads, sb_dim)[:, :dec_seq]
        va_s = proj_s[:, 2 * sb_width:3 * sb_width].reshape(dec_batch, rows_pad, sb_heads, sb_dim)[:, :dec_seq]
        kn.append(ka_s)
        vn.append(va_s)
        sn_out.append(st_s)

    y_prompt = xp.reshape(batch, seq, d)
    y_sample = xs.reshape(dec_batch, rows_pad, d)[:, :dec_seq]
    return (y_prompt, y_sample, jnp.stack(kp), jnp.stack(vp), jnp.stack(sp), jnp.stack(kn), jnp.stack(vn),
            jnp.stack(sn_out))
```

```python
import functools

import jax
import jax.numpy as jnp
from jax import lax
from jax.experimental import pallas as pl
from jax.experimental.pallas import tpu as pltpu

F32 = jnp.float32
BF16 = jnp.bfloat16
I32 = jnp.int32
U32 = jnp.uint32

RMS_EPS = 1e-6
ROPE_BASE = 10000.0
PEER_TOPK = 16
LANE = 128
SUBLANE = 8
VMEM_BYTES_V7X = 64 * 1024 * 1024
VMEM_LIMIT = VMEM_BYTES_V7X - 8 * 1024 * 1024
ROW_TILE = 256
PEER_TOKEN_BLOCK = 256
NT_DIMS = (((1,), (1,)), ((), ()))
TN_DIMS = (((0,), (0,)), ((), ()))


def _cparams(sem, vmem=None):
    return pltpu.CompilerParams(dimension_semantics=sem, vmem_limit_bytes=vmem)


PROJ_TN = 512


def _proj_kernel(x_ref, g_ref, w_ref, cs_ref, sn_ref, qg_ref, kg_ref, gm_ref, o_ref, *kv_t_refs, kb_scale):
    x = x_ref[...]
    ms = jnp.mean(x * x, axis=-1, keepdims=True)
    hn = (x * lax.rsqrt(ms + RMS_EPS) * g_ref[...]).astype(BF16)
    gm = gm_ref[...]
    for j in range(o_ref.shape[1] // PROJ_TN):
        cols = slice(j * PROJ_TN, (j + 1) * PROJ_TN)
        acc = jnp.dot(hn, w_ref[:, cols], preferred_element_type=F32)
        if j <= 1:
            sq = acc * acc
            hi = sq.astype(BF16)
            lo = (sq - hi.astype(F32)).astype(BF16)
            msq = jnp.dot(hi, gm, preferred_element_type=F32) + jnp.dot(lo, gm, preferred_element_type=F32)
            gain = qg_ref[...] if j == 0 else kg_ref[...]
            y = acc * lax.rsqrt(msq + RMS_EPS) * gain
            o_ref[:, cols] = y
            if j == 1 and kv_t_refs:
                kv_t_refs[0][...] = jnp.transpose(y)
        elif j in (3, 4):
            cs = cs_ref[...]
            sn = sn_ref[...]
            for h in range(PROJ_TN // LANE):
                xh = acc[:, h * LANE:(h + 1) * LANE]
                r = pltpu.roll(xh, LANE // 2, 1)
                y = xh * cs + r * sn
                lo_col = j * PROJ_TN + h * LANE
                o_ref[:, lo_col:lo_col + LANE] = y * kb_scale if j == 4 else y
        else:
            o_ref[:, cols] = acc
            if j == 2 and kv_t_refs:
                kv_t_refs[1][...] = jnp.transpose(acc)


def _proj_call(x2d, g, w_bf, cs, sn, qg, kg, gm, *, tm, pos_blocks, kb_scale, kv_t_batch=None):
    n, d = x2d.shape
    pw = w_bf.shape[1]
    const = lambda i: (0, 0)
    out_specs = [pl.BlockSpec((tm, pw), lambda i: (i, 0))]
    out_shape = [jax.ShapeDtypeStruct((n, pw), F32)]
    if kv_t_batch is not None:
        seq = n // kv_t_batch
        spb = seq // tm
        for _ in range(2):
            out_specs.append(pl.BlockSpec((None, PROJ_TN, tm), lambda i: (i // spb, 0, i % spb)))
            out_shape.append(jax.ShapeDtypeStruct((kv_t_batch, PROJ_TN, seq), F32))
    return pl.pallas_call(
        functools.partial(_proj_kernel, kb_scale=kb_scale),
        grid=(n // tm,),
        in_specs=[
            pl.BlockSpec((tm, d), lambda i: (i, 0)),
            pl.BlockSpec((1, d), const),
            pl.BlockSpec((d, pw), const, pipeline_mode=pl.Buffered(1)),
            pl.BlockSpec((tm, LANE), lambda i: (i % pos_blocks, 0)),
            pl.BlockSpec((tm, LANE), lambda i: (i % pos_blocks, 0)),
            pl.BlockSpec((1, PROJ_TN), const),
            pl.BlockSpec((1, PROJ_TN), const),
            pl.BlockSpec((PROJ_TN, PROJ_TN), const),
        ],
        out_specs=out_specs,
        out_shape=out_shape,
        compiler_params=_cparams(("arbitrary",), VMEM_LIMIT),
    )(x2d, g, w_bf, cs, sn, qg, kg, gm)


def _sb_prompt_kernel(bias_ref, q_ref, k_ref, v_ref, tri_ref, o_ref, qs_ref, acc_ref, car_ref, *, scale):
    i = pl.program_id(1)
    tq = q_ref.shape[0]
    pairs = qs_ref.shape[0]
    lane = lax.broadcasted_iota(I32, (1, LANE), 1)
    hmask = [lane < LANE // 2, lane >= LANE // 2]
    first = lax.broadcasted_iota(I32, (1, 2 * LANE), 1) < LANE

    def split_heads(x):
        return jnp.concatenate([jnp.where(hmask[0], x, 0.0), jnp.where(hmask[1], x, 0.0)], axis=0).astype(BF16)

    for t in range(pairs):
        qs_ref[t] = (q_ref[:, t * LANE:(t + 1) * LANE] * scale).astype(BF16)
    tri2 = tri_ref[...]
    acc_ref[...] = jnp.zeros_like(acc_ref)
    car_ref[...] = jnp.zeros_like(car_ref)
    row = lax.broadcasted_iota(I32, (tq, 2 * LANE), 0)
    col = lax.broadcasted_iota(I32, (tq, 2 * LANE), 1) & (LANE - 1)
    causal2 = col < row

    def block(kbs, mask2):
        starts = [pl.multiple_of(kb * LANE, LANE) for kb in kbs]
        tiles = [slice(t * LANE, (t + 1) * LANE) for t in range(pairs)]
        units = [(n, t) for n in range(len(kbs)) for t in range(pairs)]
        z = {}
        for n, t in units:
            rk = split_heads(k_ref[pl.ds(starts[n], LANE), tiles[t]])
            bias2 = jnp.where(first, bias_ref[2 * t], bias_ref[2 * t + 1])
            z[n, t] = lax.dot_general(qs_ref[t], rk, NT_DIMS, preferred_element_type=F32) + bias2
        sp = {u: jnp.maximum(z[u], 0.0) + jnp.log(1.0 + jnp.exp(-jnp.abs(z[u]))) for u in units}
        spm = sp if mask2 is None else {u: jnp.where(mask2, sp[u], 0.0) for u in units}
        cs = {}
        for u in units:
            hi = spm[u].astype(BF16)
            lo = (spm[u] - hi.astype(F32)).astype(BF16)
            cs[u] = jnp.dot(hi, tri2, preferred_element_type=F32) + jnp.dot(lo, tri2, preferred_element_type=F32)
        a = {}
        for t in range(pairs):
            car = car_ref[t]
            for n in range(len(kbs)):
                at = jnp.exp(z[n, t] - sp[n, t] - cs[n, t] - car)
                a[n, t] = at if mask2 is None else jnp.where(mask2, at, 0.0)
                tot = [jnp.broadcast_to(jnp.sum(spm[n, t][:, h * LANE:(h + 1) * LANE], axis=1, keepdims=True),
                                        (tq, LANE)) for h in range(2)]
                car = car + jnp.concatenate(tot, axis=1)
            car_ref[t] = car
        for t in range(pairs):
            contrib = None
            for n in range(len(kbs)):
                rv = split_heads(v_ref[pl.ds(starts[n], LANE), tiles[t]])
                d = jnp.dot(a[n, t].astype(BF16), rv, preferred_element_type=F32)
                contrib = d if contrib is None else contrib + d
            acc_ref[:, tiles[t]] += contrib

    block([i], causal2)

    def body(jj, c):
        block([i - 1 - 2 * jj, i - 2 - 2 * jj], None)
        return c

    lax.fori_loop(0, lax.shift_right_logical(i, 1), body, 0)

    @pl.when((i & 1) == 1)
    def _():
        block([0], None)

    o_ref[...] = acc_ref[...]


def _sb_prompt_call(proj, bias, *, batch, seq, heads, head_dim):
    assert 2 * head_dim == LANE
    tq = LANE
    nq = seq // tq
    width = heads * head_dim
    pairs = heads // 2
    tri2 = jnp.kron(jnp.eye(2, dtype=F32), jnp.tril(jnp.ones((LANE, LANE), F32), -1)).astype(BF16)
    return pl.pallas_call(
        functools.partial(_sb_prompt_kernel, scale=head_dim ** -0.5),
        grid=(batch, nq),
        in_specs=[
            pl.BlockSpec(memory_space=pltpu.SMEM),
            pl.BlockSpec((tq, width), lambda b, i: (b * nq + i, 0)),
            pl.BlockSpec((seq, width), lambda b, i: (b, 1)),
            pl.BlockSpec((seq, width), lambda b, i: (b, 2)),
            pl.BlockSpec((2 * LANE, 2 * LANE), lambda b, i: (0, 0)),
        ],
        out_specs=pl.BlockSpec((tq, width), lambda b, i: (b * nq + i, 0)),
        out_shape=jax.ShapeDtypeStruct((batch * seq, width), F32),
        scratch_shapes=[
            pltpu.VMEM((pairs, tq, LANE), BF16),
            pltpu.VMEM((tq, width), F32),
            pltpu.VMEM((pairs, tq, 2 * LANE), F32),
        ],
        compiler_params=_cparams(("arbitrary", "arbitrary"), VMEM_LIMIT),
    )(bias, proj, proj, proj, tri2)


def _sb_sample_kernel(pt_ref, q_ref, kn_ref, vn_ref, *rest, scale, pages_per_step):
    kc_refs = rest[:pages_per_step]
    vc_refs = rest[pages_per_step:2 * pages_per_step]
    tri_ref, bias_ref, o_ref, qh_ref, nk_ref, nv_ref, acc_ref, car_ref = rest[2 * pages_per_step:]
    p = pl.program_id(1)
    heads, rq, hd = qh_ref.shape
    tri = tri_ref[...]
    bias = bias_ref[...]
    nn_dims = (((1,), (0,)), ((), ()))

    def blocks(load_k, load_v, n_blocks, mask, token_major):
        qk_dims, av_dims = (NT_DIMS, nn_dims) if token_major else (nn_dims, NT_DIMS)
        z = []
        for j in range(n_blocks):
            zj = [lax.dot_general(qh_ref[h], load_k(j, h).astype(BF16), qk_dims, preferred_element_type=F32)
                  for h in range(heads)]
            z.append(jnp.concatenate(zj, axis=0) + bias)
        sp = [jnp.maximum(zj, 0.0) + jnp.log(1.0 + jnp.exp(-jnp.abs(zj))) for zj in z]
        spm = sp if mask is None else [jnp.where(mask, s, 0.0) for s in sp]
        cs = []
        for s in spm:
            hi = s.astype(BF16)
            lo = (s - hi.astype(F32)).astype(BF16)
            cs.append(jnp.dot(hi, tri, preferred_element_type=F32) + jnp.dot(lo, tri, preferred_element_type=F32))
        car = car_ref[...]
        totals = [None] * heads
        for j in range(n_blocks):
            a = jnp.exp(z[j] - sp[j] - cs[j][:, :LANE] - car)
            if mask is not None:
                a = jnp.where(mask, a, 0.0)
            car = car + cs[j][:, LANE:]
            for h in range(heads):
                d = lax.dot_general(a[h * rq:(h + 1) * rq, :].astype(BF16), load_v(j, h).astype(BF16), av_dims,
                                    preferred_element_type=F32)
                totals[h] = d if totals[h] is None else totals[h] + d
        car_ref[...] = car
        for h in range(heads):
            acc_ref[h * rq:(h + 1) * rq, :] += totals[h]

    @pl.when(p == 0)
    def _():
        nk_ref[...] = jnp.zeros_like(nk_ref)
        nv_ref[...] = jnp.zeros_like(nv_ref)
        for h in range(heads):
            cols = slice(h * hd, (h + 1) * hd)
            qh_ref[h] = (q_ref[:, cols] * scale).astype(BF16)
            nk_ref[h, 0:rq, :] = kn_ref[:, cols]
            nv_ref[h, 0:rq, :] = vn_ref[:, cols]
        acc_ref[...] = jnp.zeros_like(acc_ref)
        car_ref[...] = jnp.zeros_like(car_ref)
        qrow = lax.broadcasted_iota(I32, (heads * rq, LANE), 0) & (rq - 1)
        col = lax.broadcasted_iota(I32, (heads * rq, LANE), 1)
        blocks(lambda j, h: nk_ref[h], lambda j, h: nv_ref[h], 1, col < qrow, True)

    blocks(lambda j, h: kc_refs[j][h], lambda j, h: vc_refs[j][h], pages_per_step, None, False)

    @pl.when(p == pl.num_programs(1) - 1)
    def _():
        for h in range(heads):
            o_ref[:, h * hd:(h + 1) * hd] = acc_ref[h * rq:(h + 1) * rq, :]


def _sb_sample_call(page_table, proj_s, cache_k3, cache_v3, tri, bias_col, *, dec_batch, rows_pad, heads, head_dim):
    n_pages = page_table.shape[1]
    width = heads * head_dim
    page = cache_k3.shape[3]
    rows = rows_pad * heads
    pps = max(c for c in (16, 8, 4, 2, 1) if n_pages % c == 0)

    def page_spec(j):
        return pl.BlockSpec((None, heads, head_dim, page),
                            lambda b, p, pt: (pt[b * n_pages + n_pages - 1 - (p * pps + j)], 0, 0, 0))

    grid_spec = pltpu.PrefetchScalarGridSpec(
        num_scalar_prefetch=1,
        grid=(dec_batch, n_pages // pps),
        in_specs=[
            pl.BlockSpec((rows_pad, width), lambda b, p, pt: (b, 0)),
            pl.BlockSpec((rows_pad, width), lambda b, p, pt: (b, 1)),
            pl.BlockSpec((rows_pad, width), lambda b, p, pt: (b, 2)),
            *[page_spec(j) for j in range(pps)],
            *[page_spec(j) for j in range(pps)],
            pl.BlockSpec((LANE, 2 * LANE), lambda b, p, pt: (0, 0)),
            pl.BlockSpec((rows, 1), lambda b, p, pt: (0, 0)),
        ],
        out_specs=pl.BlockSpec((rows_pad, width), lambda b, p, pt: (b, 0)),
        scratch_shapes=[
            pltpu.VMEM((heads, rows_pad, head_dim), BF16),
            pltpu.VMEM((heads, page, head_dim), F32),
            pltpu.VMEM((heads, page, head_dim), F32),
            pltpu.VMEM((rows, head_dim), F32),
            pltpu.VMEM((rows, LANE), F32),
        ],
    )
    return pl.pallas_call(
        functools.partial(_sb_sample_kernel, scale=head_dim ** -0.5, pages_per_step=pps),
        grid_spec=grid_spec,
        out_shape=jax.ShapeDtypeStruct((dec_batch * rows_pad, width), F32),
        compiler_params=_cparams(("arbitrary", "arbitrary"), VMEM_LIMIT),
    )(page_table.reshape(-1), proj_s, proj_s, proj_s, *([cache_k3] * pps), *([cache_v3] * pps), tri, bias_col)


def _ret_kernel(q_ref, k_ref, v0_ref, v1_ref, g0_ref, g1_ref, st0_ref, dec_ref, qd_ref, kd_ref, sd_ref, rg_ref,
                ob_ref, st_ref, state_ref, qp_ref, kp_ref, vp_ref, *, use_init):
    c = pl.program_id(1)
    heads, dk, dv = state_ref.shape
    rows = q_ref.shape[0]
    chunk = dec_ref.shape[1]
    half = v0_ref.shape[1]

    @pl.when(c == 0)
    def _():
        if use_init:
            state_ref[...] = st0_ref[...]
        else:
            state_ref[...] = jnp.zeros_like(state_ref)

    if rows == chunk:
        q, k = q_ref[...], k_ref[...]
        v = jnp.concatenate([v0_ref[...], v1_ref[...]], axis=1)
    else:
        qp_ref[...] = jnp.zeros_like(qp_ref)
        kp_ref[...] = jnp.zeros_like(kp_ref)
        vp_ref[...] = jnp.zeros_like(vp_ref)
        qp_ref[0:rows, :] = q_ref[...]
        kp_ref[0:rows, :] = k_ref[...]
        vp_ref[0:rows, 0:half] = v0_ref[...]
        vp_ref[0:rows, half:2 * half] = v1_ref[...]
        q, k, v = qp_ref[...], kp_ref[...], vp_ref[...]
    gb = jnp.concatenate([g0_ref[...], g1_ref[...]], axis=1)

    qb = [q[:, h * dk:(h + 1) * dk].astype(BF16) for h in range(heads)]
    vb = [v[:, h * dv:(h + 1) * dv].astype(BF16) for h in range(heads)]
    st = [state_ref[h] for h in range(heads)]
    scores = [lax.dot_general(qb[h], k[:, h * dk:(h + 1) * dk].astype(BF16), NT_DIMS, preferred_element_type=F32)
              * dec_ref[h] for h in range(heads)]
    cross = [jnp.dot(qb[h], st[h].astype(BF16), preferred_element_type=F32) * qd_ref[h] for h in range(heads)]
    kv = [lax.dot_general((k[:, h * dk:(h + 1) * dk] * kd_ref[h]).astype(BF16), vb[h], TN_DIMS,
                          preferred_element_type=F32) for h in range(heads)]
    inner = [jnp.dot(scores[h].astype(BF16), vb[h], preferred_element_type=F32) for h in range(heads)]
    for h in range(heads):
        state_ref[h] = sd_ref[h] * st[h] + kv[h]
        o = (inner[h] + cross[h])[0:rows, :]
        ms = jnp.mean(o * o, axis=-1, keepdims=True)
        g = gb[:, h * dv:(h + 1) * dv]
        ob_ref[:, h * dv:(h + 1) * dv] = (o * lax.rsqrt(ms + RMS_EPS) * rg_ref[h]) * (g * jax.nn.sigmoid(g))

    @pl.when(c == pl.num_programs(1) - 1)
    def _():
        st_ref[...] = state_ref[...]


def _ret_call(proj, state0, dec, qd, kd, sd, rg, *, batch, n_chunks, rows, heads, dk, dv, col0):
    chunk = dec.shape[1]
    qw = heads * dk
    assert heads * dv == 2 * qw and col0 % qw == 0
    qc = col0 // qw
    use_init = state0 is not None
    if state0 is None:
        state0 = jnp.zeros((1, heads, dk, dv), F32)
        st_map = lambda b, c: (0, 0, 0, 0)
    else:
        st_map = lambda b, c: (b, 0, 0, 0)
    pad_rows = chunk if rows != chunk else SUBLANE
    row = lambda j: (lambda b, c: (b * n_chunks + c, qc + j))
    const = lambda b, c: (0, 0, 0)
    return pl.pallas_call(
        functools.partial(_ret_kernel, use_init=use_init),
        grid=(batch, n_chunks),
        in_specs=[
            *[pl.BlockSpec((rows, qw), row(j)) for j in range(6)],
            pl.BlockSpec((None, heads, dk, dv), st_map),
            pl.BlockSpec((heads, chunk, chunk), const),
            pl.BlockSpec((heads, chunk, 1), const),
            pl.BlockSpec((heads, chunk, 1), const),
            pl.BlockSpec((heads, 1, 1), const),
            pl.BlockSpec((heads, 1, dv), const),
        ],
        out_specs=[
            pl.BlockSpec((rows, heads * dv), lambda b, c: (b * n_chunks + c, 0)),
            pl.BlockSpec((None, heads, dk, dv), lambda b, c: (b, 0, 0, 0)),
        ],
        out_shape=[
            jax.ShapeDtypeStruct((batch * n_chunks * rows, heads * dv), F32),
            jax.ShapeDtypeStruct((batch, heads, dk, dv), F32),
        ],
        scratch_shapes=[
            pltpu.VMEM((heads, dk, dv), F32),
            pltpu.VMEM((pad_rows, qw), F32),
            pltpu.VMEM((pad_rows, qw), F32),
            pltpu.VMEM((pad_rows, 2 * qw), F32),
        ],
        compiler_params=_cparams(("arbitrary", "arbitrary"), VMEM_LIMIT),
    )(proj, proj, proj, proj, proj, proj, state0, dec, qd, kd, sd, rg)


def _merge_kernel(x_ref, oa_ref, ob_ref, ga0_ref, ga1_ref, gb0_ref, gb1_ref, wa_ref, wb_ref, wo_ref, gf_ref,
                  wq_ref, x1_ref, h2_ref, qp_ref):
    ya = jnp.dot(oa_ref[...].astype(BF16), wa_ref[...], preferred_element_type=F32)
    yb = jnp.dot(ob_ref[...].astype(BF16), wb_ref[...], preferred_element_type=F32)
    ga = jnp.concatenate([ga0_ref[...], ga1_ref[...]], axis=1)
    gb = jnp.concatenate([gb0_ref[...], gb1_ref[...]], axis=1)
    merged = jax.nn.sigmoid(ga) * ya + jax.nn.sigmoid(gb) * yb
    x1 = x_ref[...] + jnp.dot(merged.astype(BF16), wo_ref[...], preferred_element_type=F32)
    x1_ref[...] = x1
    ms = jnp.mean(x1 * x1, axis=-1, keepdims=True)
    h2 = x1 * lax.rsqrt(ms + RMS_EPS) * gf_ref[...]
    h2_ref[...] = h2
    qp_ref[...] = jnp.dot(h2.astype(BF16), wq_ref[...], preferred_element_type=F32)


def _merge_call(x2d, oa, ob, proj, wa, wb, wo, gf, wq, *, tm, gate_col0):
    n, d = x2d.shape
    half = d // 2
    g0 = gate_col0 // half
    row = lambda i: (i, 0)
    const = lambda i: (0, 0)
    return pl.pallas_call(
        _merge_kernel,
        grid=(n // tm,),
        in_specs=[
            pl.BlockSpec((tm, d), row),
            pl.BlockSpec((tm, oa.shape[1]), row),
            pl.BlockSpec((tm, ob.shape[1]), row),
            pl.BlockSpec((tm, half), lambda i: (i, g0)),
            pl.BlockSpec((tm, half), lambda i: (i, g0 + 1)),
            pl.BlockSpec((tm, half), lambda i: (i, g0 + 2)),
            pl.BlockSpec((tm, half), lambda i: (i, g0 + 3)),
            pl.BlockSpec(wa.shape, const),
            pl.BlockSpec(wb.shape, const),
            pl.BlockSpec(wo.shape, const),
            pl.BlockSpec((1, d), const),
            pl.BlockSpec(wq.shape, const),
        ],
        out_specs=[
            pl.BlockSpec((tm, d), row),
            pl.BlockSpec((tm, d), row),
            pl.BlockSpec((tm, wq.shape[1]), row),
        ],
        out_shape=[
            jax.ShapeDtypeStruct((n, d), F32),
            jax.ShapeDtypeStruct((n, d), F32),
            jax.ShapeDtypeStruct((n, wq.shape[1]), F32),
        ],
        compiler_params=_cparams(("arbitrary",), VMEM_LIMIT),
    )(x2d, oa, ob, proj, proj, proj, proj, wa, wb, wo, gf, wq)


def _topk_rows(s, codes, k):
    t = s.shape[1]
    out_row = lax.broadcasted_iota(I32, (k, t), 0)
    vals = jnp.zeros((k, t), F32)
    picked = jnp.zeros((k, t), F32)
    for r in range(k):
        m = jnp.max(s, axis=0, keepdims=True)
        am = jnp.min(jnp.where(s == m, codes, jnp.inf), axis=0, keepdims=True)
        vals = jnp.where(out_row == r, m, vals)
        picked = jnp.where(out_row == r, am, picked)
        s = jnp.where(codes == am, -jnp.inf, s)
    return vals, picked


def _select_rows(table, sel):
    out = jnp.zeros(sel.shape, table.dtype)
    for a in range(table.shape[0]):
        out = jnp.where(sel == a, table[a:a + 1, :], out)
    return out


def _pair_candidates(s1, s2):
    k, t = s1.shape
    blk = SUBLANE
    a_lo = lax.broadcasted_iota(I32, (blk, t), 0)
    vals, codes = [], []
    for a0 in range(0, k, blk):
        vals.append(s1[a0:a0 + blk, :] + s2[0:1, :])
        codes.append((a_lo + a0) * k)
    for b in range(1, blk):
        ok = a_lo < k // (b + 1)
        vals.append(jnp.where(ok, s1[0:blk, :] + s2[b:b + 1, :], -jnp.inf))
        codes.append(a_lo * k + b)
    for b0 in range(blk, k, blk):
        vals.append(s1[0:1, :] + s2[b0:b0 + blk, :])
        codes.append(a_lo + b0)
    return jnp.concatenate(vals, axis=0), jnp.concatenate(codes, axis=0).astype(F32)


def _peer_topk_kernel(qp_ref, keys_ref, e_ref, g_ref, *, n_keys):
    heads = keys_ref.shape[0]
    half = keys_ref.shape[3]
    k = PEER_TOPK
    t = qp_ref.shape[0]
    key_codes = lax.broadcasted_iota(I32, (n_keys, t), 0).astype(F32)
    e_rows, g_rows = [], []
    for h in range(heads):
        tops = []
        for c in range(2):
            col = (h * 2 + c) * half
            q = qp_ref[:, col:col + half].astype(BF16)
            st = lax.dot_general(keys_ref[h, c].astype(BF16), q, NT_DIMS, preferred_element_type=F32)
            tops.append(_topk_rows(st, key_codes, k))
        (s1, i1), (s2, i2) = tops
        cand, codes = _pair_candidates(s1, s2)
        top, flat = _topk_rows(cand, codes, k)
        flat = flat.astype(I32)
        ia = _select_rows(i1, flat // k)
        ib = _select_rows(i2, flat % k)
        e_rows.append(((ia * n_keys + ib) * EXPERT_ROWS).astype(I32))
        ex = jnp.exp(top - jnp.max(top, axis=0, keepdims=True))
        g_rows.append(ex / jnp.sum(ex, axis=0, keepdims=True))
    e_ref[...] = jnp.transpose(jnp.concatenate(e_rows, axis=0))
    g_ref[...] = jnp.transpose(jnp.concatenate(g_rows, axis=0))


def _peer_topk_call(qp, keys, *, tm):
    n = qp.shape[0]
    heads, _, n_keys, half = keys.shape
    slots = heads * PEER_TOPK
    return pl.pallas_call(
        functools.partial(_peer_topk_kernel, n_keys=n_keys),
        grid=(n // tm,),
        in_specs=[
            pl.BlockSpec((tm, qp.shape[1]), lambda i: (i, 0)),
            pl.BlockSpec(keys.shape, lambda i: (0, 0, 0, 0)),
        ],
        out_specs=[
            pl.BlockSpec((tm, slots), lambda i: (i, 0)),
            pl.BlockSpec((tm, slots), lambda i: (i, 0)),
        ],
        out_shape=[
            jax.ShapeDtypeStruct((n, slots), I32),
            jax.ShapeDtypeStruct((n, slots), F32),
        ],
        compiler_params=_cparams(("arbitrary",), VMEM_LIMIT),
    )(qp, keys)


EXPERT_ROWS = 4


def _pack_table(tab):
    n, d = tab.shape
    assert d == 2 * EXPERT_ROWS * LANE
    bits = lax.bitcast_convert_type(tab.astype(BF16), jnp.uint16).astype(U32)
    packed = bits[:, :d // 2] | (bits[:, d // 2:] << 16)
    return packed.reshape(n * EXPERT_ROWS, LANE)


PACKED_ROWS = 2 * EXPERT_ROWS
GROUPS_PER_ITER = 4


def _for_token_groups(n, group):
    per = max(c for c in range(1, GROUPS_PER_ITER + 1) if n % c == 0)

    def body(i, carry):
        for u in range(per):
            group(i * per + u, carry)
        return carry

    lax.fori_loop(0, n // per, body, 0)


IDX_WAYS = 4


def _split_index(e):
    n, slots = e.shape
    e3 = e.reshape(n, slots // IDX_WAYS, IDX_WAYS)
    return [e3[:, :, j].reshape(-1) for j in range(IDX_WAYS)]


def _gather_rows(tab_ref, idx_refs, t, slots):
    per_way = slots // IDX_WAYS
    rows = []
    for k in range(per_way):
        pos = t * per_way + k
        for j in range(IDX_WAYS):
            off = pl.multiple_of(idx_refs[j][pos], EXPERT_ROWS)
            rows.append(tab_ref[pl.ds(off, EXPERT_ROWS), :])
    return jnp.concatenate(rows, axis=0)


def _pack_bf16_pair(lo, hi):
    lo_bits = lax.bitcast_convert_type(lo.astype(BF16).astype(F32), U32) >> 16
    hi_bits = lax.bitcast_convert_type(hi.astype(BF16).astype(F32), U32) & jnp.uint32(0xFFFF0000)
    return lo_bits | hi_bits


def _peer_u_kernel(*refs):
    idx_refs = refs[:IDX_WAYS]
    h_ref, g_ref, tab_ref, fold_ref, w_ref = refs[IDX_WAYS:]
    tb, slots = g_ref.shape
    cols = slots * PACKED_ROWS
    sub = lax.broadcasted_iota(I32, (SUBLANE, cols), 0)
    lane = lax.broadcasted_iota(I32, (SUBLANE, cols), 1)
    diag = (lane & (PACKED_ROWS - 1)) == sub
    fold = fold_ref[...]

    def group(t8, carry):
        rows = pl.ds(pl.multiple_of(t8 * SUBLANE, SUBLANE), SUBLANE)
        h8 = h_ref[rows, :]
        half = h8.shape[1] // 2
        packed = _pack_bf16_pair(h8[:, :half], h8[:, half:])
        v8 = jnp.zeros((SUBLANE, cols), F32)
        for s in range(SUBLANE):
            t = t8 * SUBLANE + s
            words = _gather_rows(tab_ref, idx_refs, t, slots)
            hw = jnp.concatenate([packed[s:s + 1, j * LANE:(j + 1) * LANE] for j in range(EXPERT_ROWS)], axis=0)
            m = lax.dot_general(pltpu.bitcast(hw, BF16), pltpu.bitcast(words, BF16), NT_DIMS,
                                preferred_element_type=F32)
            v = jnp.sum(jnp.where(diag, m, 0.0), axis=0, keepdims=True)
            v8 = jnp.where(sub == s, v, v8)
        hi = v8.astype(BF16)
        lo = (v8 - hi.astype(F32)).astype(BF16)
        act = jnp.dot(hi, fold, preferred_element_type=F32) + jnp.dot(lo, fold, preferred_element_type=F32)
        w_ref[rows, :] = g_ref[rows, :] * jax.nn.gelu(act)
        return carry

    _for_token_groups(tb // SUBLANE, group)


def _index_specs(tb, slots):
    per_way = slots // IDX_WAYS
    return [pl.BlockSpec((tb * per_way,), lambda i: (i,), memory_space=pltpu.SMEM) for _ in range(IDX_WAYS)]


def _peer_u_call(idx_ways, h2, g, tab, *, tb):
    n, slots = g.shape
    d = h2.shape[1]
    fold = jnp.repeat(jnp.eye(slots, dtype=BF16), PACKED_ROWS, axis=0)
    return pl.pallas_call(
        _peer_u_kernel,
        grid=(n // tb,),
        in_specs=[
            *_index_specs(tb, slots),
            pl.BlockSpec((tb, d), lambda i: (i, 0)),
            pl.BlockSpec((tb, slots), lambda i: (i, 0)),
            pl.BlockSpec(tab.shape, lambda i: (0, 0), pipeline_mode=pl.Buffered(1)),
            pl.BlockSpec(fold.shape, lambda i: (0, 0)),
        ],
        out_specs=pl.BlockSpec((tb, slots), lambda i: (i, 0)),
        out_shape=jax.ShapeDtypeStruct((n, slots), F32),
        compiler_params=_cparams(("arbitrary",), VMEM_LIMIT),
    )(*idx_ways, h2, g, tab, fold)


def _peer_v_kernel(*refs):
    idx_refs = refs[:IDX_WAYS]
    w_ref, x_ref, tab_ref, expand_ref, o_ref = refs[IDX_WAYS:]
    tb, slots = w_ref.shape
    cols = slots * PACKED_ROWS
    sub = lax.broadcasted_iota(I32, (SUBLANE, cols), 0)
    j = lax.broadcasted_iota(I32, (SUBLANE, cols), 1) & (PACKED_ROWS - 1)
    probe = pltpu.bitcast(jnp.full((EXPERT_ROWS, LANE), 0x40003F80, U32), BF16).astype(F32)
    low_first = probe[0:1, 0:1] == 1.0
    half = jnp.where(low_first, j & 1, 1 - (j & 1))
    place = (half * EXPERT_ROWS + (j >> 1)) == sub
    expand = expand_ref[...]

    def group(t8, carry):
        rows8 = pl.ds(pl.multiple_of(t8 * SUBLANE, SUBLANE), SUBLANE)
        wrep = jnp.dot(w_ref[rows8, :].astype(BF16), expand, preferred_element_type=F32)
        outs = []
        for s in range(SUBLANE):
            t = t8 * SUBLANE + s
            words = _gather_rows(tab_ref, idx_refs, t, slots)
            lhs = jnp.where(place, wrep[s:s + 1, :], 0.0).astype(BF16)
            outs.append(jnp.dot(lhs, pltpu.bitcast(words, BF16), preferred_element_type=F32))
        peer = jnp.concatenate(
            [jnp.concatenate([outs[s][i:i + 1, :] for s in range(SUBLANE)], axis=0) for i in range(PACKED_ROWS)],
            axis=1)
        o_ref[rows8, :] = x_ref[rows8, :] + peer
        return carry

    _for_token_groups(tb // SUBLANE, group)


def _peer_v_call(idx_ways, w, x1, tab, *, tb):
    n, slots = w.shape
    d = x1.shape[1]
    assert d == PACKED_ROWS * LANE
    expand = jnp.repeat(jnp.eye(slots, dtype=BF16), PACKED_ROWS, axis=1)
    return pl.pallas_call(
        _peer_v_kernel,
        grid=(n // tb,),
        in_specs=[
            *_index_specs(tb, slots),
            pl.BlockSpec((tb, slots), lambda i: (i, 0)),
            pl.BlockSpec((tb, d), lambda i: (i, 0)),
            pl.BlockSpec(tab.shape, lambda i: (0, 0), pipeline_mode=pl.Buffered(1)),
            pl.BlockSpec(expand.shape, lambda i: (0, 0)),
        ],
        out_specs=pl.BlockSpec((tb, d), lambda i: (i, 0)),
        out_shape=jax.ShapeDtypeStruct((n, d), F32),
        compiler_params=_cparams(("arbitrary",), VMEM_LIMIT),
    )(*idx_ways, w, x1, tab, expand)


def _ple_kernel(x_ref, p_ref, g_ref, wg_ref, wp_ref, o_ref):
    x = x_ref[...]
    ms = jnp.mean(x * x, axis=-1, keepdims=True)
    hn = (x * lax.rsqrt(ms + RMS_EPS) * g_ref[...]).astype(BF16)
    gate = jax.nn.sigmoid(jnp.dot(hn, wg_ref[...], preferred_element_type=F32))
    pe = jnp.dot(p_ref[...].astype(BF16), wp_ref[...], preferred_element_type=F32)
    o_ref[...] = x + gate * pe


def _ple_call(x2, p, g, wg, wp, *, tm):
    n, d = x2.shape
    return pl.pallas_call(
        _ple_kernel,
        grid=(n // tm,),
        in_specs=[
            pl.BlockSpec((tm, d), lambda i: (i, 0)),
            pl.BlockSpec((tm, p.shape[1]), lambda i: (i, 0)),
            pl.BlockSpec((1, d), lambda i: (0, 0)),
            pl.BlockSpec(wg.shape, lambda i: (0, 0)),
            pl.BlockSpec(wp.shape, lambda i: (0, 0)),
        ],
        out_specs=pl.BlockSpec((tm, d), lambda i: (i, 0)),
        out_shape=jax.ShapeDtypeStruct((n, d), F32),
        compiler_params=_cparams(("arbitrary",), VMEM_LIMIT),
    )(x2, p, g, wg, wp)


def _rope_tables(pos, dim):
    half = dim // 2
    inv = ROPE_BASE ** (-jnp.arange(half, dtype=F32) / half)
    ang = pos.astype(F32)[:, None] * inv[None, :]
    cos, sin = jnp.cos(ang), jnp.sin(ang)
    return jnp.concatenate([cos, cos], axis=1), jnp.concatenate([-sin, sin], axis=1)


def _ret_tables(heads, chunk, n_valid):
    lg = jnp.log1p(-jnp.exp2(-5.0 - jnp.arange(heads, dtype=F32)))
    idx = jnp.arange(chunk, dtype=F32)
    diff = idx[:, None] - idx[None, :]
    causal = diff >= 0
    dec = jnp.where(causal[None], jnp.exp(jnp.where(causal, diff, 0.0)[None] * lg[:, None, None]), 0.0)
    qd = jnp.exp((idx[None, :] + 1.0) * lg[:, None])
    kd = jnp.exp((n_valid - 1.0 - idx)[None, :] * lg[:, None])
    kd = jnp.where(idx[None, :] < n_valid, kd, 0.0)
    sd = jnp.exp(n_valid * lg)
    return dec, qd[:, :, None], kd[:, :, None], sd[:, None, None]


def _token_tail(x2d, oa, ob, proj, p2d, lw, *, tm, tb, gate_col0):
    x1, h2, qp = _merge_call(x2d, oa, ob, proj, lw["wa"], lw["wb"], lw["wo"], lw["gf"], lw["wq"],
                             tm=tm, gate_col0=gate_col0)
    e, g = _peer_topk_call(qp, lw["keys"], tm=tm)
    idx_ways = _split_index(e)
    w = _peer_u_call(idx_ways, h2, g, lw["u_tab"], tb=tb)
    x2 = _peer_v_call(idx_ways, w, x1, lw["v_tab"], tb=tb)
    return _ple_call(x2, p2d, lw["gp"], lw["wg"], lw["wp"], tm=tm)


def kernel(x_prompt, x_sample, cache_k, cache_v, state_ret, page_table, p_prompt, p_sample, norm_mix_g, w_in,
           q_norm_g, k_norm_g, sb_bias, ret_norm_g, w_a_up, w_b_up, w_out, norm_ffn_g, peer_w_q, peer_sub_keys,
           peer_u, peer_v, norm_ple_g, w_ple_gate, w_ple):
    batch, seq, d = x_prompt.shape
    dec_batch, dec_seq, _ = x_sample.shape
    depth, n_phys, page, sb_heads, sb_dim = cache_k.shape
    _, _, ret_heads, dk, dv = state_ret.shape
    sb_width = sb_heads * sb_dim
    past_len = page_table.shape[1] * page
    chunk = LANE
    rows_pad = SUBLANE
    assert dec_seq <= rows_pad and dk == LANE and sb_width % LANE == 0
    qb_col0 = 3 * sb_width
    gate_col0 = qb_col0 + 2 * ret_heads * dk + 2 * ret_heads * dv

    xp = x_prompt.reshape(batch * seq, d)
    xs = jnp.pad(x_sample, ((0, 0), (0, rows_pad - dec_seq), (0, 0))).reshape(dec_batch * rows_pad, d)
    ns = dec_batch * rows_pad

    cs_p, sn_p = _rope_tables(jnp.arange(seq), dk)
    cs_s, sn_s = _rope_tables(past_len + jnp.arange(rows_pad), dk)
    cs_s, sn_s = jnp.tile(cs_s, (dec_batch, 1)), jnp.tile(sn_s, (dec_batch, 1))
    assert sb_dim & (sb_dim - 1) == 0 and sb_width == PROJ_TN
    gm = jnp.kron(jnp.eye(sb_heads, dtype=F32), jnp.full((sb_dim, sb_dim), 1.0 / sb_dim, F32)).astype(BF16)
    tri = jnp.concatenate([jnp.tril(jnp.ones((LANE, LANE), F32), -1), jnp.ones((LANE, LANE), F32)], axis=1).astype(BF16)
    dec_p = _ret_tables(ret_heads, chunk, float(chunk))
    dec_s = _ret_tables(ret_heads, chunk, float(dec_seq))

    tm = ROW_TILE
    tb = PEER_TOKEN_BLOCK
    assert (batch * seq) % tm == 0 and seq % tm == 0 and (batch * seq) % tb == 0
    kp, vp, sp, kn, vn, sn_out = [], [], [], [], [], []
    for i in range(depth):
        lw = dict(
            wa=w_a_up[i].astype(BF16), wb=w_b_up[i].astype(BF16), wo=w_out[i].astype(BF16),
            gf=norm_ffn_g[i][None, :], wq=peer_w_q[i].astype(BF16), keys=peer_sub_keys[i],
            u_tab=_pack_table(peer_u[i]), v_tab=_pack_table(peer_v[i]),
            gp=norm_ple_g[i][None, :], wg=w_ple_gate[i].astype(BF16), wp=w_ple[i].astype(BF16),
        )
        w_bf = w_in[i].astype(BF16)
        g_mix = norm_mix_g[i][None, :]
        qg = jnp.tile(q_norm_g[i], sb_heads)[None, :]
        kg = jnp.tile(k_norm_g[i], sb_heads)[None, :]
        rg = ret_norm_g[i][:, None, :]
        kb_scale = dk ** -0.5

        proj, k_t, v_t = _proj_call(xp, g_mix, w_bf, cs_p, sn_p, qg, kg, gm, tm=tm, pos_blocks=seq // tm,
                                    kb_scale=kb_scale, kv_t_batch=batch)
        oa = _sb_prompt_call(proj, sb_bias[i], batch=batch, seq=seq, heads=sb_heads, head_dim=sb_dim)
        ob, st_p = _ret_call(proj, None, *dec_p, rg, batch=batch, n_chunks=seq // chunk, rows=chunk,
                             heads=ret_heads, dk=dk, dv=dv, col0=qb_col0)
        xp = _token_tail(xp, oa, ob, proj, p_prompt[i].reshape(batch * seq, -1), lw, tm=tm, tb=tb,
                         gate_col0=gate_col0)
        kp.append(jnp.transpose(k_t.reshape(batch, sb_heads, sb_dim, seq), (0, 3, 1, 2)))
        vp.append(jnp.transpose(v_t.reshape(batch, sb_heads, sb_dim, seq), (0, 3, 1, 2)))
        sp.append(st_p)

        (proj_s,) = _proj_call(xs, g_mix, w_bf, cs_s, sn_s, qg, kg, gm, tm=ns, pos_blocks=1, kb_scale=kb_scale)
        bias_col = jnp.repeat(sb_bias[i], rows_pad)[:, None]
        oa_s = _sb_sample_call(page_table, proj_s, jnp.transpose(cache_k[i], (0, 2, 3, 1)),
                               jnp.transpose(cache_v[i], (0, 2, 3, 1)), tri, bias_col,
                               dec_batch=dec_batch, rows_pad=rows_pad, heads=sb_heads, head_dim=sb_dim)
        ob_s, st_s = _ret_call(proj_s, state_ret[i], *dec_s, rg, batch=dec_batch, n_chunks=1, rows=rows_pad,
                               heads=ret_heads, dk=dk, dv=dv, col0=qb_col0)
        p_s = jnp.pad(p_sample[i], ((0, 0), (0, rows_pad - dec_seq), (0, 0))).reshape(ns, -1)
        xs = _token_tail(xs, oa_s, ob_s, proj_s, p_s, lw, tm=ns, tb=ns, gate_col0=gate_col0)
        ka_s = proj_s[:, sb_width:2 * sb_width].reshape(dec_batch, rows_pad, sb_heads, sb_dim)[:, :dec_seq]
        va_s = proj_s[:, 2 * sb_width:3 * sb_width].reshape(dec_batch, rows_pad, sb_heads, sb_dim)[:, :dec_seq]
        kn.append(ka_s)
        vn.append(va_s)
        sn_out.append(st_s)

    y_prompt = xp.reshape(batch, seq, d)
    y_sample = xs.reshape(dec_batch, rows_pad, d)[:, :dec_seq]
    return (y_prompt, y_sample, jnp.stack(kp), jnp.stack(vp), jnp.stack(sp), jnp.stack(kn), jnp.stack(vn),
            jnp.stack(sn_out))
```

```python
import functools

import jax
import jax.numpy as jnp
from jax import lax
from jax.experimental import pallas as pl
from jax.experimental.pallas import tpu as pltpu

F32 = jnp.float32
BF16 = jnp.bfloat16
I32 = jnp.int32
U32 = jnp.uint32

RMS_EPS = 1e-6
ROPE_BASE = 10000.0
PEER_TOPK = 16
LANE = 128
SUBLANE = 8
VMEM_BYTES_V7X = 64 * 1024 * 1024
VMEM_LIMIT = VMEM_BYTES_V7X - 8 * 1024 * 1024
ROW_TILE = 256
PEER_TOKEN_BLOCK = 256
NT_DIMS = (((1,), (1,)), ((), ()))
TN_DIMS = (((0,), (0,)), ((), ()))


def _cparams(sem, vmem=None):
    return pltpu.CompilerParams(dimension_semantics=sem, vmem_limit_bytes=vmem)


PROJ_TN = 512


def _proj_kernel(x_ref, g_ref, w_ref, cs_ref, sn_ref, qg_ref, kg_ref, gm_ref, o_ref, *kv_t_refs, kb_scale):
    x = x_ref[...]
    ms = jnp.mean(x * x, axis=-1, keepdims=True)
    hn = (x * lax.rsqrt(ms + RMS_EPS) * g_ref[...]).astype(BF16)
    gm = gm_ref[...]
    for j in range(o_ref.shape[1] // PROJ_TN):
        cols = slice(j * PROJ_TN, (j + 1) * PROJ_TN)
        acc = jnp.dot(hn, w_ref[:, cols], preferred_element_type=F32)
        if j <= 1:
            sq = acc * acc
            hi = sq.astype(BF16)
            lo = (sq - hi.astype(F32)).astype(BF16)
            msq = jnp.dot(hi, gm, preferred_element_type=F32) + jnp.dot(lo, gm, preferred_element_type=F32)
            gain = qg_ref[...] if j == 0 else kg_ref[...]
            y = acc * lax.rsqrt(msq + RMS_EPS) * gain
            o_ref[:, cols] = y
            if j == 1 and kv_t_refs:
                kv_t_refs[0][...] = jnp.transpose(y)
        elif j in (3, 4):
            cs = cs_ref[...]
            sn = sn_ref[...]
            for h in range(PROJ_TN // LANE):
                xh = acc[:, h * LANE:(h + 1) * LANE]
                r = pltpu.roll(xh, LANE // 2, 1)
                y = xh * cs + r * sn
                lo_col = j * PROJ_TN + h * LANE
                o_ref[:, lo_col:lo_col + LANE] = y * kb_scale if j == 4 else y
        else:
            o_ref[:, cols] = acc
            if j == 2 and kv_t_refs:
                kv_t_refs[1][...] = jnp.transpose(acc)


def _proj_call(x2d, g, w_bf, cs, sn, qg, kg, gm, *, tm, pos_blocks, kb_scale, kv_t_batch=None):
    n, d = x2d.shape
    pw = w_bf.shape[1]
    const = lambda i: (0, 0)
    out_specs = [pl.BlockSpec((tm, pw), lambda i: (i, 0))]
    out_shape = [jax.ShapeDtypeStruct((n, pw), F32)]
    if kv_t_batch is not None:
        seq = n // kv_t_batch
        spb = seq // tm
        for _ in range(2):
            out_specs.append(pl.BlockSpec((None, PROJ_TN, tm), lambda i: (i // spb, 0, i % spb)))
            out_shape.append(jax.ShapeDtypeStruct((kv_t_batch, PROJ_TN, seq), F32))
    return pl.pallas_call(
        functools.partial(_proj_kernel, kb_scale=kb_scale),
        grid=(n // tm,),
        in_specs=[
            pl.BlockSpec((tm, d), lambda i: (i, 0)),
            pl.BlockSpec((1, d), const),
            pl.BlockSpec((d, pw), const, pipeline_mode=pl.Buffered(1)),
            pl.BlockSpec((tm, LANE), lambda i: (i % pos_blocks, 0)),
            pl.BlockSpec((tm, LANE), lambda i: (i % pos_blocks, 0)),
            pl.BlockSpec((1, PROJ_TN), const),
            pl.BlockSpec((1, PROJ_TN), const),
            pl.BlockSpec((PROJ_TN, PROJ_TN), const),
        ],
        out_specs=out_specs,
        out_shape=out_shape,
        compiler_params=_cparams(("arbitrary",), VMEM_LIMIT),
    )(x2d, g, w_bf, cs, sn, qg, kg, gm)


def _sb_prompt_kernel(bias_ref, q_ref, k_ref, v_ref, tri_ref, o_ref, qs_ref, acc_ref, car_ref, *, scale):
    i = pl.program_id(1)
    tq = q_ref.shape[0]
    pairs = qs_ref.shape[0]
    lane = lax.broadcasted_iota(I32, (1, LANE), 1)
    hmask = [lane < LANE // 2, lane >= LANE // 2]
    first = lax.broadcasted_iota(I32, (1, 2 * LANE), 1) < LANE

    def split_heads(x):
        return jnp.concatenate([jnp.where(hmask[0], x, 0.0), jnp.where(hmask[1], x, 0.0)], axis=0).astype(BF16)

    for t in range(pairs):
        qs_ref[t] = (q_ref[:, t * LANE:(t + 1) * LANE] * scale).astype(BF16)
    tri2 = tri_ref[...]
    acc_ref[...] = jnp.zeros_like(acc_ref)
    car_ref[...] = jnp.zeros_like(car_ref)
    row = lax.broadcasted_iota(I32, (tq, 2 * LANE), 0)
    col = lax.broadcasted_iota(I32, (tq, 2 * LANE), 1) & (LANE - 1)
    causal2 = col < row

    def block(kbs, mask2):
        starts = [pl.multiple_of(kb * LANE, LANE) for kb in kbs]
        tiles = [slice(t * LANE, (t + 1) * LANE) for t in range(pairs)]
        units = [(n, t) for n in range(len(kbs)) for t in range(pairs)]
        z = {}
        for n, t in units:
            rk = split_heads(k_ref[pl.ds(starts[n], LANE), tiles[t]])
            bias2 = jnp.where(first, bias_ref[2 * t], bias_ref[2 * t + 1])
            z[n, t] = lax.dot_general(qs_ref[t], rk, NT_DIMS, preferred_element_type=F32) + bias2
        sp = {u: jnp.maximum(z[u], 0.0) + jnp.log(1.0 + jnp.exp(-jnp.abs(z[u]))) for u in units}
        spm = sp if mask2 is None else {u: jnp.where(mask2, sp[u], 0.0) for u in units}
        cs = {}
        for u in units:
            hi = spm[u].astype(BF16)
            lo = (spm[u] - hi.astype(F32)).astype(BF16)
            cs[u] = jnp.dot(hi, tri2, preferred_element_type=F32) + jnp.dot(lo, tri2, preferred_element_type=F32)
        a = {}
        for t in range(pairs):
            car = car_ref[t]
            for n in range(len(kbs)):
                at = jnp.exp(z[n, t] - sp[n, t] - cs[n, t] - car)
                a[n, t] = at if mask2 is None else jnp.where(mask2, at, 0.0)
                tot = [jnp.broadcast_to(jnp.sum(spm[n, t][:, h * LANE:(h + 1) * LANE], axis=1, keepdims=True),
                                        (tq, LANE)) for h in range(2)]
                car = car + jnp.concatenate(tot, axis=1)
            car_ref[t] = car
        for t in range(pairs):
            contrib = None
            for n in range(len(kbs)):
                rv = split_heads(v_ref[pl.ds(starts[n], LANE), tiles[t]])
                d = jnp.dot(a[n, t].astype(BF16), rv, preferred_element_type=F32)
                contrib = d if contrib is None else contrib + d
            acc_ref[:, tiles[t]] += contrib

    block([i], causal2)

    def body(jj, c):
        block([i - 1 - 2 * jj, i - 2 - 2 * jj], None)
        return c

    lax.fori_loop(0, lax.shift_right_logical(i, 1), body, 0)

    @pl.when((i & 1) == 1)
    def _():
        block([0], None)

    o_ref[...] = acc_ref[...]


def _sb_prompt_call(proj, bias, *, batch, seq, heads, head_dim):
    assert 2 * head_dim == LANE
    tq = LANE
    nq = seq // tq
    width = heads * head_dim
    pairs = heads // 2
    tri2 = jnp.kron(jnp.eye(2, dtype=F32), jnp.tril(jnp.ones((LANE, LANE), F32), -1)).astype(BF16)
    return pl.pallas_call(
        functools.partial(_sb_prompt_kernel, scale=head_dim ** -0.5),
        grid=(batch, nq),
        in_specs=[
            pl.BlockSpec(memory_space=pltpu.SMEM),
            pl.BlockSpec((tq, width), lambda b, i: (b * nq + i, 0)),
            pl.BlockSpec((seq, width), lambda b, i: (b, 1)),
            pl.BlockSpec((seq, width), lambda b, i: (b, 2)),
            pl.BlockSpec((2 * LANE, 2 * LANE), lambda b, i: (0, 0)),
        ],
        out_specs=pl.BlockSpec((tq, width), lambda b, i: (b * nq + i, 0)),
        out_shape=jax.ShapeDtypeStruct((batch * seq, width), F32),
        scratch_shapes=[
            pltpu.VMEM((pairs, tq, LANE), BF16),
            pltpu.VMEM((tq, width), F32),
            pltpu.VMEM((pairs, tq, 2 * LANE), F32),
        ],
        compiler_params=_cparams(("arbitrary", "arbitrary"), VMEM_LIMIT),
    )(bias, proj, proj, proj, tri2)


def _sb_sample_kernel(pt_ref, q_ref, kn_ref, vn_ref, *rest, scale, pages_per_step):
    kc_refs = rest[:pages_per_step]
    vc_refs = rest[pages_per_step:2 * pages_per_step]
    tri_ref, bias_ref, o_ref, qh_ref, nk_ref, nv_ref, acc_ref, car_ref = rest[2 * pages_per_step:]
    p = pl.program_id(1)
    heads, rq, hd = qh_ref.shape
    tri = tri_ref[...]
    bias = bias_ref[...]
    nn_dims = (((1,), (0,)), ((), ()))

    def blocks(load_k, load_v, n_blocks, mask, token_major):
        qk_dims, av_dims = (NT_DIMS, nn_dims) if token_major else (nn_dims, NT_DIMS)
        z = []
        for j in range(n_blocks):
            zj = [lax.dot_general(qh_ref[h], load_k(j, h).astype(BF16), qk_dims, preferred_element_type=F32)
                  for h in range(heads)]
            z.append(jnp.concatenate(zj, axis=0) + bias)
        sp = [jnp.maximum(zj, 0.0) + jnp.log(1.0 + jnp.exp(-jnp.abs(zj))) for zj in z]
        spm = sp if mask is None else [jnp.where(mask, s, 0.0) for s in sp]
        cs = []
        for s in spm:
            hi = s.astype(BF16)
            lo = (s - hi.astype(F32)).astype(BF16)
            cs.append(jnp.dot(hi, tri, preferred_element_type=F32) + jnp.dot(lo, tri, preferred_element_type=F32))
        car = car_ref[...]
        totals = [None] * heads
        for j in range(n_blocks):
            a = jnp.exp(z[j] - sp[j] - cs[j][:, :LANE] - car)
            if mask is not None:
                a = jnp.where(mask, a, 0.0)
            car = car + cs[j][:, LANE:]
            for h in range(heads):
                d = lax.dot_general(a[h * rq:(h + 1) * rq, :].astype(BF16), load_v(j, h).astype(BF16), av_dims,
                                    preferred_element_type=F32)
                totals[h] = d if totals[h] is None else totals[h] + d
        car_ref[...] = car
        for h in range(heads):
            acc_ref[h * rq:(h + 1) * rq, :] += totals[h]

    @pl.when(p == 0)
    def _():
        nk_ref[...] = jnp.zeros_like(nk_ref)
        nv_ref[...] = jnp.zeros_like(nv_ref)
        for h in range(heads):
            cols = slice(h * hd, (h + 1) * hd)
            qh_ref[h] = (q_ref[:, cols] * scale).astype(BF16)
            nk_ref[h, 0:rq, :] = kn_ref[:, cols]
            nv_ref[h, 0:rq, :] = vn_ref[:, cols]
        acc_ref[...] = jnp.zeros_like(acc_ref)
        car_ref[...] = jnp.zeros_like(car_ref)
        qrow = lax.broadcasted_iota(I32, (heads * rq, LANE), 0) & (rq - 1)
        col = lax.broadcasted_iota(I32, (heads * rq, LANE), 1)
        blocks(lambda j, h: nk_ref[h], lambda j, h: nv_ref[h], 1, col < qrow, True)

    blocks(lambda j, h: kc_refs[j][h], lambda j, h: vc_refs[j][h], pages_per_step, None, False)

    @pl.when(p == pl.num_programs(1) - 1)
    def _():
        for h in range(heads):
            o_ref[:, h * hd:(h + 1) * hd] = acc_ref[h * rq:(h + 1) * rq, :]


def _sb_sample_call(page_table, proj_s, cache_k3, cache_v3, tri, bias_col, *, dec_batch, rows_pad, heads, head_dim):
    n_pages = page_table.shape[1]
    width = heads * head_dim
    page = cache_k3.shape[3]
    rows = rows_pad * heads
    pps = max(c for c in (16, 8, 4, 2, 1) if n_pages % c == 0)

    def page_spec(j):
        return pl.BlockSpec((None, heads, head_dim, page),
                            lambda b, p, pt: (pt[b * n_pages + n_pages - 1 - (p * pps + j)], 0, 0, 0))

    grid_spec = pltpu.PrefetchScalarGridSpec(
        num_scalar_prefetch=1,
        grid=(dec_batch, n_pages // pps),
        in_specs=[
            pl.BlockSpec((rows_pad, width), lambda b, p, pt: (b, 0)),
            pl.BlockSpec((rows_pad, width), lambda b, p, pt: (b, 1)),
            pl.BlockSpec((rows_pad, width), lambda b, p, pt: (b, 2)),
            *[page_spec(j) for j in range(pps)],
            *[page_spec(j) for j in range(pps)],
            pl.BlockSpec((LANE, 2 * LANE), lambda b, p, pt: (0, 0)),
            pl.BlockSpec((rows, 1), lambda b, p, pt: (0, 0)),
        ],
        out_specs=pl.BlockSpec((rows_pad, width), lambda b, p, pt: (b, 0)),
        scratch_shapes=[
            pltpu.VMEM((heads, rows_pad, head_dim), BF16),
            pltpu.VMEM((heads, page, head_dim), F32),
            pltpu.VMEM((heads, page, head_dim), F32),
            pltpu.VMEM((rows, head_dim), F32),
            pltpu.VMEM((rows, LANE), F32),
        ],
    )
    return pl.pallas_call(
        functools.partial(_sb_sample_kernel, scale=head_dim ** -0.5, pages_per_step=pps),
        grid_spec=grid_spec,
        out_shape=jax.ShapeDtypeStruct((dec_batch * rows_pad, width), F32),
        compiler_params=_cparams(("arbitrary", "arbitrary"), VMEM_LIMIT),
    )(page_table.reshape(-1), proj_s, proj_s, proj_s, *([cache_k3] * pps), *([cache_v3] * pps), tri, bias_col)


def _ret_kernel(q_ref, k_ref, v0_ref, v1_ref, g0_ref, g1_ref, st0_ref, dec_ref, qd_ref, kd_ref, sd_ref, rg_ref,
                ob_ref, st_ref, state_ref, qp_ref, kp_ref, vp_ref, *, use_init):
    c = pl.program_id(1)
    heads, dk, dv = state_ref.shape
    rows = q_ref.shape[0]
    chunk = dec_ref.shape[1]
    half = v0_ref.shape[1]

    @pl.when(c == 0)
    def _():
        if use_init:
            state_ref[...] = st0_ref[...]
        else:
            state_ref[...] = jnp.zeros_like(state_ref)

    if rows == chunk:
        q, k = q_ref[...], k_ref[...]
        v = jnp.concatenate([v0_ref[...], v1_ref[...]], axis=1)
    else:
        qp_ref[...] = jnp.zeros_like(qp_ref)
        kp_ref[...] = jnp.zeros_like(kp_ref)
        vp_ref[...] = jnp.zeros_like(vp_ref)
        qp_ref[0:rows, :] = q_ref[...]
        kp_ref[0:rows, :] = k_ref[...]
        vp_ref[0:rows, 0:half] = v0_ref[...]
        vp_ref[0:rows, half:2 * half] = v1_ref[...]
        q, k, v = qp_ref[...], kp_ref[...], vp_ref[...]
    gb = jnp.concatenate([g0_ref[...], g1_ref[...]], axis=1)

    qb = [q[:, h * dk:(h + 1) * dk].astype(BF16) for h in range(heads)]
    vb = [v[:, h * dv:(h + 1) * dv].astype(BF16) for h in range(heads)]
    st = [state_ref[h] for h in range(heads)]
    scores = [lax.dot_general(qb[h], k[:, h * dk:(h + 1) * dk].astype(BF16), NT_DIMS, preferred_element_type=F32)
              * dec_ref[h] for h in range(heads)]
    cross = [jnp.dot(qb[h], st[h].astype(BF16), preferred_element_type=F32) * qd_ref[h] for h in range(heads)]
    kv = [lax.dot_general((k[:, h * dk:(h + 1) * dk] * kd_ref[h]).astype(BF16), vb[h], TN_DIMS,
                          preferred_element_type=F32) for h in range(heads)]
    inner = [jnp.dot(scores[h].astype(BF16), vb[h], preferred_element_type=F32) for h in range(heads)]
    for h in range(heads):
        state_ref[h] = sd_ref[h] * st[h] + kv[h]
        o = (inner[h] + cross[h])[0:rows, :]
        ms = jnp.mean(o * o, axis=-1, keepdims=True)
        g = gb[:, h * dv:(h + 1) * dv]
        ob_ref[:, h * dv:(h + 1) * dv] = (o * lax.rsqrt(ms + RMS_EPS) * rg_ref[h]) * (g * jax.nn.sigmoid(g))

    @pl.when(c == pl.num_programs(1) - 1)
    def _():
        st_ref[...] = state_ref[...]


def _ret_call(proj, state0, dec, qd, kd, sd, rg, *, batch, n_chunks, rows, heads, dk, dv, col0):
    chunk = dec.shape[1]
    qw = heads * dk
    assert heads * dv == 2 * qw and col0 % qw == 0
    qc = col0 // qw
    use_init = state0 is not None
    if state0 is None:
        state0 = jnp.zeros((1, heads, dk, dv), F32)
        st_map = lambda b, c: (0, 0, 0, 0)
    else:
        st_map = lambda b, c: (b, 0, 0, 0)
    pad_rows = chunk if rows != chunk else SUBLANE
    row = lambda j: (lambda b, c: (b * n_chunks + c, qc + j))
    const = lambda b, c: (0, 0, 0)
    return pl.pallas_call(
        functools.partial(_ret_kernel, use_init=use_init),
        grid=(batch, n_chunks),
        in_specs=[
            *[pl.BlockSpec((rows, qw), row(j)) for j in range(6)],
            pl.BlockSpec((None, heads, dk, dv), st_map),
            pl.BlockSpec((heads, chunk, chunk), const),
            pl.BlockSpec((heads, chunk, 1), const),
            pl.BlockSpec((heads, chunk, 1), const),
            pl.BlockSpec((heads, 1, 1), const),
            pl.BlockSpec((heads, 1, dv), const),
        ],
        out_specs=[
            pl.BlockSpec((rows, heads * dv), lambda b, c: (b * n_chunks + c, 0)),
            pl.BlockSpec((None, heads, dk, dv), lambda b, c: (b, 0, 0, 0)),
        ],
        out_shape=[
            jax.ShapeDtypeStruct((batch * n_chunks * rows, heads * dv), F32),
            jax.ShapeDtypeStruct((batch, heads, dk, dv), F32),
        ],
        scratch_shapes=[
            pltpu.VMEM((heads, dk, dv), F32),
            pltpu.VMEM((pad_rows, qw), F32),
            pltpu.VMEM((pad_rows, qw), F32),
            pltpu.VMEM((pad_rows, 2 * qw), F32),
        ],
        compiler_params=_cparams(("arbitrary", "arbitrary"), VMEM_LIMIT),
    )(proj, proj, proj, proj, proj, proj, state0, dec, qd, kd, sd, rg)


def _merge_kernel(x_ref, oa_ref, ob_ref, ga0_ref, ga1_ref, gb0_ref, gb1_ref, wa_ref, wb_ref, wo_ref, gf_ref,
                  wq_ref, x1_ref, h2_ref, qp_ref):
    ya = jnp.dot(oa_ref[...].astype(BF16), wa_ref[...], preferred_element_type=F32)
    yb = jnp.dot(ob_ref[...].astype(BF16), wb_ref[...], preferred_element_type=F32)
    ga = jnp.concatenate([ga0_ref[...], ga1_ref[...]], axis=1)
    gb = jnp.concatenate([gb0_ref[...], gb1_ref[...]], axis=1)
    merged = jax.nn.sigmoid(ga) * ya + jax.nn.sigmoid(gb) * yb
    x1 = x_ref[...] + jnp.dot(merged.astype(BF16), wo_ref[...], preferred_element_type=F32)
    x1_ref[...] = x1
    ms = jnp.mean(x1 * x1, axis=-1, keepdims=True)
    h2 = x1 * lax.rsqrt(ms + RMS_EPS) * gf_ref[...]
    h2_ref[...] = h2
    qp_ref[...] = jnp.dot(h2.astype(BF16), wq_ref[...], preferred_element_type=F32)


def _merge_call(x2d, oa, ob, proj, wa, wb, wo, gf, wq, *, tm, gate_col0):
    n, d = x2d.shape
    half = d // 2
    g0 = gate_col0 // half
    row = lambda i: (i, 0)
    const = lambda i: (0, 0)
    return pl.pallas_call(
        _merge_kernel,
        grid=(n // tm,),
        in_specs=[
            pl.BlockSpec((tm, d), row),
            pl.BlockSpec((tm, oa.shape[1]), row),
            pl.BlockSpec((tm, ob.shape[1]), row),
            pl.BlockSpec((tm, half), lambda i: (i, g0)),
            pl.BlockSpec((tm, half), lambda i: (i, g0 + 1)),
            pl.BlockSpec((tm, half), lambda i: (i, g0 + 2)),
            pl.BlockSpec((tm, half), lambda i: (i, g0 + 3)),
            pl.BlockSpec(wa.shape, const),
            pl.BlockSpec(wb.shape, const),
            pl.BlockSpec(wo.shape, const),
            pl.BlockSpec((1, d), const),
            pl.BlockSpec(wq.shape, const),
        ],
        out_specs=[
            pl.BlockSpec((tm, d), row),
            pl.BlockSpec((tm, d), row),
            pl.BlockSpec((tm, wq.shape[1]), row),
        ],
        out_shape=[
            jax.ShapeDtypeStruct((n, d), F32),
            jax.ShapeDtypeStruct((n, d), F32),
            jax.ShapeDtypeStruct((n, wq.shape[1]), F32),
        ],
        compiler_params=_cparams(("arbitrary",), VMEM_LIMIT),
    )(x2d, oa, ob, proj, proj, proj, proj, wa, wb, wo, gf, wq)


def _topk_rows(s, codes, k):
    t = s.shape[1]
    out_row = lax.broadcasted_iota(I32, (k, t), 0)
    vals = jnp.zeros((k, t), F32)
    picked = jnp.zeros((k, t), F32)
    for r in range(k):
        m = jnp.max(s, axis=0, keepdims=True)
        am = jnp.min(jnp.where(s == m, codes, jnp.inf), axis=0, keepdims=True)
        vals = jnp.where(out_row == r, m, vals)
        picked = jnp.where(out_row == r, am, picked)
        s = jnp.where(codes == am, -jnp.inf, s)
    return vals, picked


def _select_rows(table, sel):
    out = jnp.zeros(sel.shape, table.dtype)
    for a in range(table.shape[0]):
        out = jnp.where(sel == a, table[a:a + 1, :], out)
    return out


def _pair_candidates(s1, s2):
    k, t = s1.shape
    blk = SUBLANE
    a_lo = lax.broadcasted_iota(I32, (blk, t), 0)
    vals, codes = [], []
    for a0 in range(0, k, blk):
        vals.append(s1[a0:a0 + blk, :] + s2[0:1, :])
        codes.append((a_lo + a0) * k)
    for b in range(1, blk):
        ok = a_lo < k // (b + 1)
        vals.append(jnp.where(ok, s1[0:blk, :] + s2[b:b + 1, :], -jnp.inf))
        codes.append(a_lo * k + b)
    for b0 in range(blk, k, blk):
        vals.append(s1[0:1, :] + s2[b0:b0 + blk, :])
        codes.append(a_lo + b0)
    return jnp.concatenate(vals, axis=0), jnp.concatenate(codes, axis=0).astype(F32)


def _peer_topk_kernel(qp_ref, keys_ref, e_ref, g_ref, *, n_keys):
    heads = keys_ref.shape[0]
    half = keys_ref.shape[3]
    k = PEER_TOPK
    t = qp_ref.shape[0]
    key_codes = lax.broadcasted_iota(I32, (n_keys, t), 0).astype(F32)
    e_rows, g_rows = [], []
    for h in range(heads):
        tops = []
        for c in range(2):
            col = (h * 2 + c) * half
            q = qp_ref[:, col:col + half].astype(BF16)
            st = lax.dot_general(keys_ref[h, c].astype(BF16), q, NT_DIMS, preferred_element_type=F32)
            tops.append(_topk_rows(st, key_codes, k))
        (s1, i1), (s2, i2) = tops
        cand, codes = _pair_candidates(s1, s2)
        top, flat = _topk_rows(cand, codes, k)
        flat = flat.astype(I32)
        ia = _select_rows(i1, flat // k)
        ib = _select_rows(i2, flat % k)
        e_rows.append(((ia * n_keys + ib) * EXPERT_ROWS).astype(I32))
        ex = jnp.exp(top - jnp.max(top, axis=0, keepdims=True))
        g_rows.append(ex / jnp.sum(ex, axis=0, keepdims=True))
    e_ref[...] = jnp.transpose(jnp.concatenate(e_rows, axis=0))
    g_ref[...] = jnp.transpose(jnp.concatenate(g_rows, axis=0))


def _peer_topk_call(qp, keys, *, tm):
    n = qp.shape[0]
    heads, _, n_keys, half = keys.shape
    slots = heads * PEER_TOPK
    return pl.pallas_call(
        functools.partial(_peer_topk_kernel, n_keys=n_keys),
        grid=(n // tm,),
        in_specs=[
            pl.BlockSpec((tm, qp.shape[1]), lambda i: (i, 0)),
            pl.BlockSpec(keys.shape, lambda i: (0, 0, 0, 0)),
        ],
        out_specs=[
            pl.BlockSpec((tm, slots), lambda i: (i, 0)),
            pl.BlockSpec((tm, slots), lambda i: (i, 0)),
        ],
        out_shape=[
            jax.ShapeDtypeStruct((n, slots), I32),
            jax.ShapeDtypeStruct((n, slots), F32),
        ],
        compiler_params=_cparams(("arbitrary",), VMEM_LIMIT),
    )(qp, keys)


EXPERT_ROWS = 4


def _pack_table(tab):
    n, d = tab.shape
    assert d == 2 * EXPERT_ROWS * LANE
    bits = lax.bitcast_convert_type(tab.astype(BF16), jnp.uint16).astype(U32)
    packed = bits[:, :d // 2] | (bits[:, d // 2:] << 16)
    return packed.reshape(n * EXPERT_ROWS, LANE)


PACKED_ROWS = 2 * EXPERT_ROWS
GROUPS_PER_ITER = 4


def _for_token_groups(n, group, groups_per_iter=GROUPS_PER_ITER):
    per = max(c for c in range(1, groups_per_iter + 1) if n % c == 0)

    def body(i, carry):
        for u in range(per):
            group(i * per + u, carry)
        return carry

    lax.fori_loop(0, n // per, body, 0)


IDX_WAYS = 4


def _split_index(e):
    n, slots = e.shape
    e3 = e.reshape(n, slots // IDX_WAYS, IDX_WAYS)
    return [e3[:, :, j].reshape(-1) for j in range(IDX_WAYS)]


def _gather_rows(tab_ref, idx_refs, t, slots):
    per_way = slots // IDX_WAYS
    rows = []
    for k in range(per_way):
        pos = t * per_way + k
        for j in range(IDX_WAYS):
            off = pl.multiple_of(idx_refs[j][pos], EXPERT_ROWS)
            rows.append(tab_ref[pl.ds(off, EXPERT_ROWS), :])
    return jnp.concatenate(rows, axis=0)


def _pack_bf16_pair(lo, hi):
    lo_bits = lax.bitcast_convert_type(lo.astype(BF16).astype(F32), U32) >> 16
    hi_bits = lax.bitcast_convert_type(hi.astype(BF16).astype(F32), U32) & jnp.uint32(0xFFFF0000)
    return lo_bits | hi_bits


def _peer_u_kernel(*refs):
    idx_refs = refs[:IDX_WAYS]
    h_ref, g_ref, tab_ref, fold_ref, w_ref = refs[IDX_WAYS:]
    tb, slots = g_ref.shape
    cols = slots * PACKED_ROWS
    sub = lax.broadcasted_iota(I32, (SUBLANE, cols), 0)
    lane = lax.broadcasted_iota(I32, (SUBLANE, cols), 1)
    diag = (lane & (PACKED_ROWS - 1)) == sub
    fold = fold_ref[...]

    def group(t8, carry):
        rows = pl.ds(pl.multiple_of(t8 * SUBLANE, SUBLANE), SUBLANE)
        h8 = h_ref[rows, :]
        half = h8.shape[1] // 2
        packed = _pack_bf16_pair(h8[:, :half], h8[:, half:])
        v8 = jnp.zeros((SUBLANE, cols), F32)
        for s in range(SUBLANE):
            t = t8 * SUBLANE + s
            words = _gather_rows(tab_ref, idx_refs, t, slots)
            hw = jnp.concatenate([packed[s:s + 1, j * LANE:(j + 1) * LANE] for j in range(EXPERT_ROWS)], axis=0)
            m = lax.dot_general(pltpu.bitcast(hw, BF16), pltpu.bitcast(words, BF16), NT_DIMS,
                                preferred_element_type=F32)
            v = jnp.sum(jnp.where(diag, m, 0.0), axis=0, keepdims=True)
            v8 = jnp.where(sub == s, v, v8)
        hi = v8.astype(BF16)
        lo = (v8 - hi.astype(F32)).astype(BF16)
        act = jnp.dot(hi, fold, preferred_element_type=F32) + jnp.dot(lo, fold, preferred_element_type=F32)
        w_ref[rows, :] = g_ref[rows, :] * jax.nn.gelu(act)
        return carry

    _for_token_groups(tb // SUBLANE, group)


def _index_specs(tb, slots):
    per_way = slots // IDX_WAYS
    return [pl.BlockSpec((tb * per_way,), lambda i: (i,), memory_space=pltpu.SMEM) for _ in range(IDX_WAYS)]


def _peer_u_call(idx_ways, h2, g, tab, *, tb):
    n, slots = g.shape
    d = h2.shape[1]
    fold = jnp.repeat(jnp.eye(slots, dtype=BF16), PACKED_ROWS, axis=0)
    return pl.pallas_call(
        _peer_u_kernel,
        grid=(n // tb,),
        in_specs=[
            *_index_specs(tb, slots),
            pl.BlockSpec((tb, d), lambda i: (i, 0)),
            pl.BlockSpec((tb, slots), lambda i: (i, 0)),
            pl.BlockSpec(tab.shape, lambda i: (0, 0), pipeline_mode=pl.Buffered(1)),
            pl.BlockSpec(fold.shape, lambda i: (0, 0)),
        ],
        out_specs=pl.BlockSpec((tb, slots), lambda i: (i, 0)),
        out_shape=jax.ShapeDtypeStruct((n, slots), F32),
        compiler_params=_cparams(("arbitrary",), VMEM_LIMIT),
    )(*idx_ways, h2, g, tab, fold)


def _peer_v_kernel(*refs):
    idx_refs = refs[:IDX_WAYS]
    w_ref, x_ref, tab_ref, expand_ref, o_ref = refs[IDX_WAYS:]
    tb, slots = w_ref.shape
    cols = slots * PACKED_ROWS
    sub = lax.broadcasted_iota(I32, (SUBLANE, cols), 0)
    j = lax.broadcasted_iota(I32, (SUBLANE, cols), 1) & (PACKED_ROWS - 1)
    probe = pltpu.bitcast(jnp.full((EXPERT_ROWS, LANE), 0x40003F80, U32), BF16).astype(F32)
    low_first = probe[0:1, 0:1] == 1.0
    half = jnp.where(low_first, j & 1, 1 - (j & 1))
    place = (half * EXPERT_ROWS + (j >> 1)) == sub
    expand = expand_ref[...]

    def group(t8, carry):
        rows8 = pl.ds(pl.multiple_of(t8 * SUBLANE, SUBLANE), SUBLANE)
        wrep = jnp.dot(w_ref[rows8, :].astype(BF16), expand, preferred_element_type=F32)
        outs = []
        for s in range(SUBLANE):
            t = t8 * SUBLANE + s
            words = _gather_rows(tab_ref, idx_refs, t, slots)
            lhs = jnp.where(place, wrep[s:s + 1, :], 0.0).astype(BF16)
            outs.append(jnp.dot(lhs, pltpu.bitcast(words, BF16), preferred_element_type=F32))
        peer = jnp.concatenate(
            [jnp.concatenate([outs[s][i:i + 1, :] for s in range(SUBLANE)], axis=0) for i in range(PACKED_ROWS)],
            axis=1)
        o_ref[rows8, :] = x_ref[rows8, :] + peer
        return carry

    _for_token_groups(tb // SUBLANE, group, 2 * GROUPS_PER_ITER)


def _peer_v_call(idx_ways, w, x1, tab, *, tb):
    n, slots = w.shape
    d = x1.shape[1]
    assert d == PACKED_ROWS * LANE
    expand = jnp.repeat(jnp.eye(slots, dtype=BF16), PACKED_ROWS, axis=1)
    return pl.pallas_call(
        _peer_v_kernel,
        grid=(n // tb,),
        in_specs=[
            *_index_specs(tb, slots),
            pl.BlockSpec((tb, slots), lambda i: (i, 0)),
            pl.BlockSpec((tb, d), lambda i: (i, 0)),
            pl.BlockSpec(tab.shape, lambda i: (0, 0), pipeline_mode=pl.Buffered(1)),
            pl.BlockSpec(expand.shape, lambda i: (0, 0)),
        ],
        out_specs=pl.BlockSpec((tb, d), lambda i: (i, 0)),
        out_shape=jax.ShapeDtypeStruct((n, d), F32),
        compiler_params=_cparams(("arbitrary",), VMEM_LIMIT),
    )(*idx_ways, w, x1, tab, expand)


def _ple_kernel(x_ref, p_ref, g_ref, wg_ref, wp_ref, o_ref):
    x = x_ref[...]
    ms = jnp.mean(x * x, axis=-1, keepdims=True)
    hn = (x * lax.rsqrt(ms + RMS_EPS) * g_ref[...]).astype(BF16)
    gate = jax.nn.sigmoid(jnp.dot(hn, wg_ref[...], preferred_element_type=F32))
    pe = jnp.dot(p_ref[...].astype(BF16), wp_ref[...], preferred_element_type=F32)
    o_ref[...] = x + gate * pe


def _ple_call(x2, p, g, wg, wp, *, tm):
    n, d = x2.shape
    return pl.pallas_call(
        _ple_kernel,
        grid=(n // tm,),
        in_specs=[
            pl.BlockSpec((tm, d), lambda i: (i, 0)),
            pl.BlockSpec((tm, p.shape[1]), lambda i: (i, 0)),
            pl.BlockSpec((1, d), lambda i: (0, 0)),
            pl.BlockSpec(wg.shape, lambda i: (0, 0)),
            pl.BlockSpec(wp.shape, lambda i: (0, 0)),
        ],
        out_specs=pl.BlockSpec((tm, d), lambda i: (i, 0)),
        out_shape=jax.ShapeDtypeStruct((n, d), F32),
        compiler_params=_cparams(("arbitrary",), VMEM_LIMIT),
    )(x2, p, g, wg, wp)


def _rope_tables(pos, dim):
    half = dim // 2
    inv = ROPE_BASE ** (-jnp.arange(half, dtype=F32) / half)
    ang = pos.astype(F32)[:, None] * inv[None, :]
    cos, sin = jnp.cos(ang), jnp.sin(ang)
    return jnp.concatenate([cos, cos], axis=1), jnp.concatenate([-sin, sin], axis=1)


def _ret_tables(heads, chunk, n_valid):
    lg = jnp.log1p(-jnp.exp2(-5.0 - jnp.arange(heads, dtype=F32)))
    idx = jnp.arange(chunk, dtype=F32)
    diff = idx[:, None] - idx[None, :]
    causal = diff >= 0
    dec = jnp.where(causal[None], jnp.exp(jnp.where(causal, diff, 0.0)[None] * lg[:, None, None]), 0.0)
    qd = jnp.exp((idx[None, :] + 1.0) * lg[:, None])
    kd = jnp.exp((n_valid - 1.0 - idx)[None, :] * lg[:, None])
    kd = jnp.where(idx[None, :] < n_valid, kd, 0.0)
    sd = jnp.exp(n_valid * lg)
    return dec, qd[:, :, None], kd[:, :, None], sd[:, None, None]


def _token_tail(x2d, oa, ob, proj, p2d, lw, *, tm, tb, gate_col0):
    x1, h2, qp = _merge_call(x2d, oa, ob, proj, lw["wa"], lw["wb"], lw["wo"], lw["gf"], lw["wq"],
                             tm=tm, gate_col0=gate_col0)
    e, g = _peer_topk_call(qp, lw["keys"], tm=tm)
    idx_ways = _split_index(e)
    w = _peer_u_call(idx_ways, h2, g, lw["u_tab"], tb=tb)
    x2 = _peer_v_call(idx_ways, w, x1, lw["v_tab"], tb=tb)
    return _ple_call(x2, p2d, lw["gp"], lw["wg"], lw["wp"], tm=tm)


def kernel(x_prompt, x_sample, cache_k, cache_v, state_ret, page_table, p_prompt, p_sample, norm_mix_g, w_in,
           q_norm_g, k_norm_g, sb_bias, ret_norm_g, w_a_up, w_b_up, w_out, norm_ffn_g, peer_w_q, peer_sub_keys,
           peer_u, peer_v, norm_ple_g, w_ple_gate, w_ple):
    batch, seq, d = x_prompt.shape
    dec_batch, dec_seq, _ = x_sample.shape
    depth, n_phys, page, sb_heads, sb_dim = cache_k.shape
    _, _, ret_heads, dk, dv = state_ret.shape
    sb_width = sb_heads * sb_dim
    past_len = page_table.shape[1] * page
    chunk = LANE
    rows_pad = SUBLANE
    assert dec_seq <= rows_pad and dk == LANE and sb_width % LANE == 0
    qb_col0 = 3 * sb_width
    gate_col0 = qb_col0 + 2 * ret_heads * dk + 2 * ret_heads * dv

    xp = x_prompt.reshape(batch * seq, d)
    xs = jnp.pad(x_sample, ((0, 0), (0, rows_pad - dec_seq), (0, 0))).reshape(dec_batch * rows_pad, d)
    ns = dec_batch * rows_pad

    cs_p, sn_p = _rope_tables(jnp.arange(seq), dk)
    cs_s, sn_s = _rope_tables(past_len + jnp.arange(rows_pad), dk)
    cs_s, sn_s = jnp.tile(cs_s, (dec_batch, 1)), jnp.tile(sn_s, (dec_batch, 1))
    assert sb_dim & (sb_dim - 1) == 0 and sb_width == PROJ_TN
    gm = jnp.kron(jnp.eye(sb_heads, dtype=F32), jnp.full((sb_dim, sb_dim), 1.0 / sb_dim, F32)).astype(BF16)
    tri = jnp.concatenate([jnp.tril(jnp.ones((LANE, LANE), F32), -1), jnp.ones((LANE, LANE), F32)], axis=1).astype(BF16)
    dec_p = _ret_tables(ret_heads, chunk, float(chunk))
    dec_s = _ret_tables(ret_heads, chunk, float(dec_seq))

    tm = ROW_TILE
    tb = PEER_TOKEN_BLOCK
    assert (batch * seq) % tm == 0 and seq % tm == 0 and (batch * seq) % tb == 0
    kp, vp, sp, kn, vn, sn_out = [], [], [], [], [], []
    for i in range(depth):
        lw = dict(
            wa=w_a_up[i].astype(BF16), wb=w_b_up[i].astype(BF16), wo=w_out[i].astype(BF16),
            gf=norm_ffn_g[i][None, :], wq=peer_w_q[i].astype(BF16), keys=peer_sub_keys[i],
            u_tab=_pack_table(peer_u[i]), v_tab=_pack_table(peer_v[i]),
            gp=norm_ple_g[i][None, :], wg=w_ple_gate[i].astype(BF16), wp=w_ple[i].astype(BF16),
        )
        w_bf = w_in[i].astype(BF16)
        g_mix = norm_mix_g[i][None, :]
        qg = jnp.tile(q_norm_g[i], sb_heads)[None, :]
        kg = jnp.tile(k_norm_g[i], sb_heads)[None, :]
        rg = ret_norm_g[i][:, None, :]
        kb_scale = dk ** -0.5

        proj, k_t, v_t = _proj_call(xp, g_mix, w_bf, cs_p, sn_p, qg, kg, gm, tm=tm, pos_blocks=seq // tm,
                                    kb_scale=kb_scale, kv_t_batch=batch)
        oa = _sb_prompt_call(proj, sb_bias[i], batch=batch, seq=seq, heads=sb_heads, head_dim=sb_dim)
        ob, st_p = _ret_call(proj, None, *dec_p, rg, batch=batch, n_chunks=seq // chunk, rows=chunk,
                             heads=ret_heads, dk=dk, dv=dv, col0=qb_col0)
        xp = _token_tail(xp, oa, ob, proj, p_prompt[i].reshape(batch * seq, -1), lw, tm=tm, tb=tb,
                         gate_col0=gate_col0)
        kp.append(jnp.transpose(k_t.reshape(batch, sb_heads, sb_dim, seq), (0, 3, 1, 2)))
        vp.append(jnp.transpose(v_t.reshape(batch, sb_heads, sb_dim, seq), (0, 3, 1, 2)))
        sp.append(st_p)

        (proj_s,) = _proj_call(xs, g_mix, w_bf, cs_s, sn_s, qg, kg, gm, tm=ns, pos_blocks=1, kb_scale=kb_scale)
        bias_col = jnp.repeat(sb_bias[i], rows_pad)[:, None]
        oa_s = _sb_sample_call(page_table, proj_s, jnp.transpose(cache_k[i], (0, 2, 3, 1)),
                               jnp.transpose(cache_v[i], (0, 2, 3, 1)), tri, bias_col,
                               dec_batch=dec_batch, rows_pad=rows_pad, heads=sb_heads, head_dim=sb_dim)
        ob_s, st_s = _ret_call(proj_s, state_ret[i], *dec_s, rg, batch=dec_batch, n_chunks=1, rows=rows_pad,
                               heads=ret_heads, dk=dk, dv=dv, col0=qb_col0)
        p_s = jnp.pad(p_sample[i], ((0, 0), (0, rows_pad - dec_seq), (0, 0))).reshape(ns, -1)
        xs = _token_tail(xs, oa_s, ob_s, proj_s, p_s, lw, tm=ns, tb=ns, gate_col0=gate_col0)
        ka_s = proj_s[:, sb_width:2 * sb_width].reshape(dec_batch, rows_pad, sb_heads, sb_dim)[:, :dec_seq]
        va_s = proj_s[:, 2 * sb_width:3 * sb_width].reshape(dec_batch, rows_pad, sb_heads, sb_dim)[:, :dec_seq]
        kn.append(ka_s)
        vn.append(va_s)
        sn_out.append(st_s)

    y_prompt = xp.reshape(batch, seq, d)
    y_sample = xs.reshape(dec_batch, rows_pad, d)[:, :dec_seq]
    return (y_prompt, y_sample, jnp.stack(kp), jnp.stack(vp), jnp.stack(sp), jnp.stack(kn), jnp.stack(vn),
            jnp.stack(sn_out))
```
